```python
import jax, jax.numpy as jnp
from jax import lax
import numpy as np

D_MODEL = 1024
BATCH = 32
SEQ = 256
DEPTH = 2
DEC_BATCH = 4
DEC_SEQ = 2048
PAST_LEN = 256

GRID_W = 64
N_EVEN = (DEPTH + 1) // 2
N_ODD = DEPTH // 2
NA_HEADS = 8
NA_HD = 64
NA_W = NA_HEADS * NA_HD
NA_WIN_R = 8
NA_WIN_C = 16
FN_GROUPS = 4
FN_GW = 128
FN_W = FN_GROUPS * FN_GW
MIX0_IN = 3 * NA_W + FN_W
MIX0_OUT = NA_W + FN_W
HG_HEADS = 8
HG_DK = 128
HG_DV = 128
HG_KW = HG_HEADS * HG_DK
HG_VW = HG_HEADS * HG_DV
MIX1_IN = 3 * HG_KW + 2 * HG_VW
HG_CHUNK = 32
D_FF = 2816
N_SUB = 3
N_MOD = 3 * N_SUB
Q_BLOCK = 128
EPS = 1e-6

kernel_name = 'hybrid_natten_fnet_hgrn2_prefix_step'


def rms_norm(x, g):
    xf = x.astype(jnp.float32)
    y = xf * lax.rsqrt(jnp.mean(xf * xf, axis=-1, keepdims=True) + EPS)
    return (y * g.astype(jnp.float32)).astype(x.dtype)


def ada_params(cond, w, b):
    m = jax.nn.silu(cond) @ w + b
    return m.reshape(cond.shape[0], N_MOD, D_MODEL)


def modulated_norm(x, g, mod, j):
    shift = mod[:, 3 * j][:, None, :]
    scale = mod[:, 3 * j + 1][:, None, :]
    return rms_norm(x, g) * (1 + scale) + shift


def gate_of(mod, j):
    return mod[:, 3 * j + 2][:, None, :]


def swiglu(h, w1, w2):
    a, b = jnp.split(h @ w1, 2, axis=-1)
    return (jax.nn.silu(a) * b) @ w2


def half_ffn(x, mod, j, g, w1, w2):
    return x + 0.5 * gate_of(mod, j) * swiglu(modulated_norm(x, g, mod, j), w1, w2)


def context_attention(q, k, v):
    B, S, H, hd = q.shape
    nb = S // Q_BLOCK
    scale = hd ** -0.5
    qb = q.reshape(B, nb, Q_BLOCK, H, hd).transpose(1, 0, 2, 3, 4)

    def block(qi):
        s = jnp.einsum('bqhd,bkhd->bhqk', qi, k).astype(jnp.float32) * scale
        p = jax.nn.softmax(s, axis=-1).astype(v.dtype)
        return jnp.einsum('bhqk,bkhd->bqhd', p, v)

    o = lax.map(block, qb)
    return o.transpose(1, 0, 2, 3, 4).reshape(B, S, H * hd)


def neighbourhood_attention(q, k, v, ck, cv, rpb):
    B, T, H, hd = q.shape
    rows = T // GRID_W
    kr = min(NA_WIN_R, rows)
    kc = NA_WIN_C
    scale = hd ** -0.5
    qg = q.reshape(B, rows, GRID_W, H, hd)
    kg = k.reshape(B, rows, GRID_W, H, hd)
    vg = v.reshape(B, rows, GRID_W, H, hd)
    cols = jnp.arange(GRID_W)
    col_start = jnp.clip(cols - kc // 2, 0, GRID_W - kc)
    col_idx = col_start[:, None] + jnp.arange(kc)[None, :]
    col_bias_idx = col_idx - cols[:, None] + (NA_WIN_C - 1)

    def one_row(r):
        r0 = jnp.clip(r - kr // 2, 0, rows - kr)
        kb = lax.dynamic_slice_in_dim(kg, r0, kr, axis=1)
        vb = lax.dynamic_slice_in_dim(vg, r0, kr, axis=1)
        kw = kb[:, :, col_idx]
        vw = vb[:, :, col_idx]
        qr = lax.dynamic_index_in_dim(qg, r, axis=1, keepdims=False)
        row_bias_idx = r0 + jnp.arange(kr) - r + (NA_WIN_R - 1)
        bias = rpb[:, row_bias_idx[None, :, None], col_bias_idx[:, None, :]]
        s_loc = jnp.einsum('bwhd,brwkhd->bhwrk', qr, kw).astype(jnp.float32) * scale
        s_loc = (s_loc + bias[None].astype(jnp.float32)).reshape(B, H, GRID_W, kr * kc)
        s_ctx = jnp.einsum('bwhd,bphd->bhwp', qr, ck).astype(jnp.float32) * scale
        p = jax.nn.softmax(jnp.concatenate([s_loc, s_ctx], axis=-1), axis=-1).astype(v.dtype)
        p_loc = p[..., :kr * kc].reshape(B, H, GRID_W, kr, kc)
        p_ctx = p[..., kr * kc:]
        return (jnp.einsum('bhwrk,brwkhd->bwhd', p_loc, vw)
                + jnp.einsum('bhwp,bphd->bwhd', p_ctx, cv))

    o = lax.map(one_row, jnp.arange(rows))
    return o.transpose(1, 0, 2, 3, 4).reshape(B, T, H * hd)


def fourier_mix(u):
    B, T, _ = u.shape
    ug = u.reshape(B, T, FN_GROUPS, FN_GW).astype(jnp.float32)
    y = jnp.fft.fft2(ug, axes=(1, 3), norm='ortho').real
    return y.reshape(B, T, FN_W).astype(u.dtype)


def even_mixer(hp, hs, ck, cv, w_in, w_out, rpb):
    def split(u):
        B, T, _ = u.shape
        q, k, v, f = jnp.split(u, [NA_W, 2 * NA_W, 3 * NA_W], axis=-1)
        heads = lambda a: a.reshape(B, T, NA_HEADS, NA_HD)
        return heads(q), heads(k), heads(v), f

    qp, kp, vp, fp = split(hp @ w_in)
    qs, ks, vs, fs = split(hs @ w_in)
    yp = jnp.concatenate([context_attention(qp, kp, vp), fourier_mix(fp)], axis=-1) @ w_out
    ys = jnp.concatenate([neighbourhood_attention(qs, ks, vs, ck, cv, rpb), fourier_mix(fs)], axis=-1) @ w_out
    return yp, ys, kp, vp


def hgrn2_gates(z, lb):
    z = z.astype(jnp.float32)
    log_f = jnp.logaddexp(jnp.log(lb), jnp.log1p(-lb) + jax.nn.log_sigmoid(z))
    k = (1.0 - lb) * jax.nn.sigmoid(-z)
    return k, log_f


def gla_chunk_scan(q, k, v, log_f, s0):
    B, T, H, DK = q.shape
    C = HG_CHUNK
    n = T // C
    to_chunks = lambda a: a.reshape(B, n, C, H, a.shape[-1]).transpose(1, 0, 3, 2, 4)
    mask = jnp.tril(jnp.ones((C, C), dtype=bool))[:, :, None]

    def step(S, inp):
        qi, ki, vi, gi = inp
        b = jnp.cumsum(gi, axis=2)
        o_inter = jnp.einsum('bhck,bhkv->bhcv', qi * jnp.exp(b), S)
        diff = b[:, :, :, None, :] - b[:, :, None, :, :]
        decay = jnp.exp(jnp.where(mask, diff, -jnp.inf))
        A = jnp.einsum('bhtk,bhsk,bhtsk->bhts', qi, ki, decay)
        o_intra = jnp.einsum('bhts,bhsv->bhtv', A, vi)
        b_last = b[:, :, -1:, :]
        S_new = (S * jnp.exp(b_last[:, :, 0, :, None])
                 + jnp.einsum('bhsk,bhsv->bhkv', ki * jnp.exp(b_last - b), vi))
        return S_new, o_inter + o_intra

    S_fin, o = lax.scan(step, s0.astype(jnp.float32),
                        (to_chunks(q), to_chunks(k), to_chunks(v), to_chunks(log_f)))
    o = o.transpose(1, 0, 3, 2, 4).reshape(B, T, H, v.shape[-1])
    return o, S_fin


def hgrn2_bidir(u, lb_l, s0, norm_g):
    B, T, _ = u.shape
    q, zf, zb, v, g = jnp.split(u, [HG_KW, 2 * HG_KW, 3 * HG_KW, 3 * HG_KW + HG_VW], axis=-1)
    heads_k = lambda a: a.reshape(B, T, HG_HEADS, HG_DK)
    q = heads_k(q).astype(jnp.float32)
    v = v.reshape(B, T, HG_HEADS, HG_DV).astype(jnp.float32)
    kf, gf = hgrn2_gates(heads_k(zf), lb_l[0].reshape(HG_HEADS, HG_DK))
    kb, gb = hgrn2_gates(heads_k(zb), lb_l[1].reshape(HG_HEADS, HG_DK))
    rev = lambda a: jnp.flip(a, axis=1)
    o_f, s_f = gla_chunk_scan(q, kf, v, gf, s0[:, 0])
    o_b, s_b = gla_chunk_scan(rev(q), rev(kb), rev(v), rev(gb), s0[:, 1])
    o = o_f + rev(o_b)
    o = rms_norm(o, norm_g) * jax.nn.silu(g.reshape(B, T, HG_HEADS, HG_DV).astype(jnp.float32))
    return o.reshape(B, T, HG_VW).astype(u.dtype), jnp.stack([s_f, s_b], axis=1)


def setup_inputs(seed: int = 0) -> dict:
    key = jax.random.key(seed)
    ks = jax.random.split(key, 20)
    nrm = lambda k, shape, s: jax.random.normal(k, shape, jnp.float32) * s
    return {
        'x_prompt': nrm(ks[0], (BATCH, SEQ, D_MODEL), 1.0),
        'x_sample': nrm(ks[1], (DEC_BATCH, DEC_SEQ, D_MODEL), 1.0),
        'cache_k': nrm(ks[2], (DEC_BATCH, N_EVEN, PAST_LEN, NA_HEADS, NA_HD), 1.0),
        'cache_v': nrm(ks[3], (DEC_BATCH, N_EVEN, PAST_LEN, NA_HEADS, NA_HD), 1.0),
        'state_hgrn': nrm(ks[4], (DEC_BATCH, N_ODD, 2, HG_HEADS, HG_DK, HG_DV), 0.5),
        'c': nrm(ks[5], (DEC_BATCH, D_MODEL), 1.0),
        'c_ctx': nrm(ks[6], (D_MODEL,), 1.0),
        'ada_w': nrm(ks[7], (DEPTH, D_MODEL, N_MOD * D_MODEL), 0.5 * D_MODEL ** -0.5),
        'ada_b': nrm(ks[8], (DEPTH, N_MOD * D_MODEL), 0.01),
        'norm_g': 1.0 + nrm(ks[9], (DEPTH, N_SUB, D_MODEL), 0.01),
        'ffn_w1': nrm(ks[10], (DEPTH, 2, D_MODEL, 2 * D_FF), D_MODEL ** -0.5),
        'ffn_w2': nrm(ks[11], (DEPTH, 2, D_FF, D_MODEL), D_FF ** -0.5),
        'mix0_w_in': nrm(ks[12], (N_EVEN, D_MODEL, MIX0_IN), D_MODEL ** -0.5),
        'mix0_w_out': nrm(ks[13], (N_EVEN, MIX0_OUT, D_MODEL), MIX0_OUT ** -0.5),
        'na_rpb': nrm(ks[14], (N_EVEN, NA_HEADS, 2 * NA_WIN_R - 1, 2 * NA_WIN_C - 1), 0.02),
        'mix1_w_in': nrm(ks[15], (N_ODD, D_MODEL, MIX1_IN), D_MODEL ** -0.5),
        'mix1_w_out': nrm(ks[16], (N_ODD, HG_VW, D_MODEL), HG_VW ** -0.5),
        'hg_lb_logits': nrm(ks[17], (DEPTH, 2, HG_KW), 1.0),
        'hg_norm_g': 1.0 + nrm(ks[18], (N_ODD, HG_DV), 0.01),
        'norm_f': 1.0 + nrm(ks[19], (D_MODEL,), 0.01),
    }


def reference(x_prompt, x_sample, cache_k, cache_v, state_hgrn, c, c_ctx, ada_w, ada_b, norm_g,
              ffn_w1, ffn_w2, mix0_w_in, mix0_w_out, na_rpb, mix1_w_in, mix1_w_out,
              hg_lb_logits, hg_norm_g, norm_f):
    xp, xs = x_prompt, x_sample
    bp = xp.shape[0]
    sm = jax.nn.softmax(hg_lb_logits.astype(jnp.float32), axis=0)
    lb_all = jnp.cumsum(sm, axis=0) - sm[0]
    new_k, new_v, new_s = [], [], []
    for l in range(DEPTH):
        mp = ada_params(c_ctx[None, :], ada_w[l], ada_b[l])
        ms = ada_params(c, ada_w[l], ada_b[l])
        xp = half_ffn(xp, mp, 0, norm_g[l, 0], ffn_w1[l, 0], ffn_w2[l, 0])
        xs = half_ffn(xs, ms, 0, norm_g[l, 0], ffn_w1[l, 0], ffn_w2[l, 0])
        hp = modulated_norm(xp, norm_g[l, 1], mp, 1)
        hs = modulated_norm(xs, norm_g[l, 1], ms, 1)
        if l % 2 == 0:
            e = l // 2
            yp, ys, kp, vp = even_mixer(hp, hs, cache_k[:, e], cache_v[:, e],
                                        mix0_w_in[e], mix0_w_out[e], na_rpb[e])
            new_k.append(kp)
            new_v.append(vp)
        else:
            o = l // 2
            zero = jnp.zeros((bp, 2, HG_HEADS, HG_DK, HG_DV), jnp.float32)
            yp, sp = hgrn2_bidir(hp @ mix1_w_in[o], lb_all[l], zero, hg_norm_g[o])
            ys, _ = hgrn2_bidir(hs @ mix1_w_in[o], lb_all[l], state_hgrn[:, o], hg_norm_g[o])
            yp = yp @ mix1_w_out[o]
            ys = ys @ mix1_w_out[o]
            new_s.append(sp.astype(x_prompt.dtype))
        xp = xp + gate_of(mp, 1) * yp
        xs = xs + gate_of(ms, 1) * ys
        xp = half_ffn(xp, mp, 2, norm_g[l, 2], ffn_w1[l, 1], ffn_w2[l, 1])
        xs = half_ffn(xs, ms, 2, norm_g[l, 2], ffn_w1[l, 1], ffn_w2[l, 1])
    y_prompt = rms_norm(xp, norm_f)
    y_sample = rms_norm(xs, norm_f)
    new_cache_k = jnp.stack(new_k, axis=1)
    new_cache_v = jnp.stack(new_v, axis=1)
    new_state_hgrn = jnp.stack(new_s, axis=1)
    return (y_prompt, y_sample, new_cache_k, new_cache_v, new_state_hgrn)
```

```python
import functools

import numpy as np
import jax
import jax.numpy as jnp
from jax import lax
from jax.experimental import pallas as pl
from jax.experimental.pallas import tpu as pltpu

F32 = jnp.float32
BF16 = jnp.bfloat16

D_MODEL = 1024
DEPTH = 2
GRID_W = 64
NA_HEADS = 8
NA_HD = 64
NA_W = NA_HEADS * NA_HD
NA_WIN_R = 8
NA_WIN_C = 16
FN_GROUPS = 4
FN_GW = 128
FN_W = FN_GROUPS * FN_GW
HG_HEADS = 8
HG_DK = 128
HG_DV = 128
HG_KW = HG_HEADS * HG_DK
D_FF = 2816
N_MOD = 9
EPS = 1e-6

COND_ROWS = 8
CTX_ROW = 4
TOKEN_TILE = 256
SCAN_CHUNK = 128
MASK_VALUE = -1e30
VMEM_LIMIT = 56 * 1024 * 1024


def _params(n_axes, vmem=VMEM_LIMIT):
    return pltpu.CompilerParams(dimension_semantics=("arbitrary",) * n_axes, vmem_limit_bytes=vmem)


def _resident(shape):
    nd = len(shape)
    return pl.BlockSpec(shape, lambda *_: (0,) * nd, pipeline_mode=pl.Buffered(1))


def _sigmoid(a):
    return 1.0 / (1.0 + jnp.exp(-a))


def _rms_mod(x, g, shift, scale):
    y = x * lax.rsqrt(jnp.mean(x * x, axis=-1, keepdims=True) + EPS)
    return (y * g) * (1.0 + scale) + shift


def _split2(x):
    hi = x.astype(BF16)
    lo = (x - hi.astype(F32)).astype(BF16)
    return hi, lo


def _split3(x):
    hi = x.astype(BF16)
    r = x - hi.astype(F32)
    mid = r.astype(BF16)
    lo = (r - mid.astype(F32)).astype(BF16)
    return hi, mid, lo


def _dot(a, b):
    return jnp.dot(a, b, preferred_element_type=F32)


def _dot_nt(a, b):
    return lax.dot_general(a, b, (((1,), (1,)), ((), ())), preferred_element_type=F32)


def _ada_kernel(cond_ref, w_ref, b_ref, o_ref):
    c = cond_ref[...]
    s = c * _sigmoid(c)
    s_hi, s_lo = _split2(s)
    w_hi, w_lo = _split2(w_ref[0])
    o_ref[0] = _dot(s_hi, w_hi) + _dot(s_hi, w_lo) + _dot(s_lo, w_hi) + b_ref[0]


def _ada_mod(cond, ada_w, ada_b):
    depth, d, n = ada_w.shape
    tn = 1152
    out = pl.pallas_call(
        _ada_kernel,
        grid=(depth, n // tn),
        in_specs=[
            pl.BlockSpec((COND_ROWS, d), lambda l, j: (0, 0)),
            pl.BlockSpec((1, d, tn), lambda l, j: (l, 0, j)),
            pl.BlockSpec((1, 1, tn), lambda l, j: (l, 0, j)),
        ],
        out_specs=pl.BlockSpec((1, COND_ROWS, tn), lambda l, j: (l, 0, j)),
        out_shape=jax.ShapeDtypeStruct((depth, COND_ROWS, n), F32),
        compiler_params=_params(2),
        name="ada_mod",
    )(cond, ada_w, ada_b.reshape(depth, 1, n))
    return out.reshape(depth, COND_ROWS, N_MOD, d)


def _mod_spec(layer, tiles_per_seq):
    if tiles_per_seq is None:
        return pl.BlockSpec((1, 1, N_MOD, D_MODEL), lambda i: (layer, CTX_ROW, 0, 0))
    return pl.BlockSpec((1, 1, N_MOD, D_MODEL), lambda i: (layer, i // tiles_per_seq, 0, 0))


def _ffn_kernel(j, final_norm, x_ref, mod_ref, g_ref, w1a_ref, w1b_ref, w2_ref, gf_ref, o_ref):
    x = x_ref[...]
    m = mod_ref[0, 0]
    h = _rms_mod(x, g_ref[...], m[3 * j:3 * j + 1], m[3 * j + 1:3 * j + 2]).astype(BF16)
    a = _dot(h, w1a_ref[...])
    b = _dot(h, w1b_ref[...])
    act = (a * _sigmoid(a) * b).astype(BF16)
    y = x + (0.5 * m[3 * j + 2:3 * j + 3]) * _dot(act, w2_ref[...])
    if final_norm:
        y = y * lax.rsqrt(jnp.mean(y * y, axis=-1, keepdims=True) + EPS) * gf_ref[...]
    o_ref[...] = y


def _half_ffn(x, mod, layer, tiles_per_seq, j, g, w1a, w1b, w2, gf, final_norm):
    n, d = x.shape
    tm = TOKEN_TILE
    return pl.pallas_call(
        functools.partial(_ffn_kernel, j, final_norm),
        grid=(n // tm,),
        in_specs=[
            pl.BlockSpec((tm, d), lambda i: (i, 0)),
            _mod_spec(layer, tiles_per_seq),
            _resident((1, d)),
            _resident(w1a.shape),
            _resident(w1b.shape),
            _resident(w2.shape),
            _resident((1, d)),
        ],
        out_specs=pl.BlockSpec((tm, d), lambda i: (i, 0)),
        out_shape=jax.ShapeDtypeStruct((n, d), F32),
        compiler_params=_params(1),
        name="half_ffn",
    )(x, mod, g.reshape(1, d), w1a, w1b, w2, gf.reshape(1, d))


def _proj_kernel(j, splits, x_ref, mod_ref, g_ref, w_ref, *o_refs):
    m = mod_ref[0, 0]
    h = _rms_mod(x_ref[...], g_ref[...], m[3 * j:3 * j + 1], m[3 * j + 1:3 * j + 2]).astype(BF16)
    u = _dot(h, w_ref[...])
    for (lo, hi, scale), o_ref in zip(splits, o_refs):
        piece = u[:, lo:hi]
        if scale != 1.0:
            piece = piece * scale
        o_ref[...] = piece.astype(o_ref.dtype)


def _in_proj(x, mod, layer, tiles_per_seq, g, w, outs):
    n, d = x.shape
    tm = TOKEN_TILE
    splits, lo = [], 0
    for width, _, scale in outs:
        splits.append((lo, lo + width, scale))
        lo += width
    return pl.pallas_call(
        functools.partial(_proj_kernel, 1, tuple(splits)),
        grid=(n // tm,),
        in_specs=[
            pl.BlockSpec((tm, d), lambda i: (i, 0)),
            _mod_spec(layer, tiles_per_seq),
            _resident((1, d)),
            _resident(w.shape),
        ],
        out_specs=[pl.BlockSpec((tm, width), lambda i: (i, 0)) for width, _, _ in outs],
        out_shape=[jax.ShapeDtypeStruct((n, width), dt) for width, dt, _ in outs],
        compiler_params=_params(1),
        name="mixer_in_proj",
    )(x, mod, g.reshape(1, d), w)


def _out_proj_kernel(n_acts, x_ref, mod_ref, *refs):
    a_refs, w_refs, o_ref = refs[:n_acts], refs[n_acts:2 * n_acts], refs[2 * n_acts]
    y = None
    for a_ref, w_ref in zip(a_refs, w_refs):
        t = _dot(a_ref[...], w_ref[...])
        y = t if y is None else y + t
    o_ref[...] = x_ref[...] + mod_ref[0, 0][5:6] * y


def _out_proj(x, mod, layer, tiles_per_seq, acts, weights):
    n, d = x.shape
    tm = TOKEN_TILE
    return pl.pallas_call(
        functools.partial(_out_proj_kernel, len(acts)),
        grid=(n // tm,),
        in_specs=([pl.BlockSpec((tm, d), lambda i: (i, 0)), _mod_spec(layer, tiles_per_seq)]
                  + [pl.BlockSpec((tm, a.shape[1]), lambda i: (i, 0)) for a in acts]
                  + [_resident(w.shape) for w in weights]),
        out_specs=pl.BlockSpec((tm, d), lambda i: (i, 0)),
        out_shape=jax.ShapeDtypeStruct((n, d), F32),
        compiler_params=_params(1),
        name="mixer_out_proj",
    )(x, mod, *acts, *weights)


def _head_pair_attention(q, key_blocks, value_blocks, bias_blocks):
    lane = lax.broadcasted_iota(jnp.int32, (1, 2 * NA_HD), 1)
    outs = []
    for hh in range(2):
        own = (lane < NA_HD) if hh == 0 else (lane >= NA_HD)
        qm = jnp.where(own, q, jnp.zeros_like(q))
        scores = []
        for kb, bias in zip(key_blocks, bias_blocks):
            s = _dot_nt(qm, kb)
            if bias is not None:
                s = s + bias[hh]
            scores.append(s)
        mx = scores[0].max(axis=-1, keepdims=True)
        for s in scores[1:]:
            mx = jnp.maximum(mx, s.max(axis=-1, keepdims=True))
        den, acc = None, None
        for s, vb in zip(scores, value_blocks):
            p = jnp.exp(s - mx)
            dsum = p.sum(axis=-1, keepdims=True)
            pv = _dot(p.astype(BF16), vb)
            den = dsum if den is None else den + dsum
            acc = pv if acc is None else acc + pv
        outs.append(acc / den)
    return jnp.where(lane < NA_HD, outs[0], outs[1])


def _ctx_attn_kernel(q_ref, k_ref, v_ref, o_ref):
    for pr in range(NA_HEADS // 2):
        sl = slice(pr * 2 * NA_HD, (pr + 1) * 2 * NA_HD)
        o = _head_pair_attention(q_ref[0, :, sl], [k_ref[0, :, sl].astype(BF16)],
                                 [v_ref[0, :, sl].astype(BF16)], [None])
        o_ref[0, :, sl] = o.astype(o_ref.dtype)


def _context_attention(q, k, v):
    b, s, w = q.shape
    blk = lambda: pl.BlockSpec((1, s, w), lambda i: (i, 0, 0))
    return pl.pallas_call(
        _ctx_attn_kernel,
        grid=(b,),
        in_specs=[blk(), blk(), blk()],
        out_specs=blk(),
        out_shape=jax.ShapeDtypeStruct((b, s, w), BF16),
        compiler_params=_params(1),
        name="context_attention",
    )(q, k, v)


def _na_kernel(rows, q_ref, k_ref, v_ref, ck_ref, cv_ref, bias_ref, o_ref):
    r = pl.program_id(1)
    r0 = jnp.clip(r - NA_WIN_R // 2, 0, rows - NA_WIN_R)
    start = pl.multiple_of(r0 * GRID_W, GRID_W)
    win = pl.ds(start, NA_WIN_R * GRID_W)
    for pr in range(NA_HEADS // 2):
        sl = slice(pr * 2 * NA_HD, (pr + 1) * 2 * NA_HD)
        bias = [bias_ref[0, 2 * pr], bias_ref[0, 2 * pr + 1]]
        o = _head_pair_attention(q_ref[0, :, sl],
                                 [k_ref[0, win, sl], ck_ref[0, :, sl]],
                                 [v_ref[0, win, sl], cv_ref[0, :, sl]],
                                 [bias, None])
        o_ref[0, :, sl] = o.astype(o_ref.dtype)


def _na_bias_table(rpb):
    cols = np.arange(GRID_W)
    col_start = np.clip(cols - NA_WIN_C // 2, 0, GRID_W - NA_WIN_C)
    c = np.arange(GRID_W)
    inside = (c[None, :] >= col_start[:, None]) & (c[None, :] < col_start[:, None] + NA_WIN_C)
    col_idx = np.clip(c[None, :] - cols[:, None] + (NA_WIN_C - 1), 0, 2 * NA_WIN_C - 2)
    d = np.arange(NA_WIN_R)
    j = np.arange(NA_WIN_R)
    row_idx = j[None, :] - d[:, None] + (NA_WIN_R - 1)
    t = rpb[:, row_idx[:, :, None, None], col_idx[None, None, :, :]]
    t = jnp.where(inside[None, None, None], t, MASK_VALUE)
    t = t.transpose(1, 0, 3, 2, 4)
    return t.reshape(NA_WIN_R, NA_HEADS, GRID_W, NA_WIN_R * GRID_W).astype(F32)


def _neighbourhood_attention(q, k, v, ck, cv, bias):
    b, t, w = q.shape
    rows = t // GRID_W
    p = ck.shape[1]
    half = NA_WIN_R // 2
    seq = lambda n: pl.BlockSpec((1, n, w), lambda i, r: (i, 0, 0))
    row = lambda: pl.BlockSpec((1, GRID_W, w), lambda i, r: (i, r, 0))
    bias_spec = pl.BlockSpec(
        (1, NA_HEADS, GRID_W, NA_WIN_R * GRID_W),
        lambda i, r: (r - jnp.clip(r - half, 0, rows - NA_WIN_R), 0, 0, 0))
    return pl.pallas_call(
        functools.partial(_na_kernel, rows),
        grid=(b, rows),
        in_specs=[row(), seq(t), seq(t), seq(p), seq(p), bias_spec],
        out_specs=row(),
        out_shape=jax.ShapeDtypeStruct((b, t, w), BF16),
        compiler_params=_params(2),
        name="neighbourhood_attention",
    )(q, k, v, ck, cv, bias)


def _dft_tables(t):
    def cs(n):
        idx = (np.arange(n)[:, None] * np.arange(n)[None, :]) % n
        ang = 2.0 * np.pi * idx.astype(np.float64) / n
        return np.cos(ang), np.sin(ang)
    ct, st = cs(t)
    cg, sg = cs(FN_GW)
    return (np.concatenate([ct, -st], axis=1).astype(np.float32),
            np.concatenate([cg, sg], axis=1).astype(np.float32))


def _fnet_kernel(t, scale, f_ref, cgsg_ref, dft_ref, o_ref, xcs_ref):
    @pl.when(pl.program_id(1) == 0)
    def _():
        for g in range(FN_GROUPS):
            sl = slice(g * FN_GW, (g + 1) * FN_GW)
            xcs = _dot(f_ref[0, :, sl], cgsg_ref[...])
            xcs_ref[0:t, sl] = xcs[:, :FN_GW].astype(BF16)
            xcs_ref[t:2 * t, sl] = xcs[:, FN_GW:].astype(BF16)

    o_ref[0] = (_dot(dft_ref[...], xcs_ref[...]) * scale).astype(o_ref.dtype)


def _fourier_mix(f, dft, cgsg):
    b, t, w = f.shape
    tq = min(t, TOKEN_TILE)
    scale = float(1.0 / np.sqrt(float(t * FN_GW)))
    return pl.pallas_call(
        functools.partial(_fnet_kernel, t, scale),
        grid=(b, t // tq),
        in_specs=[
            pl.BlockSpec((1, t, w), lambda i, r: (i, 0, 0)),
            pl.BlockSpec(cgsg.shape, lambda i, r: (0, 0)),
            pl.BlockSpec((tq, 2 * t), lambda i, r: (r, 0)),
        ],
        out_specs=pl.BlockSpec((1, tq, w), lambda i, r: (i, r, 0)),
        out_shape=jax.ShapeDtypeStruct((b, t, w), BF16),
        scratch_shapes=[pltpu.VMEM((2 * t, w), BF16)],
        compiler_params=_params(2),
        name="fourier_mix",
    )(f, cgsg, dft)


def _ref_rows(b, b_ref, m, reverse):
    c = SCAN_CHUNK
    blk = 2 * m
    anchor = m if reverse else m - 1
    if blk >= 8:
        pieces = [jnp.broadcast_to(b_ref[p * blk + anchor:p * blk + anchor + 1, :], (blk, HG_DK))
                  for p in range(c // blk)]
        return pieces[0] if len(pieces) == 1 else jnp.concatenate(pieces, axis=0)
    pos = lax.broadcasted_iota(jnp.int32, (c, HG_DK), 0) % blk
    r = b
    for p in range(blk):
        delta = anchor - p
        if delta != 0:
            r = jnp.where(pos == p, pltpu.roll(b, (-delta) % c, 0), r)
    return r


def _scan_chunk(reverse, q, z, v, lb, st, b_ref):
    c = SCAN_CHUNK
    e = jnp.exp(-jnp.abs(z))
    inv = 1.0 / (1.0 + e)
    pos_z = z >= 0
    sig = jnp.where(pos_z, inv, e * inv)
    sig_neg = jnp.where(pos_z, e * inv, inv)
    k = (1.0 - lb) * sig_neg
    log_f = jnp.log(lb + (1.0 - lb) * sig)

    row = lax.broadcasted_iota(jnp.int32, (c, c), 0)
    col = lax.broadcasted_iota(jnp.int32, (c, c), 1)
    tri = jnp.where((col >= row) if reverse else (col <= row), 1.0, 0.0).astype(BF16)
    b = None
    for part in _split3(log_f):
        t = _dot(tri, part)
        b = t if b is None else b + t
    b_ref[...] = b
    edge = 0 if reverse else c - 1
    b_tot = b_ref[edge:edge + 1, :]

    st_b = st.astype(BF16)
    o = _dot_nt((q * jnp.exp(b)).astype(BF16), st_b)

    tok = lax.broadcasted_iota(jnp.int32, (c, HG_DK), 0)
    xr = row ^ col
    a = None
    m = c // 2
    while m >= 1:
        is_q = ((tok // m) % 2) == (0 if reverse else 1)
        r = _ref_rows(b, b_ref, m, reverse)
        d = jnp.exp(jnp.where(is_q, b - r, r - b))
        qe = jnp.where(is_q, q * d, 0.0).astype(BF16)
        ke = jnp.where(is_q, 0.0, k * d).astype(BF16)
        lvl = _dot_nt(qe, ke)
        a = lvl if a is None else jnp.where(xr < 2 * m, lvl, a)
        m //= 2
    a = jnp.where(xr == 0, _dot_nt(q.astype(BF16), k.astype(BF16)), a)
    o = o + _dot(a.astype(BF16), v.astype(BF16))

    k_end = (k * jnp.exp(b_tot - b)).astype(BF16)
    st_new = st * jnp.exp(b_tot) + _dot(v.T.astype(BF16), k_end)
    return o, st_new


def _hgrn_kernel(reverse, layer, n_sub, has_s0, combine, *refs):
    refs = list(refs)
    q_ref, z_ref, v_ref, lg_ref = refs[:4]
    del refs[:4]
    s0_ref = refs.pop(0) if has_s0 else None
    if combine:
        of_ref, gate_ref, ng_ref = refs[:3]
        del refs[:3]
    o_ref, sfin_ref, st_ref, b_ref = refs

    ti = pl.program_id(2)
    n_t = pl.num_programs(2)

    @pl.when(ti == 0)
    def _():
        if has_s0:
            st_ref[...] = s0_ref[0, 0].T
        else:
            st_ref[...] = jnp.zeros_like(st_ref)

    lg = lg_ref[:, 0, 0, :]
    ex = jnp.exp(lg - lg.max(axis=0, keepdims=True))
    lb = ex[1:layer + 1].sum(axis=0, keepdims=True) / ex.sum(axis=0, keepdims=True)

    for i in range(n_sub):
        sub = (n_sub - 1 - i) if reverse else i
        rows = slice(sub * SCAN_CHUNK, (sub + 1) * SCAN_CHUNK)
        o, st_new = _scan_chunk(reverse, q_ref[rows, :], z_ref[rows, :], v_ref[rows, :], lb,
                                st_ref[...], b_ref)
        st_ref[...] = st_new
        if combine:
            o = o + of_ref[rows, :]
            o = o * lax.rsqrt(jnp.mean(o * o, axis=-1, keepdims=True) + EPS) * ng_ref[...]
            g = gate_ref[rows, :]
            o = o * (g * _sigmoid(g))
        o_ref[rows, :] = o.astype(o_ref.dtype)

    @pl.when(ti == n_t - 1)
    def _():
        sfin_ref[0, 0] = st_ref[...].T


def _hgrn_scan(reverse, layer, seqs, q, z, v, logits, s0, combine):
    n = q.shape[0]
    t = n // seqs
    tc = min(t, 512)
    n_t = t // tc
    n_sub = tc // SCAN_CHUNK
    direction = 1 if reverse else 0

    def tok_map(s, h, i):
        return (s * n_t + ((n_t - 1 - i) if reverse else i), h)

    tok = lambda: pl.BlockSpec((tc, HG_DK), tok_map)
    in_specs = [tok(), tok(), tok(),
                pl.BlockSpec((DEPTH, 1, 1, HG_DK), lambda s, h, i: (0, direction * HG_HEADS + h, 0, 0))]
    args = [q, z, v, logits.reshape(DEPTH, 2 * HG_HEADS, 1, HG_DK)]
    if s0 is not None:
        in_specs.append(pl.BlockSpec((1, 1, HG_DK, HG_DV), lambda s, h, i: (s, h, 0, 0)))
        args.append(s0)
    if combine is not None:
        o_fwd, gate, ng = combine
        in_specs += [tok(), tok(), pl.BlockSpec((1, HG_DV), lambda s, h, i: (0, 0))]
        args += [o_fwd, gate, ng.reshape(1, HG_DV)]
    return pl.pallas_call(
        functools.partial(_hgrn_kernel, reverse, layer, n_sub, s0 is not None, combine is not None),
        grid=(seqs, HG_HEADS, n_t),
        in_specs=in_specs,
        out_specs=[tok(), pl.BlockSpec((1, 1, HG_DK, HG_DV), lambda s, h, i: (s, h, 0, 0))],
        out_shape=[jax.ShapeDtypeStruct((n, HG_KW), BF16 if combine is not None else F32),
                   jax.ShapeDtypeStruct((seqs, HG_HEADS, HG_DK, HG_DV), F32)],
        scratch_shapes=[pltpu.VMEM((HG_DV, HG_DK), F32), pltpu.VMEM((SCAN_CHUNK, HG_DK), F32)],
        compiler_params=_params(3),
        name="hgrn2_scan_rev" if reverse else "hgrn2_scan_fwd",
    )(*args)


def _hgrn_bidir(layer, seqs, u, logits, s0, norm_g):
    q, zf, zb, v, g = u
    o_f, s_f = _hgrn_scan(False, layer, seqs, q, zf, v, logits, None if s0 is None else s0[:, 0], None)
    y, s_b = _hgrn_scan(True, layer, seqs, q, zb, v, logits, None if s0 is None else s0[:, 1],
                        (o_f, g, norm_g))
    return y, s_f, s_b


def kernel(x_prompt, x_sample, cache_k, cache_v, state_hgrn, c, c_ctx, ada_w, ada_b, norm_g, ffn_w1, ffn_w2,
           mix0_w_in, mix0_w_out, na_rpb, mix1_w_in, mix1_w_out, hg_lb_logits, hg_norm_g, norm_f):
    bp, sp, d = x_prompt.shape
    bs, ts, _ = x_sample.shape
    xp = x_prompt.reshape(bp * sp, d)
    xs = x_sample.reshape(bs * ts, d)
    tiles_s = ts // TOKEN_TILE

    cond = jnp.zeros((COND_ROWS, d), F32).at[:bs].set(c).at[CTX_ROW].set(c_ctx)
    mod = _ada_mod(cond, ada_w, ada_b)

    w1a = ffn_w1[..., :D_FF].astype(BF16)
    w1b = ffn_w1[..., D_FF:].astype(BF16)
    w2 = ffn_w2.astype(BF16)

    new_k, new_v, new_s = [], [], []
    for l in range(DEPTH):
        last = l == DEPTH - 1

        def ffn(x, tiles, j, half, final):
            return _half_ffn(x, mod, l, tiles, j, norm_g[l, j], w1a[l, half], w1b[l, half], w2[l, half],
                             norm_f, final)

        xp = ffn(xp, None, 0, 0, False)
        xs = ffn(xs, tiles_s, 0, 0, False)

        if l % 2 == 0:
            e = l // 2
            w_in = mix0_w_in[e].astype(BF16)
            w_out = mix0_w_out[e].astype(BF16)
            qk_scale = float(NA_HD) ** -0.5
            qp, kp, vp, fp = _in_proj(xp, mod, l, None, norm_g[l, 1], w_in,
                                      [(NA_W, BF16, qk_scale), (NA_W, F32, 1.0), (NA_W, F32, 1.0), (FN_W, BF16, 1.0)])
            qs, ks, vs, fs = _in_proj(xs, mod, l, tiles_s, norm_g[l, 1], w_in,
                                      [(NA_W, BF16, qk_scale), (NA_W, BF16, 1.0), (NA_W, BF16, 1.0), (FN_W, BF16, 1.0)])
            new_k.append(kp.reshape(bp, sp, NA_HEADS, NA_HD))
            new_v.append(vp.reshape(bp, sp, NA_HEADS, NA_HD))

            r3 = lambda a, b_, t_: a.reshape(b_, t_, a.shape[-1])
            ap = _context_attention(r3(qp, bp, sp), r3(kp, bp, sp), r3(vp, bp, sp))
            ck = cache_k[:, e].reshape(bs, -1, NA_W).astype(BF16)
            cv = cache_v[:, e].reshape(bs, -1, NA_W).astype(BF16)
            a_s = _neighbourhood_attention(r3(qs, bs, ts), r3(ks, bs, ts), r3(vs, bs, ts), ck, cv,
                                           _na_bias_table(na_rpb[e]))
            cgsg = None
            fn = []
            for f3 in (r3(fp, bp, sp), r3(fs, bs, ts)):
                dft_np, cgsg_np = _dft_tables(f3.shape[1])
                fn.append(_fourier_mix(f3, jnp.asarray(dft_np).astype(BF16), jnp.asarray(cgsg_np).astype(BF16)))
            w_halves = [w_out[:NA_W], w_out[NA_W:]]
            xp = _out_proj(xp, mod, l, None, [ap.reshape(bp * sp, NA_W), fn[0].reshape(bp * sp, FN_W)], w_halves)
            xs = _out_proj(xs, mod, l, tiles_s, [a_s.reshape(bs * ts, NA_W), fn[1].reshape(bs * ts, FN_W)], w_halves)
        else:
            o = l // 2
            w_in = mix1_w_in[o].astype(BF16)
            w_out = mix1_w_out[o].astype(BF16)
            outs = [(HG_KW, F32, 1.0)] * 5
            up = _in_proj(xp, mod, l, None, norm_g[l, 1], w_in, outs)
            us = _in_proj(xs, mod, l, tiles_s, norm_g[l, 1], w_in, outs)
            yp, sf, sb = _hgrn_bidir(l, bp, up, hg_lb_logits, None, hg_norm_g[o])
            ys, _, _ = _hgrn_bidir(l, bs, us, hg_lb_logits, state_hgrn[:, o], hg_norm_g[o])
            new_s.append(jnp.stack([sf, sb], axis=1))
            xp = _out_proj(xp, mod, l, None, [yp], [w_out])
            xs = _out_proj(xs, mod, l, tiles_s, [ys], [w_out])

        xp = ffn(xp, None, 2, 1, last)
        xs = ffn(xs, tiles_s, 2, 1, last)

    return (xp.reshape(bp, sp, d), xs.reshape(bs, ts, d),
            jnp.stack(new_k, axis=1), jnp.stack(new_v, axis=1), jnp.stack(new_s, axis=1))
```

```python
import functools

import numpy as np
import jax
import jax.numpy as jnp
from jax import lax
from jax.experimental import pallas as pl
from jax.experimental.pallas import tpu as pltpu

F32 = jnp.float32
BF16 = jnp.bfloat16

D_MODEL = 1024
DEPTH = 2
GRID_W = 64
NA_HEADS = 8
NA_HD = 64
NA_W = NA_HEADS * NA_HD
NA_WIN_R = 8
NA_WIN_C = 16
FN_GROUPS = 4
FN_GW = 128
FN_W = FN_GROUPS * FN_GW
HG_HEADS = 8
HG_DK = 128
HG_DV = 128
HG_KW = HG_HEADS * HG_DK
D_FF = 2816
N_MOD = 9
EPS = 1e-6

COND_ROWS = 8
CTX_ROW = 4
TOKEN_TILE = 256
SCAN_CHUNK = 128
MASK_VALUE = -1e30
NEG_BIG = -1e30
VMEM_LIMIT = 56 * 1024 * 1024


def _params(n_axes, vmem=VMEM_LIMIT):
    return pltpu.CompilerParams(dimension_semantics=("arbitrary",) * n_axes, vmem_limit_bytes=vmem)


def _resident(shape):
    nd = len(shape)
    return pl.BlockSpec(shape, lambda *_: (0,) * nd, pipeline_mode=pl.Buffered(1))


def _sigmoid(a):
    return 1.0 / (1.0 + jnp.exp(-a))


def _rms_mod(x, g, shift, scale):
    y = x * lax.rsqrt(jnp.mean(x * x, axis=-1, keepdims=True) + EPS)
    return (y * g) * (1.0 + scale) + shift


def _split2(x):
    hi = x.astype(BF16)
    lo = (x - hi.astype(F32)).astype(BF16)
    return hi, lo


def _split3(x):
    hi = x.astype(BF16)
    r = x - hi.astype(F32)
    mid = r.astype(BF16)
    lo = (r - mid.astype(F32)).astype(BF16)
    return hi, mid, lo


def _dot(a, b):
    return jnp.dot(a, b, preferred_element_type=F32)


def _dot_nt(a, b):
    return lax.dot_general(a, b, (((1,), (1,)), ((), ())), preferred_element_type=F32)


def _ada_kernel(cond_ref, w_ref, b_ref, o_ref):
    c = cond_ref[...]
    s = c * _sigmoid(c)
    s_hi, s_lo = _split2(s)
    w_hi, w_lo = _split2(w_ref[0])
    o_ref[0] = _dot(s_hi, w_hi) + _dot(s_hi, w_lo) + _dot(s_lo, w_hi) + b_ref[0]


def _ada_mod(cond, ada_w, ada_b):
    depth, d, n = ada_w.shape
    tn = 1152
    out = pl.pallas_call(
        _ada_kernel,
        grid=(depth, n // tn),
        in_specs=[
            pl.BlockSpec((COND_ROWS, d), lambda l, j: (0, 0)),
            pl.BlockSpec((1, d, tn), lambda l, j: (l, 0, j)),
            pl.BlockSpec((1, 1, tn), lambda l, j: (l, 0, j)),
        ],
        out_specs=pl.BlockSpec((1, COND_ROWS, tn), lambda l, j: (l, 0, j)),
        out_shape=jax.ShapeDtypeStruct((depth, COND_ROWS, n), F32),
        compiler_params=_params(2),
        name="ada_mod",
    )(cond, ada_w, ada_b.reshape(depth, 1, n))
    return out.reshape(depth, COND_ROWS, N_MOD, d)


def _mod_spec(layer, tiles_per_seq):
    if tiles_per_seq is None:
        return pl.BlockSpec((1, 1, N_MOD, D_MODEL), lambda i: (layer, CTX_ROW, 0, 0))
    return pl.BlockSpec((1, 1, N_MOD, D_MODEL), lambda i: (layer, i // tiles_per_seq, 0, 0))


def _ffn_kernel(j, final_norm, x_ref, mod_ref, g_ref, w1a_ref, w1b_ref, w2_ref, gf_ref, o_ref):
    x = x_ref[...]
    m = mod_ref[0, 0]
    h = _rms_mod(x, g_ref[...], m[3 * j:3 * j + 1], m[3 * j + 1:3 * j + 2]).astype(BF16)
    a = _dot(h, w1a_ref[...])
    b = _dot(h, w1b_ref[...])
    act = (a * _sigmoid(a) * b).astype(BF16)
    y = x + (0.5 * m[3 * j + 2:3 * j + 3]) * _dot(act, w2_ref[...])
    if final_norm:
        y = y * lax.rsqrt(jnp.mean(y * y, axis=-1, keepdims=True) + EPS) * gf_ref[...]
    o_ref[...] = y


def _half_ffn(x, mod, layer, tiles_per_seq, j, half, g, w1, w2, gf, final_norm):
    n, d = x.shape
    tm = TOKEN_TILE
    once = pl.Buffered(1)
    return pl.pallas_call(
        functools.partial(_ffn_kernel, j, final_norm),
        grid=(n // tm,),
        in_specs=[
            pl.BlockSpec((tm, d), lambda i: (i, 0)),
            _mod_spec(layer, tiles_per_seq),
            _resident((1, d)),
            pl.BlockSpec((None, None, d, D_FF), lambda i: (layer, half, 0, 0), pipeline_mode=once),
            pl.BlockSpec((None, None, d, D_FF), lambda i: (layer, half, 0, 1), pipeline_mode=once),
            pl.BlockSpec((None, None, D_FF, d), lambda i: (layer, half, 0, 0), pipeline_mode=once),
            _resident((1, d)),
        ],
        out_specs=pl.BlockSpec((tm, d), lambda i: (i, 0)),
        out_shape=jax.ShapeDtypeStruct((n, d), F32),
        compiler_params=_params(1),
        name="half_ffn",
    )(x, mod, g.reshape(1, d), w1, w1, w2, gf.reshape(1, d))


def _proj_kernel(j, splits, x_ref, mod_ref, g_ref, w_ref, *o_refs):
    m = mod_ref[0, 0]
    h = _rms_mod(x_ref[...], g_ref[...], m[3 * j:3 * j + 1], m[3 * j + 1:3 * j + 2]).astype(BF16)
    u = _dot(h, w_ref[...])
    for (lo, hi, scale), o_ref in zip(splits, o_refs):
        piece = u[:, lo:hi]
        if scale != 1.0:
            piece = piece * scale
        o_ref[...] = piece.astype(o_ref.dtype)


def _in_proj(x, mod, layer, tiles_per_seq, g, w, outs):
    n, d = x.shape
    tm = TOKEN_TILE
    splits, lo = [], 0
    for width, _, scale in outs:
        splits.append((lo, lo + width, scale))
        lo += width
    return pl.pallas_call(
        functools.partial(_proj_kernel, 1, tuple(splits)),
        grid=(n // tm,),
        in_specs=[
            pl.BlockSpec((tm, d), lambda i: (i, 0)),
            _mod_spec(layer, tiles_per_seq),
            _resident((1, d)),
            _resident(w.shape),
        ],
        out_specs=[pl.BlockSpec((tm, width), lambda i: (i, 0)) for width, _, _ in outs],
        out_shape=[jax.ShapeDtypeStruct((n, width), dt) for width, dt, _ in outs],
        compiler_params=_params(1),
        name="mixer_in_proj",
    )(x, mod, g.reshape(1, d), w)


def _out_proj_kernel(n_acts, x_ref, mod_ref, *refs):
    a_refs, w_refs, o_ref = refs[:n_acts], refs[n_acts:2 * n_acts], refs[2 * n_acts]
    y = None
    for a_ref, w_ref in zip(a_refs, w_refs):
        t = _dot(a_ref[...], w_ref[...])
        y = t if y is None else y + t
    o_ref[...] = x_ref[...] + mod_ref[0, 0][5:6] * y


def _out_proj(x, mod, layer, tiles_per_seq, acts, w):
    n, d = x.shape
    tm = TOKEN_TILE
    width = acts[0].shape[1]
    w_specs = [pl.BlockSpec((width, d), functools.partial(lambda i, blk: (blk, 0), blk=blk),
                            pipeline_mode=pl.Buffered(1)) for blk in range(len(acts))]
    return pl.pallas_call(
        functools.partial(_out_proj_kernel, len(acts)),
        grid=(n // tm,),
        in_specs=([pl.BlockSpec((tm, d), lambda i: (i, 0)), _mod_spec(layer, tiles_per_seq)]
                  + [pl.BlockSpec((tm, width), lambda i: (i, 0)) for _ in acts] + w_specs),
        out_specs=pl.BlockSpec((tm, d), lambda i: (i, 0)),
        out_shape=jax.ShapeDtypeStruct((n, d), F32),
        compiler_params=_params(1),
        name="mixer_out_proj",
    )(x, mod, *acts, *([w] * len(acts)))


def _head_pair_attention(q, key_blocks, value_blocks, bias_blocks):
    lane = lax.broadcasted_iota(jnp.int32, (1, 2 * NA_HD), 1)
    outs = []
    for hh in range(2):
        own = (lane < NA_HD) if hh == 0 else (lane >= NA_HD)
        qm = jnp.where(own, q, jnp.zeros_like(q))
        scores = []
        for kb, bias in zip(key_blocks, bias_blocks):
            s = _dot_nt(qm, kb)
            if bias is not None:
                s = s + bias[hh]
            scores.append(s)
        mx = scores[0].max(axis=-1, keepdims=True)
        for s in scores[1:]:
            mx = jnp.maximum(mx, s.max(axis=-1, keepdims=True))
        den, acc = None, None
        for s, vb in zip(scores, value_blocks):
            p = jnp.exp(s - mx)
            dsum = p.sum(axis=-1, keepdims=True)
            pv = _dot(p.astype(BF16), vb)
            den = dsum if den is None else den + dsum
            acc = pv if acc is None else acc + pv
        outs.append(acc / den)
    return jnp.where(lane < NA_HD, outs[0], outs[1])


def _ctx_attn_kernel(q_ref, k_ref, v_ref, o_ref):
    for pr in range(NA_HEADS // 2):
        sl = slice(pr * 2 * NA_HD, (pr + 1) * 2 * NA_HD)
        o = _head_pair_attention(q_ref[0, :, sl], [k_ref[0, :, sl].astype(BF16)],
                                 [v_ref[0, :, sl].astype(BF16)], [None])
        o_ref[0, :, sl] = o.astype(o_ref.dtype)


def _context_attention(q, k, v):
    b, s, w = q.shape
    blk = lambda: pl.BlockSpec((1, s, w), lambda i: (i, 0, 0))
    return pl.pallas_call(
        _ctx_attn_kernel,
        grid=(b,),
        in_specs=[blk(), blk(), blk()],
        out_specs=blk(),
        out_shape=jax.ShapeDtypeStruct((b, s, w), BF16),
        compiler_params=_params(1),
        name="context_attention",
    )(q, k, v)


def _na_kernel(rows, q_ref, k_ref, v_ref, ck_ref, cv_ref, bias_ref, o_ref):
    r = pl.program_id(1)
    r0 = jnp.clip(r - NA_WIN_R // 2, 0, rows - NA_WIN_R)
    start = pl.multiple_of(r0 * GRID_W, GRID_W)
    win = pl.ds(start, NA_WIN_R * GRID_W)
    first = (NA_WIN_R - 1) - (r - r0)
    for pr in range(NA_HEADS // 2):
        sl = slice(pr * 2 * NA_HD, (pr + 1) * 2 * NA_HD)
        bias = [jnp.concatenate([bias_ref[2 * pr + hh, first + 2 * jj] for jj in range(NA_WIN_R // 2)], axis=1)
                for hh in range(2)]
        o = _head_pair_attention(q_ref[0, :, sl],
                                 [k_ref[0, win, sl], ck_ref[0, :, sl]],
                                 [v_ref[0, win, sl], cv_ref[0, :, sl]],
                                 [bias, None])
        o_ref[0, :, sl] = o.astype(o_ref.dtype)


def _na_bias_table(rpb):
    cols = np.arange(GRID_W)
    col_start = np.clip(cols - NA_WIN_C // 2, 0, GRID_W - NA_WIN_C)
    inside = (cols[None, :] >= col_start[:, None]) & (cols[None, :] < col_start[:, None] + NA_WIN_C)
    pad = GRID_W - NA_WIN_C
    padded = jnp.pad(rpb.astype(F32), ((0, 0), (0, 0), (pad, pad)))
    t = jnp.stack([padded[:, :, GRID_W - 1 - w:2 * GRID_W - 1 - w] for w in range(GRID_W)], axis=2)
    t = jnp.where(inside[None, None], t, MASK_VALUE)
    return jnp.concatenate([t[:, :-1], t[:, 1:]], axis=-1)


def _neighbourhood_attention(q, k, v, ck, cv, bias):
    b, t, w = q.shape
    rows = t // GRID_W
    p = ck.shape[1]
    seq = lambda n: pl.BlockSpec((1, n, w), lambda i, r: (i, 0, 0))
    row = lambda: pl.BlockSpec((1, GRID_W, w), lambda i, r: (i, r, 0))
    return pl.pallas_call(
        functools.partial(_na_kernel, rows),
        grid=(b, rows),
        in_specs=[row(), seq(t), seq(t), seq(p), seq(p), _resident(bias.shape)],
        out_specs=row(),
        out_shape=jax.ShapeDtypeStruct((b, t, w), BF16),
        compiler_params=_params(2),
        name="neighbourhood_attention",
    )(q, k, v, ck, cv, bias)


def _dft_tables(t):
    def cs(n):
        idx = (np.arange(n)[:, None] * np.arange(n)[None, :]) % n
        ang = 2.0 * np.pi * idx.astype(np.float64) / n
        return np.cos(ang), np.sin(ang)
    ct, st = cs(t)
    cg, sg = cs(FN_GW)
    return (np.concatenate([ct, -st], axis=1).astype(np.float32),
            np.concatenate([cg, sg], axis=1).astype(np.float32))


def _fnet_kernel(t, scale, f_ref, cgsg_ref, dft_ref, o_ref, xcs_ref):
    @pl.when(pl.program_id(1) == 0)
    def _():
        for g in range(FN_GROUPS):
            sl = slice(g * FN_GW, (g + 1) * FN_GW)
            xcs = _dot(f_ref[0, :, sl], cgsg_ref[...])
            xcs_ref[0:t, sl] = xcs[:, :FN_GW].astype(BF16)
            xcs_ref[t:2 * t, sl] = xcs[:, FN_GW:].astype(BF16)

    o_ref[0] = (_dot(dft_ref[...], xcs_ref[...]) * scale).astype(o_ref.dtype)


def _fourier_mix(f, dft, cgsg):
    b, t, w = f.shape
    tq = min(t, TOKEN_TILE)
    scale = float(1.0 / np.sqrt(float(t * FN_GW)))
    return pl.pallas_call(
        functools.partial(_fnet_kernel, t, scale),
        grid=(b, t // tq),
        in_specs=[
            pl.BlockSpec((1, t, w), lambda i, r: (i, 0, 0)),
            pl.BlockSpec(cgsg.shape, lambda i, r: (0, 0)),
            pl.BlockSpec((tq, 2 * t), lambda i, r: (r, 0)),
        ],
        out_specs=pl.BlockSpec((1, tq, w), lambda i, r: (i, r, 0)),
        out_shape=jax.ShapeDtypeStruct((b, t, w), BF16),
        scratch_shapes=[pltpu.VMEM((2 * t, w), BF16)],
        compiler_params=_params(2),
        name="fourier_mix",
    )(f, cgsg, dft)


def _ref_rows(b, b_ref, m, reverse):
    c = SCAN_CHUNK
    blk = 2 * m
    anchor = m if reverse else m - 1
    if blk >= 8:
        pieces = [jnp.broadcast_to(b_ref[p * blk + anchor:p * blk + anchor + 1, :], (blk, HG_DK))
                  for p in range(c // blk)]
        return pieces[0] if len(pieces) == 1 else jnp.concatenate(pieces, axis=0)
    pos = lax.broadcasted_iota(jnp.int32, (c, HG_DK), 0) % blk
    r = b
    for p in range(blk):
        delta = anchor - p
        if delta != 0:
            r = jnp.where(pos == p, pltpu.roll(b, (-delta) % c, 0), r)
    return r


def _level_masks():
    c = SCAN_CHUNK
    t = np.arange(c)
    left, right = [], []
    m = c // 2
    while m >= 1:
        is_right = (t // m) % 2 == 1
        left.append(np.where(is_right, NEG_BIG, 0.0))
        right.append(np.where(is_right, 0.0, NEG_BIG))
        m //= 2
    full = lambda rows: np.ascontiguousarray(
        np.broadcast_to(np.stack(rows)[:, :, None], (len(rows), c, HG_DK))).astype(np.float32)
    return full(left), full(right)


def _scan_chunks(chains, ml_ref, mr_ref):
    c = SCAN_CHUNK
    n = len(chains)
    row = lax.broadcasted_iota(jnp.int32, (c, c), 0)
    col = lax.broadcasted_iota(jnp.int32, (c, c), 1)
    xr = row ^ col

    ks, bs = [], []
    for reverse, q, z, v, lb, st, b_ref in chains:
        e = jnp.exp(-jnp.abs(z))
        inv = 1.0 / (1.0 + e)
        pos_z = z >= 0
        sig = jnp.where(pos_z, inv, e * inv)
        sig_neg = jnp.where(pos_z, e * inv, inv)
        ks.append((1.0 - lb) * sig_neg)
        log2_f = jnp.log2(lb + (1.0 - lb) * sig)
        tri = jnp.where((col >= row) if reverse else (col <= row), 1.0, 0.0).astype(BF16)
        b = None
        for part in _split3(log2_f):
            t = _dot(tri, part)
            b = t if b is None else b + t
        bs.append(b)

    os_, b_tots = [], []
    for (reverse, q, z, v, lb, st, b_ref), b in zip(chains, bs):
        b_ref[...] = b
        edge = 0 if reverse else c - 1
        b_tots.append(b_ref[edge:edge + 1, :])
        os_.append(_dot_nt((q * jnp.exp2(b)).astype(BF16), st.astype(BF16)))

    a_s = [None] * n
    m, level = c // 2, 0
    while m >= 1:
        for i, ((reverse, q, z, v, lb, st, b_ref), b, k) in enumerate(zip(chains, bs, ks)):
            d = b - _ref_rows(b, b_ref, m, reverse)
            mask_q, mask_k = (ml_ref, mr_ref) if reverse else (mr_ref, ml_ref)
            qe = (q * jnp.exp2(d + mask_q[level])).astype(BF16)
            ke = (k * jnp.exp2(mask_k[level] - d)).astype(BF16)
            lvl = _dot_nt(qe, ke)
            a_s[i] = lvl if a_s[i] is None else jnp.where(xr < 2 * m, lvl, a_s[i])
        m, level = m // 2, level + 1

    out = []
    for (reverse, q, z, v, lb, st, b_ref), b, k, a, o, b_tot in zip(chains, bs, ks, a_s, os_, b_tots):
        a = jnp.where(xr == 0, _dot_nt(q.astype(BF16), k.astype(BF16)), a)
        o = o + _dot(a.astype(BF16), v.astype(BF16))
        k_end = (k * jnp.exp2(b_tot - b)).astype(BF16)
        st_new = st * jnp.exp2(b_tot) + _dot(v.T.astype(BF16), k_end)
        out.append((o, st_new))
    return out


def _hgrn_kernel(layer, n_chunks, has_s0, *refs):
    refs = list(refs)
    q_ref, zf_ref, zb_ref, v_ref, g_ref, lg_ref, ng_ref, ml_ref, mr_ref = refs[:9]
    del refs[:9]
    s0_ref = refs.pop(0) if has_s0 else None
    y_ref, sfin_ref, stf_ref, stb_ref, bf_ref, bb_ref, of_ref, ob_ref = refs
    st_refs = (stf_ref, stb_ref)

    for direction in range(2):
        if has_s0:
            st_refs[direction][...] = s0_ref[0, direction, 0].T
        else:
            st_refs[direction][...] = jnp.zeros((HG_DV, HG_DK), F32)

    def lower_bound(direction):
        lg = lg_ref[:, direction, 0, 0, :]
        ex = jnp.exp(lg - lg.max(axis=0, keepdims=True))
        return ex[1:layer + 1].sum(axis=0, keepdims=True) / ex.sum(axis=0, keepdims=True)

    lb_f, lb_b = lower_bound(0), lower_bound(1)

    def chunk_rows(c):
        return pl.ds(pl.multiple_of(c * SCAN_CHUNK, SCAN_CHUNK), SCAN_CHUNK)

    def scan_body(c, carry):
        rf, rb = chunk_rows(c), chunk_rows(n_chunks - 1 - c)
        (o_f, st_f), (o_b, st_b) = _scan_chunks(
            [(False, q_ref[rf, :], zf_ref[rf, :], v_ref[rf, :], lb_f, stf_ref[...], bf_ref),
             (True, q_ref[rb, :], zb_ref[rb, :], v_ref[rb, :], lb_b, stb_ref[...], bb_ref)],
            ml_ref, mr_ref)
        stf_ref[...] = st_f
        stb_ref[...] = st_b
        of_ref[rf, :] = o_f
        ob_ref[rb, :] = o_b
        return carry

    lax.fori_loop(0, n_chunks, scan_body, 0)

    def gate_body(c, carry):
        rows = chunk_rows(c)
        o = of_ref[rows, :] + ob_ref[rows, :]
        o = o * lax.rsqrt(jnp.mean(o * o, axis=-1, keepdims=True) + EPS) * ng_ref[...]
        g = g_ref[rows, :]
        y_ref[rows, :] = (o * (g * _sigmoid(g))).astype(y_ref.dtype)
        return carry

    lax.fori_loop(0, n_chunks, gate_body, 0)

    for direction in range(2):
        sfin_ref[0, direction, 0] = st_refs[direction][...].T


def _hgrn_bidir(layer, seqs, u, logits, s0, norm_g):
    q, zf, zb, v, g = u
    n = q.shape[0]
    t = n // seqs
    n_chunks = t // SCAN_CHUNK
    ml, mr = _level_masks()
    tok = lambda: pl.BlockSpec((t, HG_DK), lambda s, h: (s, h))
    state = lambda: pl.BlockSpec((1, 2, 1, HG_DK, HG_DV), lambda s, h: (s, 0, h, 0, 0))
    in_specs = [tok(), tok(), tok(), tok(), tok(),
                pl.BlockSpec((DEPTH, 2, 1, 1, HG_DK), lambda s, h: (0, 0, h, 0, 0)),
                _resident((1, HG_DV)), _resident(ml.shape), _resident(mr.shape)]
    args = [q, zf, zb, v, g, logits.reshape(DEPTH, 2, HG_HEADS, 1, HG_DK), norm_g.reshape(1, HG_DV),
            jnp.asarray(ml), jnp.asarray(mr)]
    if s0 is not None:
        in_specs.append(state())
        args.append(s0)
    return pl.pallas_call(
        functools.partial(_hgrn_kernel, layer, n_chunks, s0 is not None),
        grid=(seqs, HG_HEADS),
        in_specs=in_specs,
        out_specs=[tok(), state()],
        out_shape=[jax.ShapeDtypeStruct((n, HG_KW), BF16),
                   jax.ShapeDtypeStruct((seqs, 2, HG_HEADS, HG_DK, HG_DV), F32)],
        scratch_shapes=[pltpu.VMEM((HG_DV, HG_DK), F32), pltpu.VMEM((HG_DV, HG_DK), F32),
                        pltpu.VMEM((SCAN_CHUNK, HG_DK), F32), pltpu.VMEM((SCAN_CHUNK, HG_DK), F32),
                        pltpu.VMEM((t, HG_DV), F32), pltpu.VMEM((t, HG_DV), F32)],
        compiler_params=_params(2),
        name="hgrn2_bidir",
    )(*args)


def kernel(x_prompt, x_sample, cache_k, cache_v, state_hgrn, c, c_ctx, ada_w, ada_b, norm_g, ffn_w1, ffn_w2,
           mix0_w_in, mix0_w_out, na_rpb, mix1_w_in, mix1_w_out, hg_lb_logits, hg_norm_g, norm_f):
    bp, sp, d = x_prompt.shape
    bs, ts, _ = x_sample.shape
    xp = x_prompt.reshape(bp * sp, d)
    xs = x_sample.reshape(bs * ts, d)
    tiles_s = ts // TOKEN_TILE

    cond = jnp.zeros((COND_ROWS, d), F32).at[:bs].set(c).at[CTX_ROW].set(c_ctx)
    mod = _ada_mod(cond, ada_w, ada_b)

    w1 = ffn_w1.astype(BF16)
    w2 = ffn_w2.astype(BF16)

    new_k, new_v, new_s = [], [], []
    for l in range(DEPTH):
        last = l == DEPTH - 1

        def ffn(x, tiles, j, half, final):
            return _half_ffn(x, mod, l, tiles, j, half, norm_g[l, j], w1, w2, norm_f, final)

        xp = ffn(xp, None, 0, 0, False)
        xs = ffn(xs, tiles_s, 0, 0, False)

        if l % 2 == 0:
            e = l // 2
            w_in = mix0_w_in[e].astype(BF16)
            w_out = mix0_w_out[e].astype(BF16)
            qk_scale = float(NA_HD) ** -0.5
            qp, kp, vp, fp = _in_proj(xp, mod, l, None, norm_g[l, 1], w_in,
                                      [(NA_W, BF16, qk_scale), (NA_W, F32, 1.0), (NA_W, F32, 1.0), (FN_W, BF16, 1.0)])
            qs, ks, vs, fs = _in_proj(xs, mod, l, tiles_s, norm_g[l, 1], w_in,
                                      [(NA_W, BF16, qk_scale), (NA_W, BF16, 1.0), (NA_W, BF16, 1.0), (FN_W, BF16, 1.0)])
            new_k.append(kp.reshape(bp, sp, NA_HEADS, NA_HD))
            new_v.append(vp.reshape(bp, sp, NA_HEADS, NA_HD))

            r3 = lambda a, b_, t_: a.reshape(b_, t_, a.shape[-1])
            ap = _context_attention(r3(qp, bp, sp), r3(kp, bp, sp), r3(vp, bp, sp))
            ck = cache_k[:, e].reshape(bs, -1, NA_W).astype(BF16)
            cv = cache_v[:, e].reshape(bs, -1, NA_W).astype(BF16)
            a_s = _neighbourhood_attention(r3(qs, bs, ts), r3(ks, bs, ts), r3(vs, bs, ts), ck, cv,
                                           _na_bias_table(na_rpb[e]))
            cgsg = None
            fn = []
            for f3 in (r3(fp, bp, sp), r3(fs, bs, ts)):
                dft_np, cgsg_np = _dft_tables(f3.shape[1])
                fn.append(_fourier_mix(f3, jnp.asarray(dft_np).astype(BF16), jnp.asarray(cgsg_np).astype(BF16)))
            xp = _out_proj(xp, mod, l, None, [ap.reshape(bp * sp, NA_W), fn[0].reshape(bp * sp, FN_W)], w_out)
            xs = _out_proj(xs, mod, l, tiles_s, [a_s.reshape(bs * ts, NA_W), fn[1].reshape(bs * ts, FN_W)], w_out)
        else:
            o = l // 2
            w_in = mix1_w_in[o].astype(BF16)
            w_out = mix1_w_out[o].astype(BF16)
            outs = [(HG_KW, F32, 1.0)] * 5
            up = _in_proj(xp, mod, l, None, norm_g[l, 1], w_in, outs)
            us = _in_proj(xs, mod, l, tiles_s, norm_g[l, 1], w_in, outs)
            yp, s_fin = _hgrn_bidir(l, bp, up, hg_lb_logits, None, hg_norm_g[o])
            ys, _ = _hgrn_bidir(l, bs, us, hg_lb_logits, state_hgrn[:, o], hg_norm_g[o])
            new_s.append(s_fin)
            xp = _out_proj(xp, mod, l, None, [yp], w_out)
            xs = _out_proj(xs, mod, l, tiles_s, [ys], w_out)

        xp = ffn(xp, None, 2, 1, last)
        xs = ffn(xs, tiles_s, 2, 1, last)

    return (xp.reshape(bp, sp, d), xs.reshape(bs, ts, d),
            jnp.stack(new_k, axis=1), jnp.stack(new_v, axis=1), jnp.stack(new_s, axis=1))
```

```python
import functools

import numpy as np
import jax
import jax.numpy as jnp
from jax import lax
from jax.experimental import pallas as pl
from jax.experimental.pallas import tpu as pltpu

F32 = jnp.float32
BF16 = jnp.bfloat16

D_MODEL = 1024
DEPTH = 2
GRID_W = 64
NA_HEADS = 8
NA_HD = 64
NA_W = NA_HEADS * NA_HD
NA_WIN_R = 8
NA_WIN_C = 16
FN_GROUPS = 4
FN_GW = 128
FN_W = FN_GROUPS * FN_GW
HG_HEADS = 8
HG_DK = 128
HG_DV = 128
HG_KW = HG_HEADS * HG_DK
D_FF = 2816
N_MOD = 9
EPS = 1e-6

COND_ROWS = 8
CTX_ROW = 4
TOKEN_TILE = 256
PROJ_TILE = 512
SCAN_CHUNK = 128
SUBLANES = 8
SMALL_LEVELS = (4, 2, 1)
FAST_BLOCK = 16
FAST_LOG2_LIMIT = 126.0 / (FAST_BLOCK // 2)
MASK_VALUE = -1e30
NEG_BIG = -1e30
VMEM_LIMIT = 56 * 1024 * 1024


def _params(n_axes, vmem=VMEM_LIMIT):
    return pltpu.CompilerParams(dimension_semantics=("arbitrary",) * n_axes, vmem_limit_bytes=vmem)


def _resident(shape):
    nd = len(shape)
    return pl.BlockSpec(shape, lambda *_: (0,) * nd, pipeline_mode=pl.Buffered(1))


def _sigmoid(a):
    return 1.0 / (1.0 + jnp.exp(-a))


def _rms_mod(x, g, shift, scale):
    y = x * lax.rsqrt(jnp.mean(x * x, axis=-1, keepdims=True) + EPS)
    return (y * g) * (1.0 + scale) + shift


def _split2(x):
    hi = x.astype(BF16)
    lo = (x - hi.astype(F32)).astype(BF16)
    return hi, lo


def _split3(x):
    hi = x.astype(BF16)
    r = x - hi.astype(F32)
    mid = r.astype(BF16)
    lo = (r - mid.astype(F32)).astype(BF16)
    return hi, mid, lo


def _dot(a, b):
    return jnp.dot(a, b, preferred_element_type=F32)


def _dot_nt(a, b):
    return lax.dot_general(a, b, (((1,), (1,)), ((), ())), preferred_element_type=F32)


def _ada_kernel(cond_ref, w_ref, b_ref, o_ref):
    c = cond_ref[...]
    s = c * _sigmoid(c)
    s_hi, s_lo = _split2(s)
    w_hi, w_lo = _split2(w_ref[0])
    o_ref[0] = _dot(s_hi, w_hi) + _dot(s_hi, w_lo) + _dot(s_lo, w_hi) + b_ref[0]


def _ada_mod(cond, ada_w, ada_b):
    depth, d, n = ada_w.shape
    tn = 1152
    out = pl.pallas_call(
        _ada_kernel,
        grid=(depth, n // tn),
        in_specs=[
            pl.BlockSpec((COND_ROWS, d), lambda l, j: (0, 0)),
            pl.BlockSpec((1, d, tn), lambda l, j: (l, 0, j)),
            pl.BlockSpec((1, 1, tn), lambda l, j: (l, 0, j)),
        ],
        out_specs=pl.BlockSpec((1, COND_ROWS, tn), lambda l, j: (l, 0, j)),
        out_shape=jax.ShapeDtypeStruct((depth, COND_ROWS, n), F32),
        compiler_params=_params(2),
        name="ada_mod",
    )(cond, ada_w, ada_b.reshape(depth, 1, n))
    return out.reshape(depth, COND_ROWS, N_MOD, d)


def _mod_spec(layer, seq_len, tm):
    if seq_len is None:
        return pl.BlockSpec((1, 1, N_MOD, D_MODEL), lambda i: (layer, CTX_ROW, 0, 0))
    tiles_per_seq = seq_len // tm
    return pl.BlockSpec((1, 1, N_MOD, D_MODEL), lambda i: (layer, i // tiles_per_seq, 0, 0))


def _ffn_kernel(j, final_norm, x_ref, mod_ref, g_ref, w1a_ref, w1b_ref, w2_ref, gf_ref, o_ref):
    x = x_ref[...]
    m = mod_ref[0, 0]
    h = _rms_mod(x, g_ref[...], m[3 * j:3 * j + 1], m[3 * j + 1:3 * j + 2]).astype(BF16)
    a = _dot(h, w1a_ref[...])
    b = _dot(h, w1b_ref[...])
    act = (a * _sigmoid(a) * b).astype(BF16)
    y = x + (0.5 * m[3 * j + 2:3 * j + 3]) * _dot(act, w2_ref[...])
    if final_norm:
        y = y * lax.rsqrt(jnp.mean(y * y, axis=-1, keepdims=True) + EPS) * gf_ref[...]
    o_ref[...] = y


def _half_ffn(x, mod, layer, seq_len, j, half, g, w1, w2, gf, final_norm):
    n, d = x.shape
    tm = TOKEN_TILE
    once = pl.Buffered(1)
    return pl.pallas_call(
        functools.partial(_ffn_kernel, j, final_norm),
        grid=(n // tm,),
        in_specs=[
            pl.BlockSpec((tm, d), lambda i: (i, 0)),
            _mod_spec(layer, seq_len, tm),
            _resident((1, d)),
            pl.BlockSpec((None, None, d, D_FF), lambda i: (layer, half, 0, 0), pipeline_mode=once),
            pl.BlockSpec((None, None, d, D_FF), lambda i: (layer, half, 0, 1), pipeline_mode=once),
            pl.BlockSpec((None, None, D_FF, d), lambda i: (layer, half, 0, 0), pipeline_mode=once),
            _resident((1, d)),
        ],
        out_specs=pl.BlockSpec((tm, d), lambda i: (i, 0)),
        out_shape=jax.ShapeDtypeStruct((n, d), F32),
        compiler_params=_params(1),
        name="half_ffn",
    )(x, mod, g.reshape(1, d), w1, w1, w2, gf.reshape(1, d))


def _proj_kernel(j, splits, x_ref, mod_ref, g_ref, w_ref, *o_refs):
    m = mod_ref[0, 0]
    h = _rms_mod(x_ref[...], g_ref[...], m[3 * j:3 * j + 1], m[3 * j + 1:3 * j + 2]).astype(BF16)
    u = _dot(h, w_ref[...])
    for (lo, hi, scale), o_ref in zip(splits, o_refs):
        piece = u[:, lo:hi]
        if scale != 1.0:
            piece = piece * scale
        o_ref[...] = piece.astype(o_ref.dtype)


def _in_proj(x, mod, layer, seq_len, g, w, outs):
    n, d = x.shape
    tm = PROJ_TILE
    splits, lo = [], 0
    for width, _, scale in outs:
        splits.append((lo, lo + width, scale))
        lo += width
    return pl.pallas_call(
        functools.partial(_proj_kernel, 1, tuple(splits)),
        grid=(n // tm,),
        in_specs=[
            pl.BlockSpec((tm, d), lambda i: (i, 0)),
            _mod_spec(layer, seq_len, tm),
            _resident((1, d)),
            _resident(w.shape),
        ],
        out_specs=[pl.BlockSpec((tm, width), lambda i: (i, 0)) for width, _, _ in outs],
        out_shape=[jax.ShapeDtypeStruct((n, width), dt) for width, dt, _ in outs],
        compiler_params=_params(1),
        name="mixer_in_proj",
    )(x, mod, g.reshape(1, d), w)


def _out_proj_kernel(n_acts, x_ref, mod_ref, *refs):
    a_refs, w_refs, o_ref = refs[:n_acts], refs[n_acts:2 * n_acts], refs[2 * n_acts]
    y = None
    for a_ref, w_ref in zip(a_refs, w_refs):
        t = _dot(a_ref[...], w_ref[...])
        y = t if y is None else y + t
    o_ref[...] = x_ref[...] + mod_ref[0, 0][5:6] * y


def _out_proj(x, mod, layer, seq_len, acts, w):
    n, d = x.shape
    tm = PROJ_TILE
    width = acts[0].shape[1]
    w_specs = [pl.BlockSpec((width, d), functools.partial(lambda i, blk: (blk, 0), blk=blk),
                            pipeline_mode=pl.Buffered(1)) for blk in range(len(acts))]
    return pl.pallas_call(
        functools.partial(_out_proj_kernel, len(acts)),
        grid=(n // tm,),
        in_specs=([pl.BlockSpec((tm, d), lambda i: (i, 0)), _mod_spec(layer, seq_len, tm)]
                  + [pl.BlockSpec((tm, width), lambda i: (i, 0)) for _ in acts] + w_specs),
        out_specs=pl.BlockSpec((tm, d), lambda i: (i, 0)),
        out_shape=jax.ShapeDtypeStruct((n, d), F32),
        compiler_params=_params(1),
        name="mixer_out_proj",
    )(x, mod, *acts, *([w] * len(acts)))


def _head_pair_attention(pairs):
    lane = lax.broadcasted_iota(jnp.int32, (1, 2 * NA_HD), 1)
    first = lane < NA_HD
    scores = []
    for q, key_blocks, _, bias_blocks in pairs:
        zero = jnp.zeros_like(q)
        qm = jnp.concatenate([jnp.where(first, q, zero), jnp.where(first, zero, q)], axis=0)
        ss = []
        for kb, bias in zip(key_blocks, bias_blocks):
            s = _dot_nt(qm, kb)
            ss.append(s if bias is None else s + bias)
        scores.append(ss)
    maxes = []
    for ss in scores:
        mx = ss[0].max(axis=-1, keepdims=True)
        for s in ss[1:]:
            mx = jnp.maximum(mx, s.max(axis=-1, keepdims=True))
        maxes.append(mx)
    outs = []
    for (q, _, value_blocks, _), ss, mx in zip(pairs, scores, maxes):
        den, acc = None, None
        for s, vb in zip(ss, value_blocks):
            p = jnp.exp(s - mx)
            dsum = p.sum(axis=-1, keepdims=True)
            pv = _dot(p.astype(BF16), vb)
            den = dsum if den is None else den + dsum
            acc = pv if acc is None else acc + pv
        o = acc / den
        tq = q.shape[0]
        outs.append(jnp.where(first, o[:tq], o[tq:]))
    return outs


def _ctx_attn_kernel(q_ref, k_ref, v_ref, o_ref):
    slices = [slice(pr * 2 * NA_HD, (pr + 1) * 2 * NA_HD) for pr in range(NA_HEADS // 2)]
    outs = _head_pair_attention([(q_ref[0, :, sl], [k_ref[0, :, sl].astype(BF16)],
                                  [v_ref[0, :, sl].astype(BF16)], [None]) for sl in slices])
    for sl, o in zip(slices, outs):
        o_ref[0, :, sl] = o.astype(o_ref.dtype)


def _context_attention(q, k, v):
    b, s, w = q.shape
    blk = lambda: pl.BlockSpec((1, s, w), lambda i: (i, 0, 0))
    return pl.pallas_call(
        _ctx_attn_kernel,
        grid=(b,),
        in_specs=[blk(), blk(), blk()],
        out_specs=blk(),
        out_shape=jax.ShapeDtypeStruct((b, s, w), BF16),
        compiler_params=_params(1),
        name="context_attention",
    )(q, k, v)


def _na_kernel(rows, q_ref, k_ref, v_ref, ck_ref, cv_ref, bias_ref, o_ref):
    r = pl.program_id(1)
    r0 = jnp.clip(r - NA_WIN_R // 2, 0, rows - NA_WIN_R)
    start = pl.multiple_of(r0 * GRID_W, GRID_W)
    win = pl.ds(start, NA_WIN_R * GRID_W)
    first = (NA_WIN_R - 1) - (r - r0)
    slices = [slice(pr * 2 * NA_HD, (pr + 1) * 2 * NA_HD) for pr in range(NA_HEADS // 2)]
    pairs = []
    for pr, sl in enumerate(slices):
        bias = jnp.concatenate(
            [jnp.concatenate([bias_ref[2 * pr + hh, first + 2 * jj] for jj in range(NA_WIN_R // 2)], axis=1)
             for hh in range(2)], axis=0)
        pairs.append((q_ref[0, :, sl], [k_ref[0, win, sl], ck_ref[0, :, sl]],
                      [v_ref[0, win, sl], cv_ref[0, :, sl]], [bias, None]))
    for sl, o in zip(slices, _head_pair_attention(pairs)):
        o_ref[0, :, sl] = o.astype(o_ref.dtype)


def _na_bias_table(rpb):
    cols = np.arange(GRID_W)
    col_start = np.clip(cols - NA_WIN_C // 2, 0, GRID_W - NA_WIN_C)
    inside = (cols[None, :] >= col_start[:, None]) & (cols[None, :] < col_start[:, None] + NA_WIN_C)
    pad = GRID_W - NA_WIN_C
    padded = jnp.pad(rpb.astype(F32), ((0, 0), (0, 0), (pad, pad)))
    t = jnp.stack([padded[:, :, GRID_W - 1 - w:2 * GRID_W - 1 - w] for w in range(GRID_W)], axis=2)
    t = jnp.where(inside[None, None], t, MASK_VALUE)
    return jnp.concatenate([t[:, :-1], t[:, 1:]], axis=-1)


def _neighbourhood_attention(q, k, v, ck, cv, bias):
    b, t, w = q.shape
    rows = t // GRID_W
    p = ck.shape[1]
    seq = lambda n: pl.BlockSpec((1, n, w), lambda i, r: (i, 0, 0))
    row = lambda: pl.BlockSpec((1, GRID_W, w), lambda i, r: (i, r, 0))
    return pl.pallas_call(
        functools.partial(_na_kernel, rows),
        grid=(b, rows),
        in_specs=[row(), seq(t), seq(t), seq(p), seq(p), _resident(bias.shape)],
        out_specs=row(),
        out_shape=jax.ShapeDtypeStruct((b, t, w), BF16),
        compiler_params=_params(2),
        name="neighbourhood_attention",
    )(q, k, v, ck, cv, bias)


def _dft_tables(t):
    def cs(n):
        idx = (np.arange(n)[:, None] * np.arange(n)[None, :]) % n
        ang = 2.0 * np.pi * idx.astype(np.float64) / n
        return np.cos(ang), np.sin(ang)
    ct, st = cs(t)
    cg, sg = cs(FN_GW)
    return (np.concatenate([ct, -st], axis=1).astype(np.float32),
            np.concatenate([cg, sg], axis=1).astype(np.float32))


def _fnet_kernel(t, scale, f_ref, cgsg_ref, dft_ref, o_ref, xcs_ref):
    @pl.when(pl.program_id(1) == 0)
    def _():
        for g in range(FN_GROUPS):
            sl = slice(g * FN_GW, (g + 1) * FN_GW)
            xcs = _dot(f_ref[0, :, sl], cgsg_ref[...])
            xcs_ref[0:t, sl] = xcs[:, :FN_GW].astype(BF16)
            xcs_ref[t:2 * t, sl] = xcs[:, FN_GW:].astype(BF16)

    o_ref[0] = (_dot(dft_ref[...], xcs_ref[...]) * scale).astype(o_ref.dtype)


def _fourier_mix(f, dft, cgsg):
    b, t, w = f.shape
    tq = min(t, TOKEN_TILE)
    scale = float(1.0 / np.sqrt(float(t * FN_GW)))
    return pl.pallas_call(
        functools.partial(_fnet_kernel, t, scale),
        grid=(b, t // tq),
        in_specs=[
            pl.BlockSpec((1, t, w), lambda i, r: (i, 0, 0)),
            pl.BlockSpec(cgsg.shape, lambda i, r: (0, 0)),
            pl.BlockSpec((tq, 2 * t), lambda i, r: (r, 0)),
        ],
        out_specs=pl.BlockSpec((1, tq, w), lambda i, r: (i, r, 0)),
        out_shape=jax.ShapeDtypeStruct((b, t, w), BF16),
        scratch_shapes=[pltpu.VMEM((2 * t, w), BF16)],
        compiler_params=_params(2),
        name="fourier_mix",
    )(f, cgsg, dft)


def _ref_rows(b, b_ref, m, reverse):
    c = SCAN_CHUNK
    blk = 2 * m
    anchor = m if reverse else m - 1
    if blk >= 8:
        pieces = [jnp.broadcast_to(b_ref[p * blk + anchor:p * blk + anchor + 1, :], (blk, HG_DK))
                  for p in range(c // blk)]
        return pieces[0] if len(pieces) == 1 else jnp.concatenate(pieces, axis=0)
    pos = lax.broadcasted_iota(jnp.int32, (c, HG_DK), 0) % blk
    r = b
    for p in range(blk):
        delta = anchor - p
        if delta != 0:
            r = jnp.where(pos == p, pltpu.roll(b, (-delta) % c, 0), r)
    return r


def _level_masks():
    c = SCAN_CHUNK
    t = np.arange(c)
    left, right = [], []
    for m in SMALL_LEVELS:
        is_right = (t // m) % 2 == 1
        left.append(np.where(is_right, NEG_BIG, 0.0))
        right.append(np.where(is_right, 0.0, NEG_BIG))
    full = lambda rows: np.ascontiguousarray(
        np.broadcast_to(np.stack(rows)[:, :, None], (len(rows), c, HG_DK))).astype(np.float32)
    return full(left), full(right)


def _tri_tables():
    t = np.arange(SCAN_CHUNK)
    return np.stack([t[None, :] <= t[:, None], t[None, :] >= t[:, None]]).astype(np.float32)


def _scan_gates(z, lb, tri):
    e = jnp.exp(-jnp.abs(z))
    inv = 1.0 / (1.0 + e)
    pos_z = z >= 0
    sig = jnp.where(pos_z, inv, e * inv)
    sig_neg = jnp.where(pos_z, e * inv, inv)
    k = (1.0 - lb) * sig_neg
    log2_f = jnp.log2(lb + (1.0 - lb) * sig)
    b = None
    for part in _split3(log2_f):
        t = _dot(tri, part)
        b = t if b is None else b + t
    return k, b, jnp.min(log2_f)


def _scan_chunks(chains, ml_ref, mr_ref, fast):
    c = SCAN_CHUNK
    n_tiles = c // SUBLANES
    tile = lambda x, i: x[i * SUBLANES:(i + 1) * SUBLANES]
    row = lax.broadcasted_iota(jnp.int32, (c, c), 0)
    col = lax.broadcasted_iota(jnp.int32, (c, c), 1)
    xr = row ^ col
    zero_tile = jnp.zeros((SUBLANES, HG_DK), F32)

    os_, b_tots = [], []
    for reverse, q, v, k, b, st, b_ref in chains:
        edge = 0 if reverse else c - 1
        b_tots.append(b_ref[edge:edge + 1, :])
        os_.append(_dot_nt((q * jnp.exp2(b)).astype(BF16), st.astype(BF16)))

    a_tiles = [[None] * n_tiles for _ in chains]
    m = c // 2
    while m >= (FAST_BLOCK if fast else SUBLANES):
        g = m // SUBLANES
        for ci, (reverse, q, v, k, b, st, b_ref) in enumerate(chains):
            q_ids, qe_tiles, ke_tiles = [], [], [zero_tile] * n_tiles
            for p in range(n_tiles // (2 * g)):
                left = range(2 * g * p, 2 * g * p + g)
                right = range(2 * g * p + g, 2 * g * (p + 1))
                anchor = (2 * g * p + g) * SUBLANES - (0 if reverse else 1)
                r = jnp.broadcast_to(b_ref[anchor:anchor + 1, :], (SUBLANES, HG_DK))
                q_side, k_side = (left, right) if reverse else (right, left)
                for i in q_side:
                    q_ids.append(i)
                    qe_tiles.append(tile(q, i) * jnp.exp2(tile(b, i) - r))
                for i in k_side:
                    ke_tiles[i] = tile(k, i) * jnp.exp2(r - tile(b, i))
            lvl = _dot_nt(jnp.concatenate(qe_tiles, axis=0).astype(BF16),
                          jnp.concatenate(ke_tiles, axis=0).astype(BF16))
            for j, i in enumerate(q_ids):
                old = a_tiles[ci][i]
                a_tiles[ci][i] = tile(lvl, j) if old is None else jnp.where(tile(xr, i) < 2 * m, tile(lvl, j), old)
        m //= 2
    a_s = [jnp.concatenate([t if t is not None else zero_tile for t in tiles], axis=0) for tiles in a_tiles]

    if fast:
        for ci, (reverse, q, v, k, b, st, b_ref) in enumerate(chains):
            mid = FAST_BLOCK // 2
            r = jnp.concatenate(
                [jnp.broadcast_to(b_ref[j * FAST_BLOCK + mid:j * FAST_BLOCK + mid + 1, :], (FAST_BLOCK, HG_DK))
                 for j in range(c // FAST_BLOCK)], axis=0)
            qe = (q * jnp.exp2(b - r)).astype(BF16)
            ke = (k * jnp.exp2(r - b)).astype(BF16)
            a_s[ci] = jnp.where(xr < FAST_BLOCK, _dot_nt(qe, ke), a_s[ci])
    else:
        for level, m in enumerate(SMALL_LEVELS):
            for ci, (reverse, q, v, k, b, st, b_ref) in enumerate(chains):
                d = b - _ref_rows(b, b_ref, m, reverse)
                mask_q, mask_k = (ml_ref, mr_ref) if reverse else (mr_ref, ml_ref)
                qe = (q * jnp.exp2(d + mask_q[level])).astype(BF16)
                ke = (k * jnp.exp2(mask_k[level] - d)).astype(BF16)
                a_s[ci] = jnp.where(xr < 2 * m, _dot_nt(qe, ke), a_s[ci])

    out = []
    for (reverse, q, v, k, b, st, b_ref), a, o, b_tot in zip(chains, a_s, os_, b_tots):
        if fast:
            a = jnp.where((col >= row) if reverse else (col <= row), a, 0.0)
        else:
            diag = jnp.where(xr == 0, _dot_nt(q.astype(BF16), k.astype(BF16)), 0.0)
            a = jnp.where((col > row) if reverse else (col < row), a, diag)
        o = o + _dot(a.astype(BF16), v.astype(BF16))
        k_end = (k * jnp.exp2(b_tot - b)).astype(BF16)
        st_new = st * jnp.exp2(b_tot) + _dot(v.T.astype(BF16), k_end)
        out.append((o, st_new))
    return out


def _hgrn_kernel(layer, n_chunks, has_s0, *refs):
    refs = list(refs)
    q_ref, zf_ref, zb_ref, v_ref, g_ref, lg_ref, ng_ref, ml_ref, mr_ref, tri_ref = refs[:10]
    del refs[:10]
    s0_ref = refs.pop(0) if has_s0 else None
    y_ref, sfin_ref, stf_ref, stb_ref, bf_ref, bb_ref, kf_ref, kb_ref, of_ref, ob_ref, fast_ref = refs
    st_refs = (stf_ref, stb_ref)

    for direction in range(2):
        if has_s0:
            st_refs[direction][...] = s0_ref[0, direction, 0].T
        else:
            st_refs[direction][...] = jnp.zeros((HG_DV, HG_DK), F32)

    def lower_bound(direction):
        lg = lg_ref[:, direction, 0, 0, :]
        ex = jnp.exp(lg - lg.max(axis=0, keepdims=True))
        return ex[1:layer + 1].sum(axis=0, keepdims=True) / ex.sum(axis=0, keepdims=True)

    lb_f, lb_b = lower_bound(0), lower_bound(1)

    def chunk_rows(c):
        return pl.ds(pl.multiple_of(c * SCAN_CHUNK, SCAN_CHUNK), SCAN_CHUNK)

    def gates(c_fwd, c_bwd):
        return (_scan_gates(zf_ref[chunk_rows(c_fwd), :], lb_f, tri_ref[0]),
                _scan_gates(zb_ref[chunk_rows(c_bwd), :], lb_b, tri_ref[1]))

    def store_gates(g):
        (k_f, b_f, lo_f), (k_b, b_b, lo_b) = g
        kf_ref[...] = k_f
        bf_ref[...] = b_f
        kb_ref[...] = k_b
        bb_ref[...] = b_b
        fast_ref[0] = (jnp.minimum(lo_f, lo_b) >= -FAST_LOG2_LIMIT).astype(jnp.int32)

    store_gates(gates(0, n_chunks - 1))

    def trip(c, fast):
        rf, rb = chunk_rows(c), chunk_rows(n_chunks - 1 - c)
        nxt = gates(jnp.minimum(c + 1, n_chunks - 1), jnp.maximum(n_chunks - 2 - c, 0))
        (o_f, st_f), (o_b, st_b) = _scan_chunks(
            [(False, q_ref[rf, :], v_ref[rf, :], kf_ref[...], bf_ref[...], stf_ref[...], bf_ref),
             (True, q_ref[rb, :], v_ref[rb, :], kb_ref[...], bb_ref[...], stb_ref[...], bb_ref)],
            ml_ref, mr_ref, fast)
        stf_ref[...] = st_f
        stb_ref[...] = st_b
        of_ref[rf, :] = o_f
        ob_ref[rb, :] = o_b
        store_gates(nxt)

    def scan_body(c, carry):
        use_fast = fast_ref[0] == 1
        pl.when(use_fast)(lambda: trip(c, True))
        pl.when(jnp.logical_not(use_fast))(lambda: trip(c, False))
        return carry

    lax.fori_loop(0, n_chunks, scan_body, 0)

    def gate_body(c, carry):
        rows = chunk_rows(c)
        o = of_ref[rows, :] + ob_ref[rows, :]
        o = o * lax.rsqrt(jnp.mean(o * o, axis=-1, keepdims=True) + EPS) * ng_ref[...]
        g = g_ref[rows, :]
        y_ref[rows, :] = (o * (g * _sigmoid(g))).astype(y_ref.dtype)
        return carry

    lax.fori_loop(0, n_chunks, gate_body, 0)

    for direction in range(2):
        sfin_ref[0, direction, 0] = st_refs[direction][...].T


def _hgrn_bidir(layer, seqs, u, logits, s0, norm_g):
    q, zf, zb, v, g = u
    n = q.shape[0]
    t = n // seqs
    n_chunks = t // SCAN_CHUNK
    ml, mr = _level_masks()
    tri = _tri_tables()
    tok = lambda: pl.BlockSpec((t, HG_DK), lambda s, h: (s, h))
    state = lambda: pl.BlockSpec((1, 2, 1, HG_DK, HG_DV), lambda s, h: (s, 0, h, 0, 0))
    in_specs = [tok(), tok(), tok(), tok(), tok(),
                pl.BlockSpec((DEPTH, 2, 1, 1, HG_DK), lambda s, h: (0, 0, h, 0, 0)),
                _resident((1, HG_DV)), _resident(ml.shape), _resident(mr.shape), _resident(tri.shape)]
    args = [q, zf, zb, v, g, logits.reshape(DEPTH, 2, HG_HEADS, 1, HG_DK), norm_g.reshape(1, HG_DV),
            jnp.asarray(ml), jnp.asarray(mr), jnp.asarray(tri).astype(BF16)]
    if s0 is not None:
        in_specs.append(state())
        args.append(s0)
    return pl.pallas_call(
        functools.partial(_hgrn_kernel, layer, n_chunks, s0 is not None),
        grid=(seqs, HG_HEADS),
        in_specs=in_specs,
        out_specs=[tok(), state()],
        out_shape=[jax.ShapeDtypeStruct((n, HG_KW), BF16),
                   jax.ShapeDtypeStruct((seqs, 2, HG_HEADS, HG_DK, HG_DV), F32)],
        scratch_shapes=[pltpu.VMEM((HG_DV, HG_DK), F32), pltpu.VMEM((HG_DV, HG_DK), F32),
                        pltpu.VMEM((SCAN_CHUNK, HG_DK), F32), pltpu.VMEM((SCAN_CHUNK, HG_DK), F32),
                        pltpu.VMEM((SCAN_CHUNK, HG_DK), F32), pltpu.VMEM((SCAN_CHUNK, HG_DK), F32),
                        pltpu.VMEM((t, HG_DV), F32), pltpu.VMEM((t, HG_DV), F32),
                        pltpu.SMEM((1,), jnp.int32)],
        compiler_params=_params(2),
        name="hgrn2_bidir",
    )(*args)


def kernel(x_prompt, x_sample, cache_k, cache_v, state_hgrn, c, c_ctx, ada_w, ada_b, norm_g, ffn_w1, ffn_w2,
           mix0_w_in, mix0_w_out, na_rpb, mix1_w_in, mix1_w_out, hg_lb_logits, hg_norm_g, norm_f):
    bp, sp, d = x_prompt.shape
    bs, ts, _ = x_sample.shape
    xp = x_prompt.reshape(bp * sp, d)
    xs = x_sample.reshape(bs * ts, d)

    cond = jnp.zeros((COND_ROWS, d), F32).at[:bs].set(c).at[CTX_ROW].set(c_ctx)
    mod = _ada_mod(cond, ada_w, ada_b)

    w1 = ffn_w1.astype(BF16)
    w2 = ffn_w2.astype(BF16)

    new_k, new_v, new_s = [], [], []
    for l in range(DEPTH):
        last = l == DEPTH - 1

        def ffn(x, tiles, j, half, final):
            return _half_ffn(x, mod, l, tiles, j, half, norm_g[l, j], w1, w2, norm_f, final)

        xp = ffn(xp, None, 0, 0, False)
        xs = ffn(xs, ts, 0, 0, False)

        if l % 2 == 0:
            e = l // 2
            w_in = mix0_w_in[e].astype(BF16)
            w_out = mix0_w_out[e].astype(BF16)
            qk_scale = float(NA_HD) ** -0.5
            qp, kp, vp, fp = _in_proj(xp, mod, l, None, norm_g[l, 1], w_in,
                                      [(NA_W, BF16, qk_scale), (NA_W, F32, 1.0), (NA_W, F32, 1.0), (FN_W, BF16, 1.0)])
            qs, ks, vs, fs = _in_proj(xs, mod, l, ts, norm_g[l, 1], w_in,
                                      [(NA_W, BF16, qk_scale), (NA_W, BF16, 1.0), (NA_W, BF16, 1.0), (FN_W, BF16, 1.0)])
            new_k.append(kp.reshape(bp, sp, NA_HEADS, NA_HD))
            new_v.append(vp.reshape(bp, sp, NA_HEADS, NA_HD))

            r3 = lambda a, b_, t_: a.reshape(b_, t_, a.shape[-1])
            ap = _context_attention(r3(qp, bp, sp), r3(kp, bp, sp), r3(vp, bp, sp))
            ck = cache_k[:, e].reshape(bs, -1, NA_W).astype(BF16)
            cv = cache_v[:, e].reshape(bs, -1, NA_W).astype(BF16)
            a_s = _neighbourhood_attention(r3(qs, bs, ts), r3(ks, bs, ts), r3(vs, bs, ts), ck, cv,
                                           _na_bias_table(na_rpb[e]))
            cgsg = None
            fn = []
            for f3 in (r3(fp, bp, sp), r3(fs, bs, ts)):
                dft_np, cgsg_np = _dft_tables(f3.shape[1])
                fn.append(_fourier_mix(f3, jnp.asarray(dft_np).astype(BF16), jnp.asarray(cgsg_np).astype(BF16)))
            xp = _out_proj(xp, mod, l, None, [ap.reshape(bp * sp, NA_W), fn[0].reshape(bp * sp, FN_W)], w_out)
            xs = _out_proj(xs, mod, l, ts, [a_s.reshape(bs * ts, NA_W), fn[1].reshape(bs * ts, FN_W)], w_out)
        else:
            o = l // 2
            w_in = mix1_w_in[o].astype(BF16)
            w_out = mix1_w_out[o].astype(BF16)
            outs = [(HG_KW, F32, 1.0)] * 5
            up = _in_proj(xp, mod, l, None, norm_g[l, 1], w_in, outs)
            us = _in_proj(xs, mod, l, ts, norm_g[l, 1], w_in, outs)
            yp, s_fin = _hgrn_bidir(l, bp, up, hg_lb_logits, None, hg_norm_g[o])
            ys, _ = _hgrn_bidir(l, bs, us, hg_lb_logits, state_hgrn[:, o], hg_norm_g[o])
            new_s.append(s_fin)
            xp = _out_proj(xp, mod, l, None, [yp], w_out)
            xs = _out_proj(xs, mod, l, ts, [ys], w_out)

        xp = ffn(xp, None, 2, 1, last)
        xs = ffn(xs, ts, 2, 1, last)

    return (xp.reshape(bp, sp, d), xs.reshape(bs, ts, d),
            jnp.stack(new_k, axis=1), jnp.stack(new_v, axis=1), jnp.stack(new_s, axis=1))
```

```python
import functools

import numpy as np
import jax
import jax.numpy as jnp
from jax import lax
from jax.experimental import pallas as pl
from jax.experimental.pallas import tpu as pltpu

F32 = jnp.float32
BF16 = jnp.bfloat16

D_MODEL = 1024
DEPTH = 2
GRID_W = 64
NA_HEADS = 8
NA_HD = 64
NA_W = NA_HEADS * NA_HD
NA_WIN_R = 8
NA_WIN_C = 16
FN_GROUPS = 4
FN_GW = 128
FN_W = FN_GROUPS * FN_GW
HG_HEADS = 8
HG_DK = 128
HG_DV = 128
HG_KW = HG_HEADS * HG_DK
D_FF = 2816
N_MOD = 9
EPS = 1e-6

COND_ROWS = 8
CTX_ROW = 4
TOKEN_TILE = 512
PROJ_TILE = 512
SCAN_CHUNK = 128
SUBLANES = 8
SMALL_LEVELS = (4, 2, 1)
SCAN_HEADS = 2
FAST_BLOCK = 16
FAST_LOG2_LIMIT = 126.0 / (FAST_BLOCK // 2)
MASK_VALUE = -1e30
NEG_BIG = -1e30
VMEM_LIMIT = 56 * 1024 * 1024


def _params(n_axes, vmem=VMEM_LIMIT):
    return pltpu.CompilerParams(dimension_semantics=("arbitrary",) * n_axes, vmem_limit_bytes=vmem)


def _resident(shape):
    nd = len(shape)
    return pl.BlockSpec(shape, lambda *_: (0,) * nd, pipeline_mode=pl.Buffered(1))


def _sigmoid(a):
    return 1.0 / (1.0 + jnp.exp(-a))


def _rms_mod(x, g, shift, scale):
    y = x * lax.rsqrt(jnp.mean(x * x, axis=-1, keepdims=True) + EPS)
    return (y * g) * (1.0 + scale) + shift


def _split2(x):
    hi = x.astype(BF16)
    lo = (x - hi.astype(F32)).astype(BF16)
    return hi, lo


def _split3(x):
    hi = x.astype(BF16)
    r = x - hi.astype(F32)
    mid = r.astype(BF16)
    lo = (r - mid.astype(F32)).astype(BF16)
    return hi, mid, lo


def _dot(a, b):
    return jnp.dot(a, b, preferred_element_type=F32)


def _dot_nt(a, b):
    return lax.dot_general(a, b, (((1,), (1,)), ((), ())), preferred_element_type=F32)


def _ada_kernel(cond_ref, w_ref, b_ref, o_ref):
    c = cond_ref[...]
    s = c * _sigmoid(c)
    s_hi, s_lo = _split2(s)
    w_hi, w_lo = _split2(w_ref[0])
    o_ref[0] = _dot(s_hi, w_hi) + _dot(s_hi, w_lo) + _dot(s_lo, w_hi) + b_ref[0]


def _ada_mod(cond, ada_w, ada_b):
    depth, d, n = ada_w.shape
    tn = 1152
    out = pl.pallas_call(
        _ada_kernel,
        grid=(depth, n // tn),
        in_specs=[
            pl.BlockSpec((COND_ROWS, d), lambda l, j: (0, 0)),
            pl.BlockSpec((1, d, tn), lambda l, j: (l, 0, j)),
            pl.BlockSpec((1, 1, tn), lambda l, j: (l, 0, j)),
        ],
        out_specs=pl.BlockSpec((1, COND_ROWS, tn), lambda l, j: (l, 0, j)),
        out_shape=jax.ShapeDtypeStruct((depth, COND_ROWS, n), F32),
        compiler_params=_params(2),
        name="ada_mod",
    )(cond, ada_w, ada_b.reshape(depth, 1, n))
    return out.reshape(depth, COND_ROWS, N_MOD, d)


def _mod_spec(layer, seq_len, tm):
    if seq_len is None:
        return pl.BlockSpec((1, 1, N_MOD, D_MODEL), lambda i: (layer, CTX_ROW, 0, 0))
    tiles_per_seq = seq_len // tm
    return pl.BlockSpec((1, 1, N_MOD, D_MODEL), lambda i: (layer, i // tiles_per_seq, 0, 0))


def _ffn_kernel(j, final_norm, n_acts, x_ref, mod_ref, g_ref, w1a_ref, w1b_ref, w2_ref, gf_ref, *refs):
    a_refs, w_refs, o_ref = refs[:n_acts], refs[n_acts:2 * n_acts], refs[2 * n_acts]
    x = x_ref[...]
    m = mod_ref[0, 0]
    if n_acts:
        y = None
        for a_ref, w_ref in zip(a_refs, w_refs):
            t = _dot(a_ref[...], w_ref[...])
            y = t if y is None else y + t
        x = x + m[5:6] * y
    h = _rms_mod(x, g_ref[...], m[3 * j:3 * j + 1], m[3 * j + 1:3 * j + 2]).astype(BF16)
    a = _dot(h, w1a_ref[...])
    b = _dot(h, w1b_ref[...])
    act = (a * _sigmoid(a) * b).astype(BF16)
    y = x + (0.5 * m[3 * j + 2:3 * j + 3]) * _dot(act, w2_ref[...])
    if final_norm:
        y = y * lax.rsqrt(jnp.mean(y * y, axis=-1, keepdims=True) + EPS) * gf_ref[...]
    o_ref[...] = y


def _half_ffn(x, mod, layer, seq_len, j, half, g, w1, w2, gf, final_norm, mixed=None):
    n, d = x.shape
    tm = TOKEN_TILE
    once = pl.Buffered(1)
    acts, w_out = mixed if mixed is not None else ((), None)
    width = acts[0].shape[1] if acts else 0
    mixed_specs = ([pl.BlockSpec((tm, width), lambda i: (i, 0)) for _ in acts]
                   + [pl.BlockSpec((width, d), functools.partial(lambda i, blk: (blk, 0), blk=blk),
                                   pipeline_mode=once) for blk in range(len(acts))])
    return pl.pallas_call(
        functools.partial(_ffn_kernel, j, final_norm, len(acts)),
        grid=(n // tm,),
        in_specs=[
            pl.BlockSpec((tm, d), lambda i: (i, 0)),
            _mod_spec(layer, seq_len, tm),
            _resident((1, d)),
            pl.BlockSpec((None, None, d, D_FF), lambda i: (layer, half, 0, 0), pipeline_mode=once),
            pl.BlockSpec((None, None, d, D_FF), lambda i: (layer, half, 0, 1), pipeline_mode=once),
            pl.BlockSpec((None, None, D_FF, d), lambda i: (layer, half, 0, 0), pipeline_mode=once),
            _resident((1, d)),
        ] + mixed_specs,
        out_specs=pl.BlockSpec((tm, d), lambda i: (i, 0)),
        out_shape=jax.ShapeDtypeStruct((n, d), F32),
        compiler_params=_params(1),
        name="half_ffn",
    )(x, mod, g.reshape(1, d), w1, w1, w2, gf.reshape(1, d), *acts, *([w_out] * len(acts)))


def _proj_kernel(j, splits, x_ref, mod_ref, g_ref, w_ref, *o_refs):
    m = mod_ref[0, 0]
    h = _rms_mod(x_ref[...], g_ref[...], m[3 * j:3 * j + 1], m[3 * j + 1:3 * j + 2]).astype(BF16)
    u = _dot(h, w_ref[...])
    for (lo, hi, scale), o_ref in zip(splits, o_refs):
        piece = u[:, lo:hi]
        if scale != 1.0:
            piece = piece * scale
        o_ref[...] = piece.astype(o_ref.dtype)


def _in_proj(x, mod, layer, seq_len, g, w, outs):
    n, d = x.shape
    tm = PROJ_TILE
    splits, lo = [], 0
    for width, _, scale in outs:
        splits.append((lo, lo + width, scale))
        lo += width
    return pl.pallas_call(
        functools.partial(_proj_kernel, 1, tuple(splits)),
        grid=(n // tm,),
        in_specs=[
            pl.BlockSpec((tm, d), lambda i: (i, 0)),
            _mod_spec(layer, seq_len, tm),
            _resident((1, d)),
            _resident(w.shape),
        ],
        out_specs=[pl.BlockSpec((tm, width), lambda i: (i, 0)) for width, _, _ in outs],
        out_shape=[jax.ShapeDtypeStruct((n, width), dt) for width, dt, _ in outs],
        compiler_params=_params(1),
        name="mixer_in_proj",
    )(x, mod, g.reshape(1, d), w)


def _head_pair_attention(pairs):
    lane = lax.broadcasted_iota(jnp.int32, (1, 2 * NA_HD), 1)
    first = lane < NA_HD
    scores = []
    for q, key_blocks, _, bias_blocks in pairs:
        zero = jnp.zeros_like(q)
        qm = jnp.concatenate([jnp.where(first, q, zero), jnp.where(first, zero, q)], axis=0)
        ss = []
        for kb, bias in zip(key_blocks, bias_blocks):
            s = _dot_nt(qm, kb)
            ss.append(s if bias is None else s + bias)
        scores.append(ss)
    maxes = []
    for ss in scores:
        mx = ss[0].max(axis=-1, keepdims=True)
        for s in ss[1:]:
            mx = jnp.maximum(mx, s.max(axis=-1, keepdims=True))
        maxes.append(mx)
    outs = []
    for (q, _, value_blocks, _), ss, mx in zip(pairs, scores, maxes):
        den, acc = None, None
        for s, vb in zip(ss, value_blocks):
            p = jnp.exp(s - mx)
            dsum = p.sum(axis=-1, keepdims=True)
            pv = _dot(p.astype(BF16), vb)
            den = dsum if den is None else den + dsum
            acc = pv if acc is None else acc + pv
        o = acc / den
        tq = q.shape[0]
        outs.append(jnp.where(first, o[:tq], o[tq:]))
    return outs


def _ctx_attn_kernel(q_ref, k_ref, v_ref, o_ref):
    slices = [slice(pr * 2 * NA_HD, (pr + 1) * 2 * NA_HD) for pr in range(NA_HEADS // 2)]
    outs = _head_pair_attention([(q_ref[0, :, sl], [k_ref[0, :, sl].astype(BF16)],
                                  [v_ref[0, :, sl].astype(BF16)], [None]) for sl in slices])
    for sl, o in zip(slices, outs):
        o_ref[0, :, sl] = o.astype(o_ref.dtype)


def _context_attention(q, k, v):
    b, s, w = q.shape
    blk = lambda: pl.BlockSpec((1, s, w), lambda i: (i, 0, 0))
    return pl.pallas_call(
        _ctx_attn_kernel,
        grid=(b,),
        in_specs=[blk(), blk(), blk()],
        out_specs=blk(),
        out_shape=jax.ShapeDtypeStruct((b, s, w), BF16),
        compiler_params=_params(1),
        name="context_attention",
    )(q, k, v)


def _na_kernel(rows, q_ref, k_ref, v_ref, ck_ref, cv_ref, bias_ref, o_ref):
    r = pl.program_id(1)
    r0 = jnp.clip(r - NA_WIN_R // 2, 0, rows - NA_WIN_R)
    start = pl.multiple_of(r0 * GRID_W, GRID_W)
    win = pl.ds(start, NA_WIN_R * GRID_W)
    first = (NA_WIN_R - 1) - (r - r0)
    slices = [slice(pr * 2 * NA_HD, (pr + 1) * 2 * NA_HD) for pr in range(NA_HEADS // 2)]
    pairs = []
    for pr, sl in enumerate(slices):
        bias = jnp.concatenate(
            [jnp.concatenate([bias_ref[2 * pr + hh, first + 2 * jj] for jj in range(NA_WIN_R // 2)], axis=1)
             for hh in range(2)], axis=0)
        pairs.append((q_ref[0, :, sl], [k_ref[0, win, sl], ck_ref[0, :, sl]],
                      [v_ref[0, win, sl], cv_ref[0, :, sl]], [bias, None]))
    for sl, o in zip(slices, _head_pair_attention(pairs)):
        o_ref[0, :, sl] = o.astype(o_ref.dtype)


def _na_bias_table(rpb):
    cols = np.arange(GRID_W)
    col_start = np.clip(cols - NA_WIN_C // 2, 0, GRID_W - NA_WIN_C)
    inside = (cols[None, :] >= col_start[:, None]) & (cols[None, :] < col_start[:, None] + NA_WIN_C)
    pad = GRID_W - NA_WIN_C
    padded = jnp.pad(rpb.astype(F32), ((0, 0), (0, 0), (pad, pad)))
    t = jnp.stack([padded[:, :, GRID_W - 1 - w:2 * GRID_W - 1 - w] for w in range(GRID_W)], axis=2)
    t = jnp.where(inside[None, None], t, MASK_VALUE)
    return jnp.concatenate([t[:, :-1], t[:, 1:]], axis=-1)


def _neighbourhood_attention(q, k, v, ck, cv, bias):
    b, t, w = q.shape
    rows = t // GRID_W
    p = ck.shape[1]
    seq = lambda n: pl.BlockSpec((1, n, w), lambda i, r: (i, 0, 0))
    row = lambda: pl.BlockSpec((1, GRID_W, w), lambda i, r: (i, r, 0))
    return pl.pallas_call(
        functools.partial(_na_kernel, rows),
        grid=(b, rows),
        in_specs=[row(), seq(t), seq(t), seq(p), seq(p), _resident(bias.shape)],
        out_specs=row(),
        out_shape=jax.ShapeDtypeStruct((b, t, w), BF16),
        compiler_params=_params(2),
        name="neighbourhood_attention",
    )(q, k, v, ck, cv, bias)


def _dft_tables(t):
    def cs(n):
        idx = (np.arange(n)[:, None] * np.arange(n)[None, :]) % n
        ang = 2.0 * np.pi * idx.astype(np.float64) / n
        return np.cos(ang), np.sin(ang)
    ct, st = cs(t)
    cg, sg = cs(FN_GW)
    return (np.concatenate([ct, -st], axis=1).astype(np.float32),
            np.concatenate([cg, sg], axis=1).astype(np.float32))


def _fnet_kernel(t, scale, f_ref, cgsg_ref, dft_ref, o_ref, xcs_ref):
    @pl.when(pl.program_id(1) == 0)
    def _():
        for g in range(FN_GROUPS):
            sl = slice(g * FN_GW, (g + 1) * FN_GW)
            xcs = _dot(f_ref[0, :, sl], cgsg_ref[...])
            xcs_ref[0:t, sl] = xcs[:, :FN_GW].astype(BF16)
            xcs_ref[t:2 * t, sl] = xcs[:, FN_GW:].astype(BF16)

    o_ref[0] = (_dot(dft_ref[...], xcs_ref[...]) * scale).astype(o_ref.dtype)


def _fourier_mix(f, dft, cgsg):
    b, t, w = f.shape
    tq = min(t, TOKEN_TILE)
    scale = float(1.0 / np.sqrt(float(t * FN_GW)))
    return pl.pallas_call(
        functools.partial(_fnet_kernel, t, scale),
        grid=(b, t // tq),
        in_specs=[
            pl.BlockSpec((1, t, w), lambda i, r: (i, 0, 0)),
            pl.BlockSpec(cgsg.shape, lambda i, r: (0, 0)),
            pl.BlockSpec((tq, 2 * t), lambda i, r: (r, 0)),
        ],
        out_specs=pl.BlockSpec((1, tq, w), lambda i, r: (i, r, 0)),
        out_shape=jax.ShapeDtypeStruct((b, t, w), BF16),
        scratch_shapes=[pltpu.VMEM((2 * t, w), BF16)],
        compiler_params=_params(2),
        name="fourier_mix",
    )(f, cgsg, dft)


def _ref_rows(b, b_ref, m, reverse):
    c = SCAN_CHUNK
    blk = 2 * m
    anchor = m if reverse else m - 1
    if blk >= 8:
        pieces = [jnp.broadcast_to(b_ref[p * blk + anchor:p * blk + anchor + 1, :], (blk, HG_DK))
                  for p in range(c // blk)]
        return pieces[0] if len(pieces) == 1 else jnp.concatenate(pieces, axis=0)
    pos = lax.broadcasted_iota(jnp.int32, (c, HG_DK), 0) % blk
    r = b
    for p in range(blk):
        delta = anchor - p
        if delta != 0:
            r = jnp.where(pos == p, pltpu.roll(b, (-delta) % c, 0), r)
    return r


def _level_masks():
    c = SCAN_CHUNK
    t = np.arange(c)
    left, right = [], []
    for m in SMALL_LEVELS:
        is_right = (t // m) % 2 == 1
        left.append(np.where(is_right, NEG_BIG, 0.0))
        right.append(np.where(is_right, 0.0, NEG_BIG))
    full = lambda rows: np.ascontiguousarray(
        np.broadcast_to(np.stack(rows)[:, :, None], (len(rows), c, HG_DK))).astype(np.float32)
    return full(left), full(right)


def _tri_tables():
    t = np.arange(SCAN_CHUNK)
    return np.stack([t[None, :] <= t[:, None], t[None, :] >= t[:, None]]).astype(np.float32)


def _scan_gates(z, lb, tri):
    e = jnp.exp(-jnp.abs(z))
    inv = 1.0 / (1.0 + e)
    pos_z = z >= 0
    sig = jnp.where(pos_z, inv, e * inv)
    sig_neg = jnp.where(pos_z, e * inv, inv)
    k = (1.0 - lb) * sig_neg
    log2_f = jnp.log2(lb + (1.0 - lb) * sig)
    b = None
    for part in _split3(log2_f):
        t = _dot(tri, part)
        b = t if b is None else b + t
    return k, b, jnp.min(log2_f)


def _scan_chunks(chains, ml_ref, mr_ref, fast):
    c = SCAN_CHUNK
    n_tiles = c // SUBLANES
    tile = lambda x, i: x[i * SUBLANES:(i + 1) * SUBLANES]
    row = lax.broadcasted_iota(jnp.int32, (c, c), 0)
    col = lax.broadcasted_iota(jnp.int32, (c, c), 1)
    xr = row ^ col
    zero_tile = jnp.zeros((SUBLANES, HG_DK), F32)

    os_, b_tots = [], []
    for reverse, q, v, k, b, st, b_ref in chains:
        edge = 0 if reverse else c - 1
        b_tots.append(b_ref[edge:edge + 1, :])
        os_.append(_dot_nt((q * jnp.exp2(b)).astype(BF16), st.astype(BF16)))

    a_tiles = [[None] * n_tiles for _ in chains]
    m = c // 2
    while m >= (FAST_BLOCK if fast else SUBLANES):
        g = m // SUBLANES
        for ci, (reverse, q, v, k, b, st, b_ref) in enumerate(chains):
            q_ids, qe_tiles, ke_tiles = [], [], [zero_tile] * n_tiles
            for p in range(n_tiles // (2 * g)):
                left = range(2 * g * p, 2 * g * p + g)
                right = range(2 * g * p + g, 2 * g * (p + 1))
                anchor = (2 * g * p + g) * SUBLANES - (0 if reverse else 1)
                r = jnp.broadcast_to(b_ref[anchor:anchor + 1, :], (SUBLANES, HG_DK))
                q_side, k_side = (left, right) if reverse else (right, left)
                for i in q_side:
                    q_ids.append(i)
                    qe_tiles.append(tile(q, i) * jnp.exp2(tile(b, i) - r))
                for i in k_side:
                    ke_tiles[i] = tile(k, i) * jnp.exp2(r - tile(b, i))
            lvl = _dot_nt(jnp.concatenate(qe_tiles, axis=0).astype(BF16),
                          jnp.concatenate(ke_tiles, axis=0).astype(BF16))
            for j, i in enumerate(q_ids):
                old = a_tiles[ci][i]
                a_tiles[ci][i] = tile(lvl, j) if old is None else jnp.where(tile(xr, i) < 2 * m, tile(lvl, j), old)
        m //= 2
    a_s = [jnp.concatenate([t if t is not None else zero_tile for t in tiles], axis=0) for tiles in a_tiles]

    if fast:
        for ci, (reverse, q, v, k, b, st, b_ref) in enumerate(chains):
            mid = FAST_BLOCK // 2
            r = jnp.concatenate(
                [jnp.broadcast_to(b_ref[j * FAST_BLOCK + mid:j * FAST_BLOCK + mid + 1, :], (FAST_BLOCK, HG_DK))
                 for j in range(c // FAST_BLOCK)], axis=0)
            qe = (q * jnp.exp2(b - r)).astype(BF16)
            ke = (k * jnp.exp2(r - b)).astype(BF16)
            a_s[ci] = jnp.where(xr < FAST_BLOCK, _dot_nt(qe, ke), a_s[ci])
    else:
        for level, m in enumerate(SMALL_LEVELS):
            for ci, (reverse, q, v, k, b, st, b_ref) in enumerate(chains):
                d = b - _ref_rows(b, b_ref, m, reverse)
                mask_q, mask_k = (ml_ref, mr_ref) if reverse else (mr_ref, ml_ref)
                qe = (q * jnp.exp2(d + mask_q[level])).astype(BF16)
                ke = (k * jnp.exp2(mask_k[level] - d)).astype(BF16)
                a_s[ci] = jnp.where(xr < 2 * m, _dot_nt(qe, ke), a_s[ci])

    out = []
    for (reverse, q, v, k, b, st, b_ref), a, o, b_tot in zip(chains, a_s, os_, b_tots):
        if fast:
            a = jnp.where((col >= row) if reverse else (col <= row), a, 0.0)
        else:
            diag = jnp.where(xr == 0, _dot_nt(q.astype(BF16), k.astype(BF16)), 0.0)
            a = jnp.where((col > row) if reverse else (col < row), a, diag)
        o = o + _dot(a.astype(BF16), v.astype(BF16))
        k_end = (k * jnp.exp2(b_tot - b)).astype(BF16)
        st_new = st * jnp.exp2(b_tot) + _dot(v.T.astype(BF16), k_end)
        out.append((o, st_new))
    return out


def _hgrn_kernel(layer, n_chunks, has_s0, *refs):
    refs = list(refs)
    q_ref, zf_ref, zb_ref, v_ref, g_ref, lg_ref, ng_ref, ml_ref, mr_ref, tri_ref = refs[:10]
    del refs[:10]
    s0_ref = refs.pop(0) if has_s0 else None
    y_ref, sfin_ref = refs[:2]
    del refs[:2]
    n_chains = 2 * SCAN_HEADS
    st_refs, b_refs, k_refs, o_refs = (refs[i * n_chains:(i + 1) * n_chains] for i in range(4))
    fast_ref = refs[4 * n_chains]
    chains = [(hh, direction) for hh in range(SCAN_HEADS) for direction in range(2)]
    z_refs = (zf_ref, zb_ref)
    lanes = lambda hh: slice(hh * HG_DK, (hh + 1) * HG_DK)

    for ci, (hh, direction) in enumerate(chains):
        if has_s0:
            st_refs[ci][...] = s0_ref[0, direction, hh].T
        else:
            st_refs[ci][...] = jnp.zeros((HG_DV, HG_DK), F32)

    def lower_bound(hh, direction):
        lg = lg_ref[:, direction, hh, 0, :]
        ex = jnp.exp(lg - lg.max(axis=0, keepdims=True))
        return ex[1:layer + 1].sum(axis=0, keepdims=True) / ex.sum(axis=0, keepdims=True)

    lbs = [lower_bound(hh, direction) for hh, direction in chains]

    def chunk_rows(c):
        return pl.ds(pl.multiple_of(c * SCAN_CHUNK, SCAN_CHUNK), SCAN_CHUNK)

    def gates(c_fwd, c_bwd):
        rows = (chunk_rows(c_fwd), chunk_rows(c_bwd))
        return [_scan_gates(z_refs[direction][rows[direction], lanes(hh)], lb, tri_ref[direction])
                for (hh, direction), lb in zip(chains, lbs)]

    def store_gates(gs):
        lo = None
        for (k, b, lo_c), k_ref, b_ref in zip(gs, k_refs, b_refs):
            k_ref[...] = k
            b_ref[...] = b
            lo = lo_c if lo is None else jnp.minimum(lo, lo_c)
        fast_ref[0] = (lo >= -FAST_LOG2_LIMIT).astype(jnp.int32)

    store_gates(gates(0, n_chunks - 1))

    def trip(c, fast):
        rows = (chunk_rows(c), chunk_rows(n_chunks - 1 - c))
        nxt = gates(jnp.minimum(c + 1, n_chunks - 1), jnp.maximum(n_chunks - 2 - c, 0))
        outs = _scan_chunks(
            [(direction == 1, q_ref[rows[direction], lanes(hh)], v_ref[rows[direction], lanes(hh)],
              k_refs[ci][...], b_refs[ci][...], st_refs[ci][...], b_refs[ci])
             for ci, (hh, direction) in enumerate(chains)],
            ml_ref, mr_ref, fast)
        for ci, ((hh, direction), (o, st)) in enumerate(zip(chains, outs)):
            st_refs[ci][...] = st
            o_refs[ci][rows[direction], :] = o
        store_gates(nxt)

    def scan_body(c, carry):
        use_fast = fast_ref[0] == 1
        pl.when(use_fast)(lambda: trip(c, True))
        pl.when(jnp.logical_not(use_fast))(lambda: trip(c, False))
        return carry

    lax.fori_loop(0, n_chunks, scan_body, 0)

    def gate_body(c, carry):
        rows = chunk_rows(c)
        for hh in range(SCAN_HEADS):
            o = o_refs[2 * hh][rows, :] + o_refs[2 * hh + 1][rows, :]
            o = o * lax.rsqrt(jnp.mean(o * o, axis=-1, keepdims=True) + EPS) * ng_ref[...]
            g = g_ref[rows, lanes(hh)]
            y_ref[rows, lanes(hh)] = (o * (g * _sigmoid(g))).astype(y_ref.dtype)
        return carry

    lax.fori_loop(0, n_chunks, gate_body, 0)

    for ci, (hh, direction) in enumerate(chains):
        sfin_ref[0, direction, hh] = st_refs[ci][...].T


def _hgrn_bidir(layer, seqs, u, logits, s0, norm_g):
    q, zf, zb, v, g = u
    n = q.shape[0]
    t = n // seqs
    n_chunks = t // SCAN_CHUNK
    n_chains = 2 * SCAN_HEADS
    ml, mr = _level_masks()
    tri = _tri_tables()
    tok = lambda: pl.BlockSpec((t, SCAN_HEADS * HG_DK), lambda s, h: (s, h))
    state = lambda: pl.BlockSpec((1, 2, SCAN_HEADS, HG_DK, HG_DV), lambda s, h: (s, 0, h, 0, 0))
    in_specs = [tok(), tok(), tok(), tok(), tok(),
                pl.BlockSpec((DEPTH, 2, SCAN_HEADS, 1, HG_DK), lambda s, h: (0, 0, h, 0, 0)),
                _resident((1, HG_DV)), _resident(ml.shape), _resident(mr.shape), _resident(tri.shape)]
    args = [q, zf, zb, v, g, logits.reshape(DEPTH, 2, HG_HEADS, 1, HG_DK), norm_g.reshape(1, HG_DV),
            jnp.asarray(ml), jnp.asarray(mr), jnp.asarray(tri).astype(BF16)]
    if s0 is not None:
        in_specs.append(state())
        args.append(s0)
    return pl.pallas_call(
        functools.partial(_hgrn_kernel, layer, n_chunks, s0 is not None),
        grid=(seqs, HG_HEADS // SCAN_HEADS),
        in_specs=in_specs,
        out_specs=[tok(), state()],
        out_shape=[jax.ShapeDtypeStruct((n, HG_KW), BF16),
                   jax.ShapeDtypeStruct((seqs, 2, HG_HEADS, HG_DK, HG_DV), F32)],
        scratch_shapes=([pltpu.VMEM((HG_DV, HG_DK), F32)] * n_chains
                        + [pltpu.VMEM((SCAN_CHUNK, HG_DK), F32)] * n_chains
                        + [pltpu.VMEM((SCAN_CHUNK, HG_DK), F32)] * n_chains
                        + [pltpu.VMEM((t, HG_DV), F32)] * n_chains
                        + [pltpu.SMEM((1,), jnp.int32)]),
        compiler_params=_params(2),
        name="hgrn2_bidir",
    )(*args)


def kernel(x_prompt, x_sample, cache_k, cache_v, state_hgrn, c, c_ctx, ada_w, ada_b, norm_g, ffn_w1, ffn_w2,
           mix0_w_in, mix0_w_out, na_rpb, mix1_w_in, mix1_w_out, hg_lb_logits, hg_norm_g, norm_f):
    bp, sp, d = x_prompt.shape
    bs, ts, _ = x_sample.shape
    xp = x_prompt.reshape(bp * sp, d)
    xs = x_sample.reshape(bs * ts, d)

    cond = jnp.zeros((COND_ROWS, d), F32).at[:bs].set(c).at[CTX_ROW].set(c_ctx)
    mod = _ada_mod(cond, ada_w, ada_b)

    w1 = ffn_w1.astype(BF16)
    w2 = ffn_w2.astype(BF16)

    new_k, new_v, new_s = [], [], []
    for l in range(DEPTH):
        last = l == DEPTH - 1

        def ffn(x, seq_len, j, half, final, mixed=None):
            return _half_ffn(x, mod, l, seq_len, j, half, norm_g[l, j], w1, w2, norm_f, final, mixed)

        xp = ffn(xp, None, 0, 0, False)
        xs = ffn(xs, ts, 0, 0, False)

        if l % 2 == 0:
            e = l // 2
            w_in = mix0_w_in[e].astype(BF16)
            w_out = mix0_w_out[e].astype(BF16)
            qk_scale = float(NA_HD) ** -0.5
            qp, kp, vp, fp = _in_proj(xp, mod, l, None, norm_g[l, 1], w_in,
                                      [(NA_W, BF16, qk_scale), (NA_W, F32, 1.0), (NA_W, F32, 1.0), (FN_W, BF16, 1.0)])
            qs, ks, vs, fs = _in_proj(xs, mod, l, ts, norm_g[l, 1], w_in,
                                      [(NA_W, BF16, qk_scale), (NA_W, BF16, 1.0), (NA_W, BF16, 1.0), (FN_W, BF16, 1.0)])
            new_k.append(kp.reshape(bp, sp, NA_HEADS, NA_HD))
            new_v.append(vp.reshape(bp, sp, NA_HEADS, NA_HD))

            r3 = lambda a, b_, t_: a.reshape(b_, t_, a.shape[-1])
            ap = _context_attention(r3(qp, bp, sp), r3(kp, bp, sp), r3(vp, bp, sp))
            ck = cache_k[:, e].reshape(bs, -1, NA_W).astype(BF16)
            cv = cache_v[:, e].reshape(bs, -1, NA_W).astype(BF16)
            a_s = _neighbourhood_attention(r3(qs, bs, ts), r3(ks, bs, ts), r3(vs, bs, ts), ck, cv,
                                           _na_bias_table(na_rpb[e]))
            cgsg = None
            fn = []
            for f3 in (r3(fp, bp, sp), r3(fs, bs, ts)):
                dft_np, cgsg_np = _dft_tables(f3.shape[1])
                fn.append(_fourier_mix(f3, jnp.asarray(dft_np).astype(BF16), jnp.asarray(cgsg_np).astype(BF16)))
            mixed_p = ([ap.reshape(bp * sp, NA_W), fn[0].reshape(bp * sp, FN_W)], w_out)
            mixed_s = ([a_s.reshape(bs * ts, NA_W), fn[1].reshape(bs * ts, FN_W)], w_out)
        else:
            o = l // 2
            w_in = mix1_w_in[o].astype(BF16)
            w_out = mix1_w_out[o].astype(BF16)
            outs = [(HG_KW, F32, 1.0)] * 5
            up = _in_proj(xp, mod, l, None, norm_g[l, 1], w_in, outs)
            us = _in_proj(xs, mod, l, ts, norm_g[l, 1], w_in, outs)
            yp, s_fin = _hgrn_bidir(l, bp, up, hg_lb_logits, None, hg_norm_g[o])
            ys, _ = _hgrn_bidir(l, bs, us, hg_lb_logits, state_hgrn[:, o], hg_norm_g[o])
            new_s.append(s_fin)
            mixed_p, mixed_s = ([yp], w_out), ([ys], w_out)

        xp = ffn(xp, None, 2, 1, last, mixed_p)
        xs = ffn(xs, ts, 2, 1, last, mixed_s)

    return (xp.reshape(bp, sp, d), xs.reshape(bs, ts, d),
            jnp.stack(new_k, axis=1), jnp.stack(new_v, axis=1), jnp.stack(new_s, axis=1))
```

```python
import functools

import numpy as np
import jax
import jax.numpy as jnp
from jax import lax
from jax.experimental import pallas as pl
from jax.experimental.pallas import tpu as pltpu

F32 = jnp.float32
BF16 = jnp.bfloat16

D_MODEL = 1024
DEPTH = 2
GRID_W = 64
NA_HEADS = 8
NA_HD = 64
NA_W = NA_HEADS * NA_HD
NA_WIN_R = 8
NA_WIN_C = 16
CTX_SEQS_PER_STEP = 2
NA_ROWS_PER_STEP = 4
FN_GROUPS = 4
FN_GW = 128
FN_W = FN_GROUPS * FN_GW
HG_HEADS = 8
HG_DK = 128
HG_DV = 128
HG_KW = HG_HEADS * HG_DK
D_FF = 2816
N_MOD = 9
EPS = 1e-6

COND_ROWS = 8
CTX_ROW = 4
TOKEN_TILE = 512
PROJ_TILE = 512
SCAN_CHUNK = 128
SUBLANES = 8
SMALL_LEVELS = (4, 2, 1)
SCAN_HEADS_MAX = 4
SCAN_WINDOW_BYTES = 24 * 1024 * 1024
FAST_BLOCK = 16
FAST_LOG2_LIMIT = 126.0 / (FAST_BLOCK // 2)
MASK_VALUE = -1e30
NEG_BIG = -1e30
VMEM_LIMIT = 56 * 1024 * 1024


def _params(n_axes, vmem=VMEM_LIMIT):
    return pltpu.CompilerParams(dimension_semantics=("arbitrary",) * n_axes, vmem_limit_bytes=vmem)


def _resident(shape):
    nd = len(shape)
    return pl.BlockSpec(shape, lambda *_: (0,) * nd, pipeline_mode=pl.Buffered(1))


def _sigmoid(a):
    return 1.0 / (1.0 + jnp.exp(-a))


def _rms_mod(x, g, shift, scale):
    y = x * lax.rsqrt(jnp.mean(x * x, axis=-1, keepdims=True) + EPS)
    return (y * g) * (1.0 + scale) + shift


def _split2(x):
    hi = x.astype(BF16)
    lo = (x - hi.astype(F32)).astype(BF16)
    return hi, lo


def _split3(x):
    hi = x.astype(BF16)
    r = x - hi.astype(F32)
    mid = r.astype(BF16)
    lo = (r - mid.astype(F32)).astype(BF16)
    return hi, mid, lo


def _dot(a, b):
    return jnp.dot(a, b, preferred_element_type=F32)


def _dot_nt(a, b):
    return lax.dot_general(a, b, (((1,), (1,)), ((), ())), preferred_element_type=F32)


def _ada_kernel(cond_ref, w_ref, b_ref, o_ref):
    c = cond_ref[...]
    s = c * _sigmoid(c)
    s_hi, s_lo = _split2(s)
    w_hi, w_lo = _split2(w_ref[0])
    o_ref[0] = _dot(s_hi, w_hi) + _dot(s_hi, w_lo) + _dot(s_lo, w_hi) + b_ref[0]


def _ada_mod(cond, ada_w, ada_b):
    depth, d, n = ada_w.shape
    tn = 1152
    out = pl.pallas_call(
        _ada_kernel,
        grid=(depth, n // tn),
        in_specs=[
            pl.BlockSpec((COND_ROWS, d), lambda l, j: (0, 0)),
            pl.BlockSpec((1, d, tn), lambda l, j: (l, 0, j)),
            pl.BlockSpec((1, 1, tn), lambda l, j: (l, 0, j)),
        ],
        out_specs=pl.BlockSpec((1, COND_ROWS, tn), lambda l, j: (l, 0, j)),
        out_shape=jax.ShapeDtypeStruct((depth, COND_ROWS, n), F32),
        compiler_params=_params(2),
        name="ada_mod",
    )(cond, ada_w, ada_b.reshape(depth, 1, n))
    return out.reshape(depth, COND_ROWS, N_MOD, d)


def _mod_spec(layer, seq_len, tm):
    if seq_len is None:
        return pl.BlockSpec((1, 1, N_MOD, D_MODEL), lambda i: (layer, CTX_ROW, 0, 0))
    tiles_per_seq = seq_len // tm
    return pl.BlockSpec((1, 1, N_MOD, D_MODEL), lambda i: (layer, i // tiles_per_seq, 0, 0))


def _ffn_kernel(j, final_norm, n_acts, x_ref, mod_ref, g_ref, w1a_ref, w1b_ref, w2_ref, gf_ref, *refs):
    a_refs, w_refs, o_ref = refs[:n_acts], refs[n_acts:2 * n_acts], refs[2 * n_acts]
    x = x_ref[...]
    m = mod_ref[0, 0]
    if n_acts:
        y = None
        for a_ref, w_ref in zip(a_refs, w_refs):
            t = _dot(a_ref[...], w_ref[...])
            y = t if y is None else y + t
        x = x + m[5:6] * y
    h = _rms_mod(x, g_ref[...], m[3 * j:3 * j + 1], m[3 * j + 1:3 * j + 2]).astype(BF16)
    a = _dot(h, w1a_ref[...])
    b = _dot(h, w1b_ref[...])
    act = (a * _sigmoid(a) * b).astype(BF16)
    y = x + (0.5 * m[3 * j + 2:3 * j + 3]) * _dot(act, w2_ref[...])
    if final_norm:
        y = y * lax.rsqrt(jnp.mean(y * y, axis=-1, keepdims=True) + EPS) * gf_ref[...]
    o_ref[...] = y


def _half_ffn(x, mod, layer, seq_len, j, half, g, w1, w2, gf, final_norm, mixed=None):
    n, d = x.shape
    tm = TOKEN_TILE
    once = pl.Buffered(1)
    acts, w_out = mixed if mixed is not None else ((), None)
    width = acts[0].shape[1] if acts else 0
    mixed_specs = ([pl.BlockSpec((tm, width), lambda i: (i, 0)) for _ in acts]
                   + [pl.BlockSpec((width, d), functools.partial(lambda i, blk: (blk, 0), blk=blk),
                                   pipeline_mode=once) for blk in range(len(acts))])
    return pl.pallas_call(
        functools.partial(_ffn_kernel, j, final_norm, len(acts)),
        grid=(n // tm,),
        in_specs=[
            pl.BlockSpec((tm, d), lambda i: (i, 0)),
            _mod_spec(layer, seq_len, tm),
            _resident((1, d)),
            pl.BlockSpec((None, None, d, D_FF), lambda i: (layer, half, 0, 0), pipeline_mode=once),
            pl.BlockSpec((None, None, d, D_FF), lambda i: (layer, half, 0, 1), pipeline_mode=once),
            pl.BlockSpec((None, None, D_FF, d), lambda i: (layer, half, 0, 0), pipeline_mode=once),
            _resident((1, d)),
        ] + mixed_specs,
        out_specs=pl.BlockSpec((tm, d), lambda i: (i, 0)),
        out_shape=jax.ShapeDtypeStruct((n, d), F32),
        compiler_params=_params(1),
        name="half_ffn",
    )(x, mod, g.reshape(1, d), w1, w1, w2, gf.reshape(1, d), *acts, *([w_out] * len(acts)))


def _proj_kernel(j, splits, x_ref, mod_ref, g_ref, w_ref, *o_refs):
    m = mod_ref[0, 0]
    h = _rms_mod(x_ref[...], g_ref[...], m[3 * j:3 * j + 1], m[3 * j + 1:3 * j + 2]).astype(BF16)
    u = _dot(h, w_ref[...])
    for (lo, hi, scale), o_ref in zip(splits, o_refs):
        piece = u[:, lo:hi]
        if scale != 1.0:
            piece = piece * scale
        o_ref[...] = piece.astype(o_ref.dtype)


def _in_proj(x, mod, layer, seq_len, g, w, outs):
    n, d = x.shape
    tm = PROJ_TILE
    splits, lo = [], 0
    for width, _, scale in outs:
        splits.append((lo, lo + width, scale))
        lo += width
    return pl.pallas_call(
        functools.partial(_proj_kernel, 1, tuple(splits)),
        grid=(n // tm,),
        in_specs=[
            pl.BlockSpec((tm, d), lambda i: (i, 0)),
            _mod_spec(layer, seq_len, tm),
            _resident((1, d)),
            _resident(w.shape),
        ],
        out_specs=[pl.BlockSpec((tm, width), lambda i: (i, 0)) for width, _, _ in outs],
        out_shape=[jax.ShapeDtypeStruct((n, width), dt) for width, dt, _ in outs],
        compiler_params=_params(1),
        name="mixer_in_proj",
    )(x, mod, g.reshape(1, d), w)


def _head_pair_attention(pairs):
    lane = lax.broadcasted_iota(jnp.int32, (1, 2 * NA_HD), 1)
    first = lane < NA_HD
    scores = []
    for q, key_blocks, _, bias_blocks in pairs:
        zero = jnp.zeros_like(q)
        qm = jnp.concatenate([jnp.where(first, q, zero), jnp.where(first, zero, q)], axis=0)
        ss = []
        for kb, bias in zip(key_blocks, bias_blocks):
            s = _dot_nt(qm, kb)
            ss.append(s if bias is None else s + bias)
        scores.append(ss)
    maxes = []
    for ss in scores:
        mx = ss[0].max(axis=-1, keepdims=True)
        for s in ss[1:]:
            mx = jnp.maximum(mx, s.max(axis=-1, keepdims=True))
        maxes.append(mx)
    outs = []
    for (q, _, value_blocks, _), ss, mx in zip(pairs, scores, maxes):
        den, acc = None, None
        for s, vb in zip(ss, value_blocks):
            p = jnp.exp(s - mx)
            dsum = p.sum(axis=-1, keepdims=True)
            pv = _dot(p.astype(BF16), vb)
            den = dsum if den is None else den + dsum
            acc = pv if acc is None else acc + pv
        o = acc / den
        tq = q.shape[0]
        outs.append(jnp.where(first, o[:tq], o[tq:]))
    return outs


def _ctx_attn_kernel(q_ref, k_ref, v_ref, o_ref):
    dests = [(i, slice(pr * 2 * NA_HD, (pr + 1) * 2 * NA_HD))
             for i in range(CTX_SEQS_PER_STEP) for pr in range(NA_HEADS // 2)]
    outs = _head_pair_attention([(q_ref[i, :, sl], [k_ref[i, :, sl].astype(BF16)],
                                  [v_ref[i, :, sl].astype(BF16)], [None]) for i, sl in dests])
    for (i, sl), o in zip(dests, outs):
        o_ref[i, :, sl] = o.astype(o_ref.dtype)


def _context_attention(q, k, v):
    b, s, w = q.shape
    blk = lambda: pl.BlockSpec((CTX_SEQS_PER_STEP, s, w), lambda i: (i, 0, 0))
    return pl.pallas_call(
        _ctx_attn_kernel,
        grid=(b // CTX_SEQS_PER_STEP,),
        in_specs=[blk(), blk(), blk()],
        out_specs=blk(),
        out_shape=jax.ShapeDtypeStruct((b, s, w), BF16),
        compiler_params=_params(1),
        name="context_attention",
    )(q, k, v)


def _na_kernel(rows, q_ref, k_ref, v_ref, ck_ref, cv_ref, bias_ref, o_ref):
    slices = [slice(pr * 2 * NA_HD, (pr + 1) * 2 * NA_HD) for pr in range(NA_HEADS // 2)]
    pairs, dests = [], []
    for rr in range(NA_ROWS_PER_STEP):
        r = pl.program_id(1) * NA_ROWS_PER_STEP + rr
        r0 = jnp.clip(r - NA_WIN_R // 2, 0, rows - NA_WIN_R)
        start = pl.multiple_of(r0 * GRID_W, GRID_W)
        win = pl.ds(start, NA_WIN_R * GRID_W)
        first = (NA_WIN_R - 1) - (r - r0)
        q_rows = slice(rr * GRID_W, (rr + 1) * GRID_W)
        for pr, sl in enumerate(slices):
            bias = jnp.concatenate(
                [jnp.concatenate([bias_ref[2 * pr + hh, first + 2 * jj] for jj in range(NA_WIN_R // 2)], axis=1)
                 for hh in range(2)], axis=0)
            pairs.append((q_ref[0, q_rows, sl], [k_ref[0, win, sl], ck_ref[0, :, sl]],
                          [v_ref[0, win, sl], cv_ref[0, :, sl]], [bias, None]))
            dests.append((q_rows, sl))
    for (q_rows, sl), o in zip(dests, _head_pair_attention(pairs)):
        o_ref[0, q_rows, sl] = o.astype(o_ref.dtype)


def _na_bias_table(rpb):
    cols = np.arange(GRID_W)
    col_start = np.clip(cols - NA_WIN_C // 2, 0, GRID_W - NA_WIN_C)
    inside = (cols[None, :] >= col_start[:, None]) & (cols[None, :] < col_start[:, None] + NA_WIN_C)
    pad = GRID_W - NA_WIN_C
    padded = jnp.pad(rpb.astype(F32), ((0, 0), (0, 0), (pad, pad)))
    t = jnp.stack([padded[:, :, GRID_W - 1 - w:2 * GRID_W - 1 - w] for w in range(GRID_W)], axis=2)
    t = jnp.where(inside[None, None], t, MASK_VALUE)
    return jnp.concatenate([t[:, :-1], t[:, 1:]], axis=-1)


def _neighbourhood_attention(q, k, v, ck, cv, bias):
    b, t, w = q.shape
    rows = t // GRID_W
    p = ck.shape[1]
    seq = lambda n: pl.BlockSpec((1, n, w), lambda i, r: (i, 0, 0))
    row = lambda: pl.BlockSpec((1, NA_ROWS_PER_STEP * GRID_W, w), lambda i, r: (i, r, 0))
    return pl.pallas_call(
        functools.partial(_na_kernel, rows),
        grid=(b, rows // NA_ROWS_PER_STEP),
        in_specs=[row(), seq(t), seq(t), seq(p), seq(p), _resident(bias.shape)],
        out_specs=row(),
        out_shape=jax.ShapeDtypeStruct((b, t, w), BF16),
        compiler_params=_params(2),
        name="neighbourhood_attention",
    )(q, k, v, ck, cv, bias)


def _dft_tables(t):
    def cs(n):
        idx = (np.arange(n)[:, None] * np.arange(n)[None, :]) % n
        ang = 2.0 * np.pi * idx.astype(np.float64) / n
        return np.cos(ang), np.sin(ang)
    ct, st = cs(t)
    cg, sg = cs(FN_GW)
    return (np.concatenate([ct, -st], axis=1).astype(np.float32),
            np.concatenate([cg, sg], axis=1).astype(np.float32))


def _fnet_kernel(t, scale, f_ref, cgsg_ref, dft_ref, o_ref, xcs_ref):
    @pl.when(pl.program_id(1) == 0)
    def _():
        for g in range(FN_GROUPS):
            sl = slice(g * FN_GW, (g + 1) * FN_GW)
            xcs = _dot(f_ref[0, :, sl], cgsg_ref[...])
            xcs_ref[0:t, sl] = xcs[:, :FN_GW].astype(BF16)
            xcs_ref[t:2 * t, sl] = xcs[:, FN_GW:].astype(BF16)

    o_ref[0] = (_dot(dft_ref[...], xcs_ref[...]) * scale).astype(o_ref.dtype)


def _fourier_mix(f, dft, cgsg):
    b, t, w = f.shape
    tq = min(t, TOKEN_TILE)
    scale = float(1.0 / np.sqrt(float(t * FN_GW)))
    return pl.pallas_call(
        functools.partial(_fnet_kernel, t, scale),
        grid=(b, t // tq),
        in_specs=[
            pl.BlockSpec((1, t, w), lambda i, r: (i, 0, 0)),
            pl.BlockSpec(cgsg.shape, lambda i, r: (0, 0)),
            pl.BlockSpec((tq, 2 * t), lambda i, r: (r, 0)),
        ],
        out_specs=pl.BlockSpec((1, tq, w), lambda i, r: (i, r, 0)),
        out_shape=jax.ShapeDtypeStruct((b, t, w), BF16),
        scratch_shapes=[pltpu.VMEM((2 * t, w), BF16)],
        compiler_params=_params(2),
        name="fourier_mix",
    )(f, cgsg, dft)


def _ref_rows(b, b_ref, m, reverse):
    c = SCAN_CHUNK
    blk = 2 * m
    anchor = m if reverse else m - 1
    if blk >= 8:
        pieces = [jnp.broadcast_to(b_ref[p * blk + anchor:p * blk + anchor + 1, :], (blk, HG_DK))
                  for p in range(c // blk)]
        return pieces[0] if len(pieces) == 1 else jnp.concatenate(pieces, axis=0)
    pos = lax.broadcasted_iota(jnp.int32, (c, HG_DK), 0) % blk
    r = b
    for p in range(blk):
        delta = anchor - p
        if delta != 0:
            r = jnp.where(pos == p, pltpu.roll(b, (-delta) % c, 0), r)
    return r


def _level_masks():
    c = SCAN_CHUNK
    t = np.arange(c)
    left, right = [], []
    for m in SMALL_LEVELS:
        is_right = (t // m) % 2 == 1
        left.append(np.where(is_right, NEG_BIG, 0.0))
        right.append(np.where(is_right, 0.0, NEG_BIG))
    full = lambda rows: np.ascontiguousarray(
        np.broadcast_to(np.stack(rows)[:, :, None], (len(rows), c, HG_DK))).astype(np.float32)
    return full(left), full(right)


def _tri_tables():
    t = np.arange(SCAN_CHUNK)
    return np.stack([t[None, :] <= t[:, None], t[None, :] >= t[:, None]]).astype(np.float32)


def _scan_gates(z, lb, tri):
    e = jnp.exp(-jnp.abs(z))
    inv = 1.0 / (1.0 + e)
    pos_z = z >= 0
    sig = jnp.where(pos_z, inv, e * inv)
    sig_neg = jnp.where(pos_z, e * inv, inv)
    k = (1.0 - lb) * sig_neg
    log2_f = jnp.log2(lb + (1.0 - lb) * sig)
    b = None
    for part in _split3(log2_f):
        t = _dot(tri, part)
        b = t if b is None else b + t
    return k, b, jnp.min(log2_f)


def _scan_chunks(chains, ml_ref, mr_ref, fast):
    c = SCAN_CHUNK
    n_tiles = c // SUBLANES
    tile = lambda x, i: x[i * SUBLANES:(i + 1) * SUBLANES]
    row = lax.broadcasted_iota(jnp.int32, (c, c), 0)
    col = lax.broadcasted_iota(jnp.int32, (c, c), 1)
    xr = row ^ col
    zero_tile = jnp.zeros((SUBLANES, HG_DK), F32)

    os_, b_tots = [], []
    for reverse, q, v, k, b, st, b_ref in chains:
        edge = 0 if reverse else c - 1
        b_tots.append(b_ref[edge:edge + 1, :])
        os_.append(_dot_nt((q * jnp.exp2(b)).astype(BF16), st.astype(BF16)))

    a_tiles = [[None] * n_tiles for _ in chains]
    m = c // 2
    while m >= (FAST_BLOCK if fast else SUBLANES):
        g = m // SUBLANES
        for ci, (reverse, q, v, k, b, st, b_ref) in enumerate(chains):
            q_ids, qe_tiles, ke_tiles = [], [], [zero_tile] * n_tiles
            for p in range(n_tiles // (2 * g)):
                left = range(2 * g * p, 2 * g * p + g)
                right = range(2 * g * p + g, 2 * g * (p + 1))
                anchor = (2 * g * p + g) * SUBLANES - (0 if reverse else 1)
                r = jnp.broadcast_to(b_ref[anchor:anchor + 1, :], (SUBLANES, HG_DK))
                q_side, k_side = (left, right) if reverse else (right, left)
                for i in q_side:
                    q_ids.append(i)
                    qe_tiles.append(tile(q, i) * jnp.exp2(tile(b, i) - r))
                for i in k_side:
                    ke_tiles[i] = tile(k, i) * jnp.exp2(r - tile(b, i))
            lvl = _dot_nt(jnp.concatenate(qe_tiles, axis=0).astype(BF16),
                          jnp.concatenate(ke_tiles, axis=0).astype(BF16))
            for j, i in enumerate(q_ids):
                old = a_tiles[ci][i]
                a_tiles[ci][i] = tile(lvl, j) if old is None else jnp.where(tile(xr, i) < 2 * m, tile(lvl, j), old)
        m //= 2
    a_s = [jnp.concatenate([t if t is not None else zero_tile for t in tiles], axis=0) for tiles in a_tiles]

    if fast:
        for ci, (reverse, q, v, k, b, st, b_ref) in enumerate(chains):
            mid = FAST_BLOCK // 2
            r = jnp.concatenate(
                [jnp.broadcast_to(b_ref[j * FAST_BLOCK + mid:j * FAST_BLOCK + mid + 1, :], (FAST_BLOCK, HG_DK))
                 for j in range(c // FAST_BLOCK)], axis=0)
            qe = (q * jnp.exp2(b - r)).astype(BF16)
            ke = (k * jnp.exp2(r - b)).astype(BF16)
            a_s[ci] = jnp.where(xr < FAST_BLOCK, _dot_nt(qe, ke), a_s[ci])
    else:
        for level, m in enumerate(SMALL_LEVELS):
            for ci, (reverse, q, v, k, b, st, b_ref) in enumerate(chains):
                d = b - _ref_rows(b, b_ref, m, reverse)
                mask_q, mask_k = (ml_ref, mr_ref) if reverse else (mr_ref, ml_ref)
                qe = (q * jnp.exp2(d + mask_q[level])).astype(BF16)
                ke = (k * jnp.exp2(mask_k[level] - d)).astype(BF16)
                a_s[ci] = jnp.where(xr < 2 * m, _dot_nt(qe, ke), a_s[ci])

    out = []
    for (reverse, q, v, k, b, st, b_ref), a, o, b_tot in zip(chains, a_s, os_, b_tots):
        if fast:
            a = jnp.where((col >= row) if reverse else (col <= row), a, 0.0)
        else:
            diag = jnp.where(xr == 0, _dot_nt(q.astype(BF16), k.astype(BF16)), 0.0)
            a = jnp.where((col > row) if reverse else (col < row), a, diag)
        o = o + _dot(a.astype(BF16), v.astype(BF16))
        k_end = (k * jnp.exp2(b_tot - b)).astype(BF16)
        st_new = st * jnp.exp2(b_tot) + _dot(v.T.astype(BF16), k_end)
        out.append((o, st_new))
    return out


def _hgrn_kernel(layer, n_chunks, n_heads, has_s0, *refs):
    refs = list(refs)
    q_ref, zf_ref, zb_ref, v_ref, g_ref, lg_ref, ng_ref, ml_ref, mr_ref, tri_ref = refs[:10]
    del refs[:10]
    s0_ref = refs.pop(0) if has_s0 else None
    y_ref, sfin_ref = refs[:2]
    del refs[:2]
    n_chains = 2 * n_heads
    st_refs, b_refs, k_refs, o_refs = (refs[i * n_chains:(i + 1) * n_chains] for i in range(4))
    fast_ref = refs[4 * n_chains]
    chains = [(hh, direction) for hh in range(n_heads) for direction in range(2)]
    z_refs = (zf_ref, zb_ref)
    lanes = lambda hh: slice(hh * HG_DK, (hh + 1) * HG_DK)

    for ci, (hh, direction) in enumerate(chains):
        if has_s0:
            st_refs[ci][...] = s0_ref[0, direction, hh].T
        else:
            st_refs[ci][...] = jnp.zeros((HG_DV, HG_DK), F32)

    def lower_bound(hh, direction):
        lg = lg_ref[:, direction, hh, 0, :]
        ex = jnp.exp(lg - lg.max(axis=0, keepdims=True))
        return ex[1:layer + 1].sum(axis=0, keepdims=True) / ex.sum(axis=0, keepdims=True)

    lbs = [lower_bound(hh, direction) for hh, direction in chains]

    def chunk_rows(c):
        return pl.ds(pl.multiple_of(c * SCAN_CHUNK, SCAN_CHUNK), SCAN_CHUNK)

    def gates(c_fwd, c_bwd):
        rows = (chunk_rows(c_fwd), chunk_rows(c_bwd))
        return [_scan_gates(z_refs[direction][rows[direction], lanes(hh)], lb, tri_ref[direction])
                for (hh, direction), lb in zip(chains, lbs)]

    def store_gates(gs):
        lo = None
        for (k, b, lo_c), k_ref, b_ref in zip(gs, k_refs, b_refs):
            k_ref[...] = k
            b_ref[...] = b
            lo = lo_c if lo is None else jnp.minimum(lo, lo_c)
        fast_ref[0] = (lo >= -FAST_LOG2_LIMIT).astype(jnp.int32)

    store_gates(gates(0, n_chunks - 1))

    def trip(c, fast):
        rows = (chunk_rows(c), chunk_rows(n_chunks - 1 - c))
        nxt = gates(jnp.minimum(c + 1, n_chunks - 1), jnp.maximum(n_chunks - 2 - c, 0))
        outs = _scan_chunks(
            [(direction == 1, q_ref[rows[direction], lanes(hh)], v_ref[rows[direction], lanes(hh)],
              k_refs[ci][...], b_refs[ci][...], st_refs[ci][...], b_refs[ci])
             for ci, (hh, direction) in enumerate(chains)],
            ml_ref, mr_ref, fast)
        for ci, ((hh, direction), (o, st)) in enumerate(zip(chains, outs)):
            st_refs[ci][...] = st
            o_refs[ci][rows[direction], :] = o
        store_gates(nxt)

    def scan_body(c, carry):
        use_fast = fast_ref[0] == 1
        pl.when(use_fast)(lambda: trip(c, True))
        pl.when(jnp.logical_not(use_fast))(lambda: trip(c, False))
        return carry

    lax.fori_loop(0, n_chunks, scan_body, 0)

    def gate_body(c, carry):
        rows = chunk_rows(c)
        for hh in range(n_heads):
            o = o_refs[2 * hh][rows, :] + o_refs[2 * hh + 1][rows, :]
            o = o * lax.rsqrt(jnp.mean(o * o, axis=-1, keepdims=True) + EPS) * ng_ref[...]
            g = g_ref[rows, lanes(hh)]
            y_ref[rows, lanes(hh)] = (o * (g * _sigmoid(g))).astype(y_ref.dtype)
        return carry

    lax.fori_loop(0, n_chunks, gate_body, 0)

    for ci, (hh, direction) in enumerate(chains):
        sfin_ref[0, direction, hh] = st_refs[ci][...].T


def _hgrn_bidir(layer, seqs, u, logits, s0, norm_g):
    q, zf, zb, v, g = u
    n = q.shape[0]
    t = n // seqs
    n_chunks = t // SCAN_CHUNK
    n_heads = max(1, min(SCAN_HEADS_MAX, SCAN_WINDOW_BYTES // (5 * 2 * t * HG_DK * 4)))
    n_chains = 2 * n_heads
    ml, mr = _level_masks()
    tri = _tri_tables()
    tok = lambda: pl.BlockSpec((t, n_heads * HG_DK), lambda s, h: (s, h))
    state = lambda: pl.BlockSpec((1, 2, n_heads, HG_DK, HG_DV), lambda s, h: (s, 0, h, 0, 0))
    in_specs = [tok(), tok(), tok(), tok(), tok(),
                pl.BlockSpec((DEPTH, 2, n_heads, 1, HG_DK), lambda s, h: (0, 0, h, 0, 0)),
                _resident((1, HG_DV)), _resident(ml.shape), _resident(mr.shape), _resident(tri.shape)]
    args = [q, zf, zb, v, g, logits.reshape(DEPTH, 2, HG_HEADS, 1, HG_DK), norm_g.reshape(1, HG_DV),
            jnp.asarray(ml), jnp.asarray(mr), jnp.asarray(tri).astype(BF16)]
    if s0 is not None:
        in_specs.append(state())
        args.append(s0)
    return pl.pallas_call(
        functools.partial(_hgrn_kernel, layer, n_chunks, n_heads, s0 is not None),
        grid=(seqs, HG_HEADS // n_heads),
        in_specs=in_specs,
        out_specs=[tok(), state()],
        out_shape=[jax.ShapeDtypeStruct((n, HG_KW), BF16),
                   jax.ShapeDtypeStruct((seqs, 2, HG_HEADS, HG_DK, HG_DV), F32)],
        scratch_shapes=([pltpu.VMEM((HG_DV, HG_DK), F32)] * n_chains
                        + [pltpu.VMEM((SCAN_CHUNK, HG_DK), F32)] * n_chains
                        + [pltpu.VMEM((SCAN_CHUNK, HG_DK), F32)] * n_chains
                        + [pltpu.VMEM((t, HG_DV), F32)] * n_chains
                        + [pltpu.SMEM((1,), jnp.int32)]),
        compiler_params=_params(2),
        name="hgrn2_bidir",
    )(*args)


def kernel(x_prompt, x_sample, cache_k, cache_v, state_hgrn, c, c_ctx, ada_w, ada_b, norm_g, ffn_w1, ffn_w2,
           mix0_w_in, mix0_w_out, na_rpb, mix1_w_in, mix1_w_out, hg_lb_logits, hg_norm_g, norm_f):
    bp, sp, d = x_prompt.shape
    bs, ts, _ = x_sample.shape
    xp = x_prompt.reshape(bp * sp, d)
    xs = x_sample.reshape(bs * ts, d)

    cond = jnp.zeros((COND_ROWS, d), F32).at[:bs].set(c).at[CTX_ROW].set(c_ctx)
    mod = _ada_mod(cond, ada_w, ada_b)

    w1 = ffn_w1.astype(BF16)
    w2 = ffn_w2.astype(BF16)

    new_k, new_v, new_s = [], [], []
    for l in range(DEPTH):
        last = l == DEPTH - 1

        def ffn(x, seq_len, j, half, final, mixed=None):
            return _half_ffn(x, mod, l, seq_len, j, half, norm_g[l, j], w1, w2, norm_f, final, mixed)

        xp = ffn(xp, None, 0, 0, False)
        xs = ffn(xs, ts, 0, 0, False)

        if l % 2 == 0:
            e = l // 2
            w_in = mix0_w_in[e].astype(BF16)
            w_out = mix0_w_out[e].astype(BF16)
            qk_scale = float(NA_HD) ** -0.5
            qp, kp, vp, fp = _in_proj(xp, mod, l, None, norm_g[l, 1], w_in,
                                      [(NA_W, BF16, qk_scale), (NA_W, F32, 1.0), (NA_W, F32, 1.0), (FN_W, BF16, 1.0)])
            qs, ks, vs, fs = _in_proj(xs, mod, l, ts, norm_g[l, 1], w_in,
                                      [(NA_W, BF16, qk_scale), (NA_W, BF16, 1.0), (NA_W, BF16, 1.0), (FN_W, BF16, 1.0)])
            new_k.append(kp.reshape(bp, sp, NA_HEADS, NA_HD))
            new_v.append(vp.reshape(bp, sp, NA_HEADS, NA_HD))

            r3 = lambda a, b_, t_: a.reshape(b_, t_, a.shape[-1])
            ap = _context_attention(r3(qp, bp, sp), r3(kp, bp, sp), r3(vp, bp, sp))
            ck = cache_k[:, e].reshape(bs, -1, NA_W).astype(BF16)
            cv = cache_v[:, e].reshape(bs, -1, NA_W).astype(BF16)
            a_s = _neighbourhood_attention(r3(qs, bs, ts), r3(ks, bs, ts), r3(vs, bs, ts), ck, cv,
                                           _na_bias_table(na_rpb[e]))
            cgsg = None
            fn = []
            for f3 in (r3(fp, bp, sp), r3(fs, bs, ts)):
                dft_np, cgsg_np = _dft_tables(f3.shape[1])
                fn.append(_fourier_mix(f3, jnp.asarray(dft_np).astype(BF16), jnp.asarray(cgsg_np).astype(BF16)))
            mixed_p = ([ap.reshape(bp * sp, NA_W), fn[0].reshape(bp * sp, FN_W)], w_out)
            mixed_s = ([a_s.reshape(bs * ts, NA_W), fn[1].reshape(bs * ts, FN_W)], w_out)
        else:
            o = l // 2
            w_in = mix1_w_in[o].astype(BF16)
            w_out = mix1_w_out[o].astype(BF16)
            outs = [(HG_KW, F32, 1.0)] * 5
            up = _in_proj(xp, mod, l, None, norm_g[l, 1], w_in, outs)
            us = _in_proj(xs, mod, l, ts, norm_g[l, 1], w_in, outs)
            yp, s_fin = _hgrn_bidir(l, bp, up, hg_lb_logits, None, hg_norm_g[o])
            ys, _ = _hgrn_bidir(l, bs, us, hg_lb_logits, state_hgrn[:, o], hg_norm_g[o])
            new_s.append(s_fin)
            mixed_p, mixed_s = ([yp], w_out), ([ys], w_out)

        xp = ffn(xp, None, 2, 1, last, mixed_p)
        xs = ffn(xs, ts, 2, 1, last, mixed_s)

    stack = lambda parts: jnp.expand_dims(parts[0], 1) if len(parts) == 1 else jnp.stack(parts, axis=1)
    return (xp.reshape(bp, sp, d), xs.reshape(bs, ts, d), stack(new_k), stack(new_v), stack(new_s))
```

```python
import functools

import numpy as np
import jax
import jax.numpy as jnp
from jax import lax
from jax.experimental import pallas as pl
from jax.experimental.pallas import tpu as pltpu

F32 = jnp.float32
BF16 = jnp.bfloat16

D_MODEL = 1024
DEPTH = 2
GRID_W = 64
NA_HEADS = 8
NA_HD = 64
NA_W = NA_HEADS * NA_HD
NA_WIN_R = 8
NA_WIN_C = 16
CTX_SEQS_PER_STEP = 2
NA_ROWS_PER_STEP = 4
FN_GROUPS = 4
FN_GW = 128
FN_W = FN_GROUPS * FN_GW
HG_HEADS = 8
HG_DK = 128
HG_DV = 128
HG_KW = HG_HEADS * HG_DK
D_FF = 2816
N_MOD = 9
EPS = 1e-6

COND_ROWS = 8
CTX_ROW = 4
TOKEN_TILE = 512
PROJ_TILE = 512
SCAN_CHUNK = 128
SUBLANES = 8
SMALL_LEVELS = (4, 2, 1)
SCAN_HEADS_MAX = 4
SCAN_WINDOW_BYTES = 24 * 1024 * 1024
FAST_BLOCK = 32
FAST_LOG2_LIMIT = 126.0 / (FAST_BLOCK // 2)
MASK_VALUE = -1e30
NEG_BIG = -1e30
VMEM_LIMIT = 56 * 1024 * 1024


def _params(n_axes, vmem=VMEM_LIMIT):
    return pltpu.CompilerParams(dimension_semantics=("arbitrary",) * n_axes, vmem_limit_bytes=vmem)


def _resident(shape):
    nd = len(shape)
    return pl.BlockSpec(shape, lambda *_: (0,) * nd, pipeline_mode=pl.Buffered(1))


def _sigmoid(a):
    return 1.0 / (1.0 + jnp.exp(-a))


def _rms_mod(x, g, shift, scale):
    y = x * lax.rsqrt(jnp.mean(x * x, axis=-1, keepdims=True) + EPS)
    return (y * g) * (1.0 + scale) + shift


def _split2(x):
    hi = x.astype(BF16)
    lo = (x - hi.astype(F32)).astype(BF16)
    return hi, lo


def _split3(x):
    hi = x.astype(BF16)
    r = x - hi.astype(F32)
    mid = r.astype(BF16)
    lo = (r - mid.astype(F32)).astype(BF16)
    return hi, mid, lo


def _dot(a, b):
    return jnp.dot(a, b, preferred_element_type=F32)


def _dot_nt(a, b):
    return lax.dot_general(a, b, (((1,), (1,)), ((), ())), preferred_element_type=F32)


def _ada_kernel(cond_ref, w_ref, b_ref, o_ref):
    c = cond_ref[...]
    s = c * _sigmoid(c)
    s_hi, s_lo = _split2(s)
    w_hi, w_lo = _split2(w_ref[0])
    o_ref[0] = _dot(s_hi, w_hi) + _dot(s_hi, w_lo) + _dot(s_lo, w_hi) + b_ref[0]


def _ada_mod(cond, ada_w, ada_b):
    depth, d, n = ada_w.shape
    tn = 1152
    out = pl.pallas_call(
        _ada_kernel,
        grid=(depth, n // tn),
        in_specs=[
            pl.BlockSpec((COND_ROWS, d), lambda l, j: (0, 0)),
            pl.BlockSpec((1, d, tn), lambda l, j: (l, 0, j)),
            pl.BlockSpec((1, 1, tn), lambda l, j: (l, 0, j)),
        ],
        out_specs=pl.BlockSpec((1, COND_ROWS, tn), lambda l, j: (l, 0, j)),
        out_shape=jax.ShapeDtypeStruct((depth, COND_ROWS, n), F32),
        compiler_params=_params(2),
        name="ada_mod",
    )(cond, ada_w, ada_b.reshape(depth, 1, n))
    return out.reshape(depth, COND_ROWS, N_MOD, d)


def _mod_spec(layer, seq_len, tm):
    if seq_len is None:
        return pl.BlockSpec((1, 1, N_MOD, D_MODEL), lambda i: (layer, CTX_ROW, 0, 0))
    tiles_per_seq = seq_len // tm
    return pl.BlockSpec((1, 1, N_MOD, D_MODEL), lambda i: (layer, i // tiles_per_seq, 0, 0))


def _ffn_kernel(j, final_norm, n_acts, x_ref, mod_ref, g_ref, w1a_ref, w1b_ref, w2_ref, gf_ref, *refs):
    a_refs, w_refs, o_ref = refs[:n_acts], refs[n_acts:2 * n_acts], refs[2 * n_acts]
    x = x_ref[...]
    m = mod_ref[0, 0]
    if n_acts:
        y = None
        for a_ref, w_ref in zip(a_refs, w_refs):
            t = _dot(a_ref[...], w_ref[...])
            y = t if y is None else y + t
        x = x + m[5:6] * y
    h = _rms_mod(x, g_ref[...], m[3 * j:3 * j + 1], m[3 * j + 1:3 * j + 2]).astype(BF16)
    a = _dot(h, w1a_ref[...])
    b = _dot(h, w1b_ref[...])
    act = (a * _sigmoid(a) * b).astype(BF16)
    y = x + (0.5 * m[3 * j + 2:3 * j + 3]) * _dot(act, w2_ref[...])
    if final_norm:
        y = y * lax.rsqrt(jnp.mean(y * y, axis=-1, keepdims=True) + EPS) * gf_ref[...]
    o_ref[...] = y


def _half_ffn(x, mod, layer, seq_len, j, half, g, w1, w2, gf, final_norm, mixed=None):
    n, d = x.shape
    tm = TOKEN_TILE
    once = pl.Buffered(1)
    acts, w_out = mixed if mixed is not None else ((), None)
    width = acts[0].shape[1] if acts else 0
    mixed_specs = ([pl.BlockSpec((tm, width), lambda i: (i, 0)) for _ in acts]
                   + [pl.BlockSpec((width, d), functools.partial(lambda i, blk: (blk, 0), blk=blk),
                                   pipeline_mode=once) for blk in range(len(acts))])
    return pl.pallas_call(
        functools.partial(_ffn_kernel, j, final_norm, len(acts)),
        grid=(n // tm,),
        in_specs=[
            pl.BlockSpec((tm, d), lambda i: (i, 0)),
            _mod_spec(layer, seq_len, tm),
            _resident((1, d)),
            pl.BlockSpec((None, None, d, D_FF), lambda i: (layer, half, 0, 0), pipeline_mode=once),
            pl.BlockSpec((None, None, d, D_FF), lambda i: (layer, half, 0, 1), pipeline_mode=once),
            pl.BlockSpec((None, None, D_FF, d), lambda i: (layer, half, 0, 0), pipeline_mode=once),
            _resident((1, d)),
        ] + mixed_specs,
        out_specs=pl.BlockSpec((tm, d), lambda i: (i, 0)),
        out_shape=jax.ShapeDtypeStruct((n, d), F32),
        compiler_params=_params(1),
        name="half_ffn",
    )(x, mod, g.reshape(1, d), w1, w1, w2, gf.reshape(1, d), *acts, *([w_out] * len(acts)))


def _proj_kernel(j, splits, x_ref, mod_ref, g_ref, w_ref, *o_refs):
    m = mod_ref[0, 0]
    h = _rms_mod(x_ref[...], g_ref[...], m[3 * j:3 * j + 1], m[3 * j + 1:3 * j + 2]).astype(BF16)
    u = _dot(h, w_ref[...])
    for (lo, hi, scale), o_ref in zip(splits, o_refs):
        piece = u[:, lo:hi]
        if scale != 1.0:
            piece = piece * scale
        o_ref[...] = piece.astype(o_ref.dtype)


def _in_proj(x, mod, layer, seq_len, g, w, outs):
    n, d = x.shape
    tm = PROJ_TILE
    splits, lo = [], 0
    for width, _, scale in outs:
        splits.append((lo, lo + width, scale))
        lo += width
    return pl.pallas_call(
        functools.partial(_proj_kernel, 1, tuple(splits)),
        grid=(n // tm,),
        in_specs=[
            pl.BlockSpec((tm, d), lambda i: (i, 0)),
            _mod_spec(layer, seq_len, tm),
            _resident((1, d)),
            _resident(w.shape),
        ],
        out_specs=[pl.BlockSpec((tm, width), lambda i: (i, 0)) for width, _, _ in outs],
        out_shape=[jax.ShapeDtypeStruct((n, width), dt) for width, dt, _ in outs],
        compiler_params=_params(1),
        name="mixer_in_proj",
    )(x, mod, g.reshape(1, d), w)


def _head_pair_attention(pairs):
    lane = lax.broadcasted_iota(jnp.int32, (1, 2 * NA_HD), 1)
    first = lane < NA_HD
    scores = []
    for q, key_blocks, _, bias_blocks in pairs:
        zero = jnp.zeros_like(q)
        qm = jnp.concatenate([jnp.where(first, q, zero), jnp.where(first, zero, q)], axis=0)
        ss = []
        for kb, bias in zip(key_blocks, bias_blocks):
            s = _dot_nt(qm, kb)
            ss.append(s if bias is None else s + bias)
        scores.append(ss)
    maxes = []
    for ss in scores:
        mx = ss[0].max(axis=-1, keepdims=True)
        for s in ss[1:]:
            mx = jnp.maximum(mx, s.max(axis=-1, keepdims=True))
        maxes.append(mx)
    outs = []
    for (q, _, value_blocks, _), ss, mx in zip(pairs, scores, maxes):
        den, acc = None, None
        for s, vb in zip(ss, value_blocks):
            p = jnp.exp(s - mx)
            dsum = p.sum(axis=-1, keepdims=True)
            pv = _dot(p.astype(BF16), vb)
            den = dsum if den is None else den + dsum
            acc = pv if acc is None else acc + pv
        o = acc / den
        tq = q.shape[0]
        outs.append(jnp.where(first, o[:tq], o[tq:]))
    return outs


def _ctx_attn_kernel(q_ref, k_ref, v_ref, o_ref):
    dests = [(i, slice(pr * 2 * NA_HD, (pr + 1) * 2 * NA_HD))
             for i in range(CTX_SEQS_PER_STEP) for pr in range(NA_HEADS // 2)]
    outs = _head_pair_attention([(q_ref[i, :, sl], [k_ref[i, :, sl].astype(BF16)],
                                  [v_ref[i, :, sl].astype(BF16)], [None]) for i, sl in dests])
    for (i, sl), o in zip(dests, outs):
        o_ref[i, :, sl] = o.astype(o_ref.dtype)


def _context_attention(q, k, v):
    b, s, w = q.shape
    blk = lambda: pl.BlockSpec((CTX_SEQS_PER_STEP, s, w), lambda i: (i, 0, 0))
    return pl.pallas_call(
        _ctx_attn_kernel,
        grid=(b // CTX_SEQS_PER_STEP,),
        in_specs=[blk(), blk(), blk()],
        out_specs=blk(),
        out_shape=jax.ShapeDtypeStruct((b, s, w), BF16),
        compiler_params=_params(1),
        name="context_attention",
    )(q, k, v)


def _na_kernel(rows, q_ref, k_ref, v_ref, ck_ref, cv_ref, bias_ref, o_ref):
    slices = [slice(pr * 2 * NA_HD, (pr + 1) * 2 * NA_HD) for pr in range(NA_HEADS // 2)]
    pairs, dests = [], []
    for rr in range(NA_ROWS_PER_STEP):
        r = pl.program_id(1) * NA_ROWS_PER_STEP + rr
        r0 = jnp.clip(r - NA_WIN_R // 2, 0, rows - NA_WIN_R)
        start = pl.multiple_of(r0 * GRID_W, GRID_W)
        win = pl.ds(start, NA_WIN_R * GRID_W)
        first = (NA_WIN_R - 1) - (r - r0)
        q_rows = slice(rr * GRID_W, (rr + 1) * GRID_W)
        for pr, sl in enumerate(slices):
            bias = jnp.concatenate(
                [jnp.concatenate([bias_ref[2 * pr + hh, first + 2 * jj] for jj in range(NA_WIN_R // 2)], axis=1)
                 for hh in range(2)], axis=0)
            pairs.append((q_ref[0, q_rows, sl], [k_ref[0, win, sl], ck_ref[0, :, sl]],
                          [v_ref[0, win, sl], cv_ref[0, :, sl]], [bias, None]))
            dests.append((q_rows, sl))
    for (q_rows, sl), o in zip(dests, _head_pair_attention(pairs)):
        o_ref[0, q_rows, sl] = o.astype(o_ref.dtype)


def _na_bias_table(rpb):
    cols = np.arange(GRID_W)
    col_start = np.clip(cols - NA_WIN_C // 2, 0, GRID_W - NA_WIN_C)
    inside = (cols[None, :] >= col_start[:, None]) & (cols[None, :] < col_start[:, None] + NA_WIN_C)
    pad = GRID_W - NA_WIN_C
    padded = jnp.pad(rpb.astype(F32), ((0, 0), (0, 0), (pad, pad)))
    t = jnp.stack([padded[:, :, GRID_W - 1 - w:2 * GRID_W - 1 - w] for w in range(GRID_W)], axis=2)
    t = jnp.where(inside[None, None], t, MASK_VALUE)
    return jnp.concatenate([t[:, :-1], t[:, 1:]], axis=-1)


def _neighbourhood_attention(q, k, v, ck, cv, bias):
    b, t, w = q.shape
    rows = t // GRID_W
    p = ck.shape[1]
    seq = lambda n: pl.BlockSpec((1, n, w), lambda i, r: (i, 0, 0))
    row = lambda: pl.BlockSpec((1, NA_ROWS_PER_STEP * GRID_W, w), lambda i, r: (i, r, 0))
    return pl.pallas_call(
        functools.partial(_na_kernel, rows),
        grid=(b, rows // NA_ROWS_PER_STEP),
        in_specs=[row(), seq(t), seq(t), seq(p), seq(p), _resident(bias.shape)],
        out_specs=row(),
        out_shape=jax.ShapeDtypeStruct((b, t, w), BF16),
        compiler_params=_params(2),
        name="neighbourhood_attention",
    )(q, k, v, ck, cv, bias)


def _dft_tables(t):
    def cs(n):
        idx = (np.arange(n)[:, None] * np.arange(n)[None, :]) % n
        ang = 2.0 * np.pi * idx.astype(np.float64) / n
        return np.cos(ang), np.sin(ang)
    ct, st = cs(t)
    cg, sg = cs(FN_GW)
    return (np.concatenate([ct, -st], axis=1).astype(np.float32),
            np.concatenate([cg, sg], axis=1).astype(np.float32))


def _fnet_kernel(t, scale, f_ref, cgsg_ref, dft_ref, o_ref, xcs_ref):
    @pl.when(pl.program_id(1) == 0)
    def _():
        for g in range(FN_GROUPS):
            sl = slice(g * FN_GW, (g + 1) * FN_GW)
            xcs = _dot(f_ref[0, :, sl], cgsg_ref[...])
            xcs_ref[0:t, sl] = xcs[:, :FN_GW].astype(BF16)
            xcs_ref[t:2 * t, sl] = xcs[:, FN_GW:].astype(BF16)

    o_ref[0] = (_dot(dft_ref[...], xcs_ref[...]) * scale).astype(o_ref.dtype)


def _fourier_mix(f, dft, cgsg):
    b, t, w = f.shape
    tq = min(t, TOKEN_TILE)
    scale = float(1.0 / np.sqrt(float(t * FN_GW)))
    return pl.pallas_call(
        functools.partial(_fnet_kernel, t, scale),
        grid=(b, t // tq),
        in_specs=[
            pl.BlockSpec((1, t, w), lambda i, r: (i, 0, 0)),
            pl.BlockSpec(cgsg.shape, lambda i, r: (0, 0)),
            pl.BlockSpec((tq, 2 * t), lambda i, r: (r, 0)),
        ],
        out_specs=pl.BlockSpec((1, tq, w), lambda i, r: (i, r, 0)),
        out_shape=jax.ShapeDtypeStruct((b, t, w), BF16),
        scratch_shapes=[pltpu.VMEM((2 * t, w), BF16)],
        compiler_params=_params(2),
        name="fourier_mix",
    )(f, cgsg, dft)


def _ref_rows(b, b_ref, m, reverse):
    c = SCAN_CHUNK
    blk = 2 * m
    anchor = m if reverse else m - 1
    if blk >= 8:
        pieces = [jnp.broadcast_to(b_ref[p * blk + anchor:p * blk + anchor + 1, :], (blk, HG_DK))
                  for p in range(c // blk)]
        return pieces[0] if len(pieces) == 1 else jnp.concatenate(pieces, axis=0)
    pos = lax.broadcasted_iota(jnp.int32, (c, HG_DK), 0) % blk
    r = b
    for p in range(blk):
        delta = anchor - p
        if delta != 0:
            r = jnp.where(pos == p, pltpu.roll(b, (-delta) % c, 0), r)
    return r


def _level_masks():
    c = SCAN_CHUNK
    t = np.arange(c)
    left, right = [], []
    for m in SMALL_LEVELS:
        is_right = (t // m) % 2 == 1
        left.append(np.where(is_right, NEG_BIG, 0.0))
        right.append(np.where(is_right, 0.0, NEG_BIG))
    full = lambda rows: np.ascontiguousarray(
        np.broadcast_to(np.stack(rows)[:, :, None], (len(rows), c, HG_DK))).astype(np.float32)
    return full(left), full(right)


def _tri_tables():
    t = np.arange(SCAN_CHUNK)
    return np.stack([t[None, :] <= t[:, None], t[None, :] >= t[:, None]]).astype(np.float32)


def _scan_gates(z, lb, tri):
    e = jnp.exp(-jnp.abs(z))
    inv = 1.0 / (1.0 + e)
    pos_z = z >= 0
    sig = jnp.where(pos_z, inv, e * inv)
    sig_neg = jnp.where(pos_z, e * inv, inv)
    k = (1.0 - lb) * sig_neg
    log2_f = jnp.log2(lb + (1.0 - lb) * sig)
    b = None
    for part in _split3(log2_f):
        t = _dot(tri, part)
        b = t if b is None else b + t
    return k, b, jnp.min(log2_f)


def _scan_chunks(chains, ml_ref, mr_ref, fast):
    c = SCAN_CHUNK
    n_tiles = c // SUBLANES
    tile = lambda x, i: x[i * SUBLANES:(i + 1) * SUBLANES]
    row = lax.broadcasted_iota(jnp.int32, (c, c), 0)
    col = lax.broadcasted_iota(jnp.int32, (c, c), 1)
    xr = row ^ col
    zero_tile = jnp.zeros((SUBLANES, HG_DK), F32)

    os_, b_tots = [], []
    for reverse, q, v, k, b, st, b_ref in chains:
        edge = 0 if reverse else c - 1
        b_tots.append(b_ref[edge:edge + 1, :])
        os_.append(_dot_nt((q * jnp.exp2(b)).astype(BF16), st.astype(BF16)))

    a_tiles = [[None] * n_tiles for _ in chains]
    m = c // 2
    while m >= (FAST_BLOCK if fast else SUBLANES):
        g = m // SUBLANES
        for ci, (reverse, q, v, k, b, st, b_ref) in enumerate(chains):
            q_ids, qe_tiles, ke_tiles = [], [], [zero_tile] * n_tiles
            for p in range(n_tiles // (2 * g)):
                left = range(2 * g * p, 2 * g * p + g)
                right = range(2 * g * p + g, 2 * g * (p + 1))
                anchor = (2 * g * p + g) * SUBLANES - (0 if reverse else 1)
                r = jnp.broadcast_to(b_ref[anchor:anchor + 1, :], (SUBLANES, HG_DK))
                q_side, k_side = (left, right) if reverse else (right, left)
                for i in q_side:
                    q_ids.append(i)
                    qe_tiles.append(tile(q, i) * jnp.exp2(tile(b, i) - r))
                for i in k_side:
                    ke_tiles[i] = tile(k, i) * jnp.exp2(r - tile(b, i))
            lvl = _dot_nt(jnp.concatenate(qe_tiles, axis=0).astype(BF16),
                          jnp.concatenate(ke_tiles, axis=0).astype(BF16))
            for j, i in enumerate(q_ids):
                old = a_tiles[ci][i]
                a_tiles[ci][i] = tile(lvl, j) if old is None else jnp.where(tile(xr, i) < 2 * m, tile(lvl, j), old)
        m //= 2
    a_s = [jnp.concatenate([t if t is not None else zero_tile for t in tiles], axis=0) for tiles in a_tiles]

    if fast:
        for ci, (reverse, q, v, k, b, st, b_ref) in enumerate(chains):
            mid = FAST_BLOCK // 2
            r = jnp.concatenate(
                [jnp.broadcast_to(b_ref[j * FAST_BLOCK + mid:j * FAST_BLOCK + mid + 1, :], (FAST_BLOCK, HG_DK))
                 for j in range(c // FAST_BLOCK)], axis=0)
            qe = (q * jnp.exp2(b - r)).astype(BF16)
            ke = (k * jnp.exp2(r - b)).astype(BF16)
            a_s[ci] = jnp.where(xr < FAST_BLOCK, _dot_nt(qe, ke), a_s[ci])
    else:
        for level, m in enumerate(SMALL_LEVELS):
            for ci, (reverse, q, v, k, b, st, b_ref) in enumerate(chains):
                d = b - _ref_rows(b, b_ref, m, reverse)
                mask_q, mask_k = (ml_ref, mr_ref) if reverse else (mr_ref, ml_ref)
                qe = (q * jnp.exp2(d + mask_q[level])).astype(BF16)
                ke = (k * jnp.exp2(mask_k[level] - d)).astype(BF16)
                a_s[ci] = jnp.where(xr < 2 * m, _dot_nt(qe, ke), a_s[ci])

    out = []
    for (reverse, q, v, k, b, st, b_ref), a, o, b_tot in zip(chains, a_s, os_, b_tots):
        if fast:
            a = jnp.where((col >= row) if reverse else (col <= row), a, 0.0)
        else:
            diag = jnp.where(xr == 0, _dot_nt(q.astype(BF16), k.astype(BF16)), 0.0)
            a = jnp.where((col > row) if reverse else (col < row), a, diag)
        o = o + _dot(a.astype(BF16), v.astype(BF16))
        k_end = (k * jnp.exp2(b_tot - b)).astype(BF16)
        st_new = st * jnp.exp2(b_tot) + _dot(v.T.astype(BF16), k_end)
        out.append((o, st_new))
    return out


def _hgrn_kernel(layer, n_chunks, n_heads, has_s0, *refs):
    refs = list(refs)
    q_ref, zf_ref, zb_ref, v_ref, g_ref, lg_ref, ng_ref, ml_ref, mr_ref, tri_ref = refs[:10]
    del refs[:10]
    s0_ref = refs.pop(0) if has_s0 else None
    y_ref, sfin_ref = refs[:2]
    del refs[:2]
    n_chains = 2 * n_heads
    st_refs, b_refs, k_refs, o_refs = (refs[i * n_chains:(i + 1) * n_chains] for i in range(4))
    fast_ref = refs[4 * n_chains]
    chains = [(hh, direction) for hh in range(n_heads) for direction in range(2)]
    z_refs = (zf_ref, zb_ref)
    lanes = lambda hh: slice(hh * HG_DK, (hh + 1) * HG_DK)

    for ci, (hh, direction) in enumerate(chains):
        if has_s0:
            st_refs[ci][...] = s0_ref[0, direction, hh].T
        else:
            st_refs[ci][...] = jnp.zeros((HG_DV, HG_DK), F32)

    def lower_bound(hh, direction):
        lg = lg_ref[:, direction, hh, 0, :]
        ex = jnp.exp(lg - lg.max(axis=0, keepdims=True))
        return ex[1:layer + 1].sum(axis=0, keepdims=True) / ex.sum(axis=0, keepdims=True)

    lbs = [lower_bound(hh, direction) for hh, direction in chains]

    def chunk_rows(c):
        return pl.ds(pl.multiple_of(c * SCAN_CHUNK, SCAN_CHUNK), SCAN_CHUNK)

    def gates(c_fwd, c_bwd):
        rows = (chunk_rows(c_fwd), chunk_rows(c_bwd))
        return [_scan_gates(z_refs[direction][rows[direction], lanes(hh)], lb, tri_ref[direction])
                for (hh, direction), lb in zip(chains, lbs)]

    def store_gates(gs):
        lo = None
        for (k, b, lo_c), k_ref, b_ref in zip(gs, k_refs, b_refs):
            k_ref[...] = k
            b_ref[...] = b
            lo = lo_c if lo is None else jnp.minimum(lo, lo_c)
        fast_ref[0] = (lo >= -FAST_LOG2_LIMIT).astype(jnp.int32)

    store_gates(gates(0, n_chunks - 1))

    def trip(c, fast, last):
        rows = (chunk_rows(c), chunk_rows(n_chunks - 1 - c))
        nxt = None if last else gates(c + 1, n_chunks - 2 - c)
        outs = _scan_chunks(
            [(direction == 1, q_ref[rows[direction], lanes(hh)], v_ref[rows[direction], lanes(hh)],
              k_refs[ci][...], b_refs[ci][...], st_refs[ci][...], b_refs[ci])
             for ci, (hh, direction) in enumerate(chains)],
            ml_ref, mr_ref, fast)
        for ci, ((hh, direction), (o, st)) in enumerate(zip(chains, outs)):
            st_refs[ci][...] = st
            o_refs[ci][rows[direction], :] = o
        if not last:
            store_gates(nxt)

    def guarded_trip(c, last):
        use_fast = fast_ref[0] == 1
        pl.when(use_fast)(lambda: trip(c, True, last))
        pl.when(jnp.logical_not(use_fast))(lambda: trip(c, False, last))

    def scan_body(c, carry):
        guarded_trip(c, False)
        return carry

    lax.fori_loop(0, n_chunks - 1, scan_body, 0)
    guarded_trip(n_chunks - 1, True)

    def gate_body(c, carry):
        rows = chunk_rows(c)
        for hh in range(n_heads):
            o = o_refs[2 * hh][rows, :] + o_refs[2 * hh + 1][rows, :]
            o = o * lax.rsqrt(jnp.mean(o * o, axis=-1, keepdims=True) + EPS) * ng_ref[...]
            g = g_ref[rows, lanes(hh)]
            y_ref[rows, lanes(hh)] = (o * (g * _sigmoid(g))).astype(y_ref.dtype)
        return carry

    lax.fori_loop(0, n_chunks, gate_body, 0)

    for ci, (hh, direction) in enumerate(chains):
        sfin_ref[0, direction, hh] = st_refs[ci][...].T


def _hgrn_bidir(layer, seqs, u, logits, s0, norm_g):
    q, zf, zb, v, g = u
    n = q.shape[0]
    t = n // seqs
    n_chunks = t // SCAN_CHUNK
    n_heads = max(1, min(SCAN_HEADS_MAX, SCAN_WINDOW_BYTES // (5 * 2 * t * HG_DK * 4)))
    n_chains = 2 * n_heads
    ml, mr = _level_masks()
    tri = _tri_tables()
    tok = lambda: pl.BlockSpec((t, n_heads * HG_DK), lambda s, h: (s, h))
    state = lambda: pl.BlockSpec((1, 2, n_heads, HG_DK, HG_DV), lambda s, h: (s, 0, h, 0, 0))
    in_specs = [tok(), tok(), tok(), tok(), tok(),
                pl.BlockSpec((DEPTH, 2, n_heads, 1, HG_DK), lambda s, h: (0, 0, h, 0, 0)),
                _resident((1, HG_DV)), _resident(ml.shape), _resident(mr.shape), _resident(tri.shape)]
    args = [q, zf, zb, v, g, logits.reshape(DEPTH, 2, HG_HEADS, 1, HG_DK), norm_g.reshape(1, HG_DV),
            jnp.asarray(ml), jnp.asarray(mr), jnp.asarray(tri).astype(BF16)]
    if s0 is not None:
        in_specs.append(state())
        args.append(s0)
    return pl.pallas_call(
        functools.partial(_hgrn_kernel, layer, n_chunks, n_heads, s0 is not None),
        grid=(seqs, HG_HEADS // n_heads),
        in_specs=in_specs,
        out_specs=[tok(), state()],
        out_shape=[jax.ShapeDtypeStruct((n, HG_KW), BF16),
                   jax.ShapeDtypeStruct((seqs, 2, HG_HEADS, HG_DK, HG_DV), F32)],
        scratch_shapes=([pltpu.VMEM((HG_DV, HG_DK), F32)] * n_chains
                        + [pltpu.VMEM((SCAN_CHUNK, HG_DK), F32)] * n_chains
                        + [pltpu.VMEM((SCAN_CHUNK, HG_DK), F32)] * n_chains
                        + [pltpu.VMEM((t, HG_DV), F32)] * n_chains
                        + [pltpu.SMEM((1,), jnp.int32)]),
        compiler_params=_params(2),
        name="hgrn2_bidir",
    )(*args)


def kernel(x_prompt, x_sample, cache_k, cache_v, state_hgrn, c, c_ctx, ada_w, ada_b, norm_g, ffn_w1, ffn_w2,
           mix0_w_in, mix0_w_out, na_rpb, mix1_w_in, mix1_w_out, hg_lb_logits, hg_norm_g, norm_f):
    bp, sp, d = x_prompt.shape
    bs, ts, _ = x_sample.shape
    xp = x_prompt.reshape(bp * sp, d)
    xs = x_sample.reshape(bs * ts, d)

    cond = jnp.zeros((COND_ROWS, d), F32).at[:bs].set(c).at[CTX_ROW].set(c_ctx)
    mod = _ada_mod(cond, ada_w, ada_b)

    w1 = ffn_w1.astype(BF16)
    w2 = ffn_w2.astype(BF16)

    new_k, new_v, new_s = [], [], []
    for l in range(DEPTH):
        last = l == DEPTH - 1

        def ffn(x, seq_len, j, half, final, mixed=None):
            return _half_ffn(x, mod, l, seq_len, j, half, norm_g[l, j], w1, w2, norm_f, final, mixed)

        xp = ffn(xp, None, 0, 0, False)
        xs = ffn(xs, ts, 0, 0, False)

        if l % 2 == 0:
            e = l // 2
            w_in = mix0_w_in[e].astype(BF16)
            w_out = mix0_w_out[e].astype(BF16)
            qk_scale = float(NA_HD) ** -0.5
            qp, kp, vp, fp = _in_proj(xp, mod, l, None, norm_g[l, 1], w_in,
                                      [(NA_W, BF16, qk_scale), (NA_W, F32, 1.0), (NA_W, F32, 1.0), (FN_W, BF16, 1.0)])
            qs, ks, vs, fs = _in_proj(xs, mod, l, ts, norm_g[l, 1], w_in,
                                      [(NA_W, BF16, qk_scale), (NA_W, BF16, 1.0), (NA_W, BF16, 1.0), (FN_W, BF16, 1.0)])
            new_k.append(kp.reshape(bp, sp, NA_HEADS, NA_HD))
            new_v.append(vp.reshape(bp, sp, NA_HEADS, NA_HD))

            r3 = lambda a, b_, t_: a.reshape(b_, t_, a.shape[-1])
            ap = _context_attention(r3(qp, bp, sp), r3(kp, bp, sp), r3(vp, bp, sp))
            ck = cache_k[:, e].reshape(bs, -1, NA_W).astype(BF16)
            cv = cache_v[:, e].reshape(bs, -1, NA_W).astype(BF16)
            a_s = _neighbourhood_attention(r3(qs, bs, ts), r3(ks, bs, ts), r3(vs, bs, ts), ck, cv,
                                           _na_bias_table(na_rpb[e]))
            cgsg = None
            fn = []
            for f3 in (r3(fp, bp, sp), r3(fs, bs, ts)):
                dft_np, cgsg_np = _dft_tables(f3.shape[1])
                fn.append(_fourier_mix(f3, jnp.asarray(dft_np).astype(BF16), jnp.asarray(cgsg_np).astype(BF16)))
            mixed_p = ([ap.reshape(bp * sp, NA_W), fn[0].reshape(bp * sp, FN_W)], w_out)
            mixed_s = ([a_s.reshape(bs * ts, NA_W), fn[1].reshape(bs * ts, FN_W)], w_out)
        else:
            o = l // 2
            w_in = mix1_w_in[o].astype(BF16)
            w_out = mix1_w_out[o].astype(BF16)
            outs = [(HG_KW, F32, 1.0)] * 5
            up = _in_proj(xp, mod, l, None, norm_g[l, 1], w_in, outs)
            us = _in_proj(xs, mod, l, ts, norm_g[l, 1], w_in, outs)
            yp, s_fin = _hgrn_bidir(l, bp, up, hg_lb_logits, None, hg_norm_g[o])
            ys, _ = _hgrn_bidir(l, bs, us, hg_lb_logits, state_hgrn[:, o], hg_norm_g[o])
            new_s.append(s_fin)
            mixed_p, mixed_s = ([yp], w_out), ([ys], w_out)

        xp = ffn(xp, None, 2, 1, last, mixed_p)
        xs = ffn(xs, ts, 2, 1, last, mixed_s)

    stack = lambda parts: jnp.expand_dims(parts[0], 1) if len(parts) == 1 else jnp.stack(parts, axis=1)
    return (xp.reshape(bp, sp, d), xs.reshape(bs, ts, d), stack(new_k), stack(new_v), stack(new_s))
```

```python
import functools

import numpy as np
import jax
import jax.numpy as jnp
from jax import lax
from jax.experimental import pallas as pl
from jax.experimental.pallas import tpu as pltpu

F32 = jnp.float32
BF16 = jnp.bfloat16

D_MODEL = 1024
DEPTH = 2
GRID_W = 64
NA_HEADS = 8
NA_HD = 64
NA_W = NA_HEADS * NA_HD
NA_WIN_R = 8
NA_WIN_C = 16
CTX_SEQS_PER_STEP = 2
NA_ROWS_PER_STEP = 4
FN_GROUPS = 4
FN_GW = 128
FN_W = FN_GROUPS * FN_GW
HG_HEADS = 8
HG_DK = 128
HG_DV = 128
HG_KW = HG_HEADS * HG_DK
D_FF = 2816
N_MOD = 9
EPS = 1e-6

COND_ROWS = 8
CTX_ROW = 4
TOKEN_TILE = 512
PROJ_TILE = 512
PROJ_COLS = 256
SCAN_CHUNK = 128
SUBLANES = 8
SMALL_LEVELS = (4, 2, 1)
SCAN_HEADS_MAX = 4
SCAN_WINDOW_BYTES = 40 * 1024 * 1024
FAST_BLOCK = 32
FAST_LOG2_LIMIT = 125.0 / (FAST_BLOCK // 2)
MASK_VALUE = -1e30
NEG_BIG = -1e30
VMEM_LIMIT = 56 * 1024 * 1024


def _params(n_axes, vmem=VMEM_LIMIT):
    return pltpu.CompilerParams(dimension_semantics=("arbitrary",) * n_axes, vmem_limit_bytes=vmem)


def _resident(shape):
    nd = len(shape)
    return pl.BlockSpec(shape, lambda *_: (0,) * nd, pipeline_mode=pl.Buffered(1))


def _sigmoid(a):
    return 1.0 / (1.0 + jnp.exp(-a))


def _rms_mod(x, g, shift, scale):
    y = x * lax.rsqrt(jnp.mean(x * x, axis=-1, keepdims=True) + EPS)
    return (y * g) * (1.0 + scale) + shift


def _split2(x):
    hi = x.astype(BF16)
    lo = (x - hi.astype(F32)).astype(BF16)
    return hi, lo


def _split3(x):
    hi = x.astype(BF16)
    r = x - hi.astype(F32)
    mid = r.astype(BF16)
    lo = (r - mid.astype(F32)).astype(BF16)
    return hi, mid, lo


def _dot(a, b):
    return jnp.dot(a, b, preferred_element_type=F32)


def _dot_nt(a, b):
    return lax.dot_general(a, b, (((1,), (1,)), ((), ())), preferred_element_type=F32)


def _ada_kernel(cond_ref, w_ref, b_ref, o_ref):
    c = cond_ref[...]
    s = c * _sigmoid(c)
    s_hi, s_lo = _split2(s)
    w_hi, w_lo = _split2(w_ref[0])
    o_ref[0] = _dot(s_hi, w_hi) + _dot(s_hi, w_lo) + _dot(s_lo, w_hi) + b_ref[0]


def _ada_mod(cond, ada_w, ada_b):
    depth, d, n = ada_w.shape
    tn = 1152
    out = pl.pallas_call(
        _ada_kernel,
        grid=(depth, n // tn),
        in_specs=[
            pl.BlockSpec((COND_ROWS, d), lambda l, j: (0, 0)),
            pl.BlockSpec((1, d, tn), lambda l, j: (l, 0, j)),
            pl.BlockSpec((1, 1, tn), lambda l, j: (l, 0, j)),
        ],
        out_specs=pl.BlockSpec((1, COND_ROWS, tn), lambda l, j: (l, 0, j)),
        out_shape=jax.ShapeDtypeStruct((depth, COND_ROWS, n), F32),
        compiler_params=_params(2),
        name="ada_mod",
    )(cond, ada_w, ada_b.reshape(depth, 1, n))
    return out.reshape(depth, COND_ROWS, N_MOD, d)


def _mod_spec(layer, seq_len, tm):
    if seq_len is None:
        return pl.BlockSpec((1, 1, N_MOD, D_MODEL), lambda i: (layer, CTX_ROW, 0, 0))
    tiles_per_seq = seq_len // tm
    return pl.BlockSpec((1, 1, N_MOD, D_MODEL), lambda i: (layer, i // tiles_per_seq, 0, 0))


def _ffn_kernel(j, final_norm, n_acts, scan_mix, x_ref, mod_ref, g_ref, w1a_ref, w1b_ref, w2_ref, gf_ref, *refs):
    x = x_ref[...]
    m = mod_ref[0, 0]
    if scan_mix:
        of_ref, ob_ref, gate_ref, ng_ref, w_ref, o_ref = refs
        o = of_ref[...] + ob_ref[...]
        ng = ng_ref[...]
        heads = []
        for hh in range(HG_HEADS):
            oh = o[:, hh * HG_DV:(hh + 1) * HG_DV]
            heads.append(oh * lax.rsqrt(jnp.mean(oh * oh, axis=-1, keepdims=True) + EPS) * ng)
        gate = gate_ref[...]
        y = (jnp.concatenate(heads, axis=1) * (gate * _sigmoid(gate))).astype(BF16)
        x = x + m[5:6] * _dot(y, w_ref[...])
    else:
        a_refs, w_refs, o_ref = refs[:n_acts], refs[n_acts:2 * n_acts], refs[2 * n_acts]
        if n_acts:
            y = None
            for a_ref, w_ref in zip(a_refs, w_refs):
                t = _dot(a_ref[...], w_ref[...])
                y = t if y is None else y + t
            x = x + m[5:6] * y
    h = _rms_mod(x, g_ref[...], m[3 * j:3 * j + 1], m[3 * j + 1:3 * j + 2]).astype(BF16)
    a = _dot(h, w1a_ref[...])
    b = _dot(h, w1b_ref[...])
    act = (a * _sigmoid(a) * b).astype(BF16)
    y = x + (0.5 * m[3 * j + 2:3 * j + 3]) * _dot(act, w2_ref[...])
    if final_norm:
        y = y * lax.rsqrt(jnp.mean(y * y, axis=-1, keepdims=True) + EPS) * gf_ref[...]
    o_ref[...] = y


def _half_ffn(x, mod, layer, seq_len, j, half, g, w1, w2, gf, final_norm, mixed=None, scan=None):
    n, d = x.shape
    tm = TOKEN_TILE
    once = pl.Buffered(1)
    acts, w_out = mixed if mixed is not None else ((), None)
    width = acts[0].shape[1] if acts else 0
    mixed_specs = ([pl.BlockSpec((tm, width), lambda i: (i, 0)) for _ in acts]
                   + [pl.BlockSpec((width, d), functools.partial(lambda i, blk: (blk, 0), blk=blk),
                                   pipeline_mode=once) for blk in range(len(acts))])
    mixed_args = [*acts, *([w_out] * len(acts))]
    if scan is not None:
        o_fwd, o_bwd, gate, head_g, w_scan = scan
        mixed_specs = ([pl.BlockSpec((tm, d), lambda i: (i, 0)) for _ in range(3)]
                       + [_resident((1, HG_DV)), _resident(w_scan.shape)])
        mixed_args = [o_fwd, o_bwd, gate, head_g.reshape(1, HG_DV), w_scan]
    return pl.pallas_call(
        functools.partial(_ffn_kernel, j, final_norm, len(acts), scan is not None),
        grid=(n // tm,),
        in_specs=[
            pl.BlockSpec((tm, d), lambda i: (i, 0)),
            _mod_spec(layer, seq_len, tm),
            _resident((1, d)),
            pl.BlockSpec((None, None, d, D_FF), lambda i: (layer, half, 0, 0), pipeline_mode=once),
            pl.BlockSpec((None, None, d, D_FF), lambda i: (layer, half, 0, 1), pipeline_mode=once),
            pl.BlockSpec((None, None, D_FF, d), lambda i: (layer, half, 0, 0), pipeline_mode=once),
            _resident((1, d)),
        ] + mixed_specs,
        out_specs=pl.BlockSpec((tm, d), lambda i: (i, 0)),
        out_shape=jax.ShapeDtypeStruct((n, d), F32),
        compiler_params=_params(1),
        name="half_ffn",
    )(x, mod, g.reshape(1, d), w1, w1, w2, gf.reshape(1, d), *mixed_args)


def _proj_kernel(j, splits, x_ref, mod_ref, g_ref, w_ref, *o_refs):
    m = mod_ref[0, 0]
    h = _rms_mod(x_ref[...], g_ref[...], m[3 * j:3 * j + 1], m[3 * j + 1:3 * j + 2]).astype(BF16)
    u = _dot(h, w_ref[...])
    for (lo, hi, scale), o_ref in zip(splits, o_refs):
        piece = u[:, lo:hi]
        if scale != 1.0:
            piece = piece * scale
        o_ref[...] = piece.astype(o_ref.dtype)


def _in_proj(x, mod, layer, seq_len, g, w, outs):
    n, d = x.shape
    tm = PROJ_TILE
    splits, lo = [], 0
    for width, _, scale in outs:
        splits.append((lo, lo + width, scale))
        lo += width
    return pl.pallas_call(
        functools.partial(_proj_kernel, 1, tuple(splits)),
        grid=(n // tm,),
        in_specs=[
            pl.BlockSpec((tm, d), lambda i: (i, 0)),
            _mod_spec(layer, seq_len, tm),
            _resident((1, d)),
            _resident(w.shape),
        ],
        out_specs=[pl.BlockSpec((tm, width), lambda i: (i, 0)) for width, _, _ in outs],
        out_shape=[jax.ShapeDtypeStruct((n, width), dt) for width, dt, _ in outs],
        compiler_params=_params(1),
        name="mixer_in_proj",
    )(x, mod, g.reshape(1, d), w)


def _forget_gates(z, lb):
    e = jnp.exp(-jnp.abs(z))
    inv = 1.0 / (1.0 + e)
    pos_z = z >= 0
    sig = jnp.where(pos_z, inv, e * inv)
    sig_neg = jnp.where(pos_z, e * inv, inv)
    return (1.0 - lb) * sig_neg, jnp.log2(lb + (1.0 - lb) * sig)


def _scan_proj_kernel(layer, x_ref, mod_ref, g_ref, w_ref, lg_ref, q_ref, v_ref, gate_ref, *gate_refs):
    m = mod_ref[0, 0]
    h = _rms_mod(x_ref[...], g_ref[...], m[3:4], m[4:5]).astype(BF16)

    def lower_bound(direction, cols):
        lg = lg_ref[:, direction, cols]
        ex = jnp.exp(lg - lg.max(axis=0, keepdims=True))
        return ex[1:layer + 1].sum(axis=0, keepdims=True) / ex.sum(axis=0, keepdims=True)

    n_blocks = HG_KW // PROJ_COLS
    gate_blocks = [(1 + direction, c) for c in range(n_blocks) for direction in range(2)]
    plain = [(grp, c) for c in range(n_blocks) for grp in (0, 3, 4)]
    order = []
    for i, block in enumerate(gate_blocks):
        order.append(block)
        lo, hi = len(plain) * i // len(gate_blocks), len(plain) * (i + 1) // len(gate_blocks)
        order.extend(plain[lo:hi])
    for grp, c in order:
        cols = slice(c * PROJ_COLS, (c + 1) * PROJ_COLS)
        u = _dot(h, w_ref[:, grp * HG_KW + c * PROJ_COLS:grp * HG_KW + (c + 1) * PROJ_COLS])
        if grp in (1, 2):
            k, log2_f = _forget_gates(u, lower_bound(grp - 1, cols))
            k_ref, hi_ref, mid_ref, lo_ref = gate_refs[4 * (grp - 1):4 * grp]
            k_ref[:, cols] = k
            hi_ref[:, cols], mid_ref[:, cols], lo_ref[:, cols] = _split3(log2_f)
        else:
            out = {0: q_ref, 3: v_ref, 4: gate_ref}[grp]
            out[:, cols] = u.astype(out.dtype)


def _scan_in_proj(x, mod, layer, seq_len, g, w, logits):
    n, d = x.shape
    tm = PROJ_TILE
    dtypes = [F32, BF16, F32] + [F32, BF16, BF16, BF16] * 2
    return pl.pallas_call(
        functools.partial(_scan_proj_kernel, layer),
        grid=(n // tm,),
        in_specs=[
            pl.BlockSpec((tm, d), lambda i: (i, 0)),
            _mod_spec(layer, seq_len, tm),
            _resident((1, d)),
            _resident(w.shape),
            _resident(logits.shape),
        ],
        out_specs=[pl.BlockSpec((tm, HG_KW), lambda i: (i, 0)) for _ in dtypes],
        out_shape=[jax.ShapeDtypeStruct((n, HG_KW), dt) for dt in dtypes],
        compiler_params=_params(1),
        name="scan_in_proj",
    )(x, mod, g.reshape(1, d), w, logits)


def _head_pair_attention(pairs):
    lane = lax.broadcasted_iota(jnp.int32, (1, 2 * NA_HD), 1)
    first = lane < NA_HD
    scores = []
    for q, key_blocks, _, bias_blocks in pairs:
        zero = jnp.zeros_like(q)
        qm = jnp.concatenate([jnp.where(first, q, zero), jnp.where(first, zero, q)], axis=0)
        ss = []
        for kb, bias in zip(key_blocks, bias_blocks):
            s = _dot_nt(qm, kb)
            ss.append(s if bias is None else s + bias)
        scores.append(ss)
    maxes = []
    for ss in scores:
        mx = ss[0].max(axis=-1, keepdims=True)
        for s in ss[1:]:
            mx = jnp.maximum(mx, s.max(axis=-1, keepdims=True))
        maxes.append(mx)
    outs = []
    for (q, _, value_blocks, _), ss, mx in zip(pairs, scores, maxes):
        den, acc = None, None
        for s, vb in zip(ss, value_blocks):
            p = jnp.exp(s - mx)
            dsum = p.sum(axis=-1, keepdims=True)
            pv = _dot(p.astype(BF16), vb)
            den = dsum if den is None else den + dsum
            acc = pv if acc is None else acc + pv
        o = acc / den
        tq = q.shape[0]
        outs.append(jnp.where(first, o[:tq], o[tq:]))
    return outs


def _ctx_attn_kernel(q_ref, k_ref, v_ref, o_ref):
    dests = [(i, slice(pr * 2 * NA_HD, (pr + 1) * 2 * NA_HD))
             for i in range(CTX_SEQS_PER_STEP) for pr in range(NA_HEADS // 2)]
    outs = _head_pair_attention([(q_ref[i, :, sl], [k_ref[i, :, sl].astype(BF16)],
                                  [v_ref[i, :, sl].astype(BF16)], [None]) for i, sl in dests])
    for (i, sl), o in zip(dests, outs):
        o_ref[i, :, sl] = o.astype(o_ref.dtype)


def _context_attention(q, k, v):
    b, s, w = q.shape
    blk = lambda: pl.BlockSpec((CTX_SEQS_PER_STEP, s, w), lambda i: (i, 0, 0))
    return pl.pallas_call(
        _ctx_attn_kernel,
        grid=(b // CTX_SEQS_PER_STEP,),
        in_specs=[blk(), blk(), blk()],
        out_specs=blk(),
        out_shape=jax.ShapeDtypeStruct((b, s, w), BF16),
        compiler_params=_params(1),
        name="context_attention",
    )(q, k, v)


def _na_kernel(rows, q_ref, k_ref, v_ref, ck_ref, cv_ref, bias_ref, o_ref):
    slices = [slice(pr * 2 * NA_HD, (pr + 1) * 2 * NA_HD) for pr in range(NA_HEADS // 2)]
    pairs, dests = [], []
    for rr in range(NA_ROWS_PER_STEP):
        r = pl.program_id(1) * NA_ROWS_PER_STEP + rr
        r0 = jnp.clip(r - NA_WIN_R // 2, 0, rows - NA_WIN_R)
        start = pl.multiple_of(r0 * GRID_W, GRID_W)
        win = pl.ds(start, NA_WIN_R * GRID_W)
        first = (NA_WIN_R - 1) - (r - r0)
        q_rows = slice(rr * GRID_W, (rr + 1) * GRID_W)
        for pr, sl in enumerate(slices):
            bias = jnp.concatenate(
                [jnp.concatenate([bias_ref[2 * pr + hh, first + 2 * jj] for jj in range(NA_WIN_R // 2)], axis=1)
                 for hh in range(2)], axis=0)
            pairs.append((q_ref[0, q_rows, sl], [k_ref[0, win, sl], ck_ref[0, :, sl]],
                          [v_ref[0, win, sl], cv_ref[0, :, sl]], [bias, None]))
            dests.append((q_rows, sl))
    for (q_rows, sl), o in zip(dests, _head_pair_attention(pairs)):
        o_ref[0, q_rows, sl] = o.astype(o_ref.dtype)


def _na_bias_table(rpb):
    cols = np.arange(GRID_W)
    col_start = np.clip(cols - NA_WIN_C // 2, 0, GRID_W - NA_WIN_C)
    inside = (cols[None, :] >= col_start[:, None]) & (cols[None, :] < col_start[:, None] + NA_WIN_C)
    pad = GRID_W - NA_WIN_C
    padded = jnp.pad(rpb.astype(F32), ((0, 0), (0, 0), (pad, pad)))
    t = jnp.stack([padded[:, :, GRID_W - 1 - w:2 * GRID_W - 1 - w] for w in range(GRID_W)], axis=2)
    t = jnp.where(inside[None, None], t, MASK_VALUE)
    return jnp.concatenate([t[:, :-1], t[:, 1:]], axis=-1)


def _neighbourhood_attention(q, k, v, ck, cv, bias):
    b, t, w = q.shape
    rows = t // GRID_W
    p = ck.shape[1]
    seq = lambda n: pl.BlockSpec((1, n, w), lambda i, r: (i, 0, 0))
    row = lambda: pl.BlockSpec((1, NA_ROWS_PER_STEP * GRID_W, w), lambda i, r: (i, r, 0))
    return pl.pallas_call(
        functools.partial(_na_kernel, rows),
        grid=(b, rows // NA_ROWS_PER_STEP),
        in_specs=[row(), seq(t), seq(t), seq(p), seq(p), _resident(bias.shape)],
        out_specs=row(),
        out_shape=jax.ShapeDtypeStruct((b, t, w), BF16),
        compiler_params=_params(2),
        name="neighbourhood_attention",
    )(q, k, v, ck, cv, bias)


def _dft_tables(t):
    def cs(n):
        idx = (np.arange(n)[:, None] * np.arange(n)[None, :]) % n
        ang = 2.0 * np.pi * idx.astype(np.float64) / n
        return np.cos(ang), np.sin(ang)
    ct, st = cs(t)
    cg, sg = cs(FN_GW)
    return (np.concatenate([ct, -st], axis=1).astype(np.float32),
            np.concatenate([cg, sg], axis=1).astype(np.float32))


def _fnet_kernel(t, scale, f_ref, cgsg_ref, dft_ref, o_ref, xcs_ref):
    @pl.when(pl.program_id(1) == 0)
    def _():
        for g in range(FN_GROUPS):
            sl = slice(g * FN_GW, (g + 1) * FN_GW)
            xcs = _dot(f_ref[0, :, sl], cgsg_ref[...])
            xcs_ref[0:t, sl] = xcs[:, :FN_GW].astype(BF16)
            xcs_ref[t:2 * t, sl] = xcs[:, FN_GW:].astype(BF16)

    o_ref[0] = (_dot(dft_ref[...], xcs_ref[...]) * scale).astype(o_ref.dtype)


def _fourier_mix(f, dft, cgsg):
    b, t, w = f.shape
    tq = min(t, TOKEN_TILE)
    scale = float(1.0 / np.sqrt(float(t * FN_GW)))
    return pl.pallas_call(
        functools.partial(_fnet_kernel, t, scale),
        grid=(b, t // tq),
        in_specs=[
            pl.BlockSpec((1, t, w), lambda i, r: (i, 0, 0)),
            pl.BlockSpec(cgsg.shape, lambda i, r: (0, 0)),
            pl.BlockSpec((tq, 2 * t), lambda i, r: (r, 0)),
        ],
        out_specs=pl.BlockSpec((1, tq, w), lambda i, r: (i, r, 0)),
        out_shape=jax.ShapeDtypeStruct((b, t, w), BF16),
        scratch_shapes=[pltpu.VMEM((2 * t, w), BF16)],
        compiler_params=_params(2),
        name="fourier_mix",
    )(f, cgsg, dft)


def _ref_rows(b, b_ref, m, reverse):
    c = SCAN_CHUNK
    blk = 2 * m
    anchor = m if reverse else m - 1
    if blk >= 8:
        pieces = [jnp.broadcast_to(b_ref[p * blk + anchor:p * blk + anchor + 1, :], (blk, HG_DK))
                  for p in range(c // blk)]
        return pieces[0] if len(pieces) == 1 else jnp.concatenate(pieces, axis=0)
    pos = lax.broadcasted_iota(jnp.int32, (c, HG_DK), 0) % blk
    r = b
    for p in range(blk):
        delta = anchor - p
        if delta != 0:
            r = jnp.where(pos == p, pltpu.roll(b, (-delta) % c, 0), r)
    return r


def _level_masks():
    c = SCAN_CHUNK
    t = np.arange(c)
    left, right = [], []
    for m in SMALL_LEVELS:
        is_right = (t // m) % 2 == 1
        left.append(np.where(is_right, NEG_BIG, 0.0))
        right.append(np.where(is_right, 0.0, NEG_BIG))
    full = lambda rows: np.ascontiguousarray(
        np.broadcast_to(np.stack(rows)[:, :, None], (len(rows), c, HG_DK))).astype(np.float32)
    return full(left), full(right)


def _tri_tables():
    t = np.arange(SCAN_CHUNK)
    return np.stack([t[None, :] <= t[:, None], t[None, :] >= t[:, None]]).astype(np.float32)


def _running_sum(parts, tri):
    b = None
    for part in parts:
        t = _dot(tri, part)
        b = t if b is None else b + t
    return b, jnp.min(parts[0].astype(F32))


def _scan_chunks(chains, ml_ref, mr_ref, fast):
    c = SCAN_CHUNK
    n_tiles = c // SUBLANES
    tile = lambda x, i: x[i * SUBLANES:(i + 1) * SUBLANES]
    row = lax.broadcasted_iota(jnp.int32, (c, c), 0)
    col = lax.broadcasted_iota(jnp.int32, (c, c), 1)
    xr = row ^ col
    zero_tile = jnp.zeros((SUBLANES, HG_DK), F32)

    os_, b_tots = [], []
    for reverse, q, v, k, b, st, b_ref in chains:
        edge = 0 if reverse else c - 1
        b_tots.append(b_ref[edge:edge + 1, :])
        os_.append(_dot_nt((q * jnp.exp2(b)).astype(BF16), st.astype(BF16)))

    a_tiles = [[None] * n_tiles for _ in chains]
    m = c // 2
    while m >= (FAST_BLOCK if fast else SUBLANES):
        g = m // SUBLANES
        for ci, (reverse, q, v, k, b, st, b_ref) in enumerate(chains):
            q_ids, qe_tiles, ke_tiles = [], [], [zero_tile] * n_tiles
            for p in range(n_tiles // (2 * g)):
                left = range(2 * g * p, 2 * g * p + g)
                right = range(2 * g * p + g, 2 * g * (p + 1))
                anchor = (2 * g * p + g) * SUBLANES - (0 if reverse else 1)
                r = jnp.broadcast_to(b_ref[anchor:anchor + 1, :], (SUBLANES, HG_DK))
                q_side, k_side = (left, right) if reverse else (right, left)
                for i in q_side:
                    q_ids.append(i)
                    qe_tiles.append(tile(q, i) * jnp.exp2(tile(b, i) - r))
                for i in k_side:
                    ke_tiles[i] = tile(k, i) * jnp.exp2(r - tile(b, i))
            lvl = _dot_nt(jnp.concatenate(qe_tiles, axis=0).astype(BF16),
                          jnp.concatenate(ke_tiles, axis=0).astype(BF16))
            for j, i in enumerate(q_ids):
                old = a_tiles[ci][i]
                a_tiles[ci][i] = tile(lvl, j) if old is None else jnp.where(tile(xr, i) < 2 * m, tile(lvl, j), old)
        m //= 2
    a_s = [jnp.concatenate([t if t is not None else zero_tile for t in tiles], axis=0) for tiles in a_tiles]

    if fast:
        for ci, (reverse, q, v, k, b, st, b_ref) in enumerate(chains):
            mid = FAST_BLOCK // 2
            r = jnp.concatenate(
                [jnp.broadcast_to(b_ref[j * FAST_BLOCK + mid:j * FAST_BLOCK + mid + 1, :], (FAST_BLOCK, HG_DK))
                 for j in range(c // FAST_BLOCK)], axis=0)
            qe = (q * jnp.exp2(b - r)).astype(BF16)
            ke = (k * jnp.exp2(r - b)).astype(BF16)
            a_s[ci] = jnp.where(xr < FAST_BLOCK, _dot_nt(qe, ke), a_s[ci])
    else:
        for level, m in enumerate(SMALL_LEVELS):
            for ci, (reverse, q, v, k, b, st, b_ref) in enumerate(chains):
                d = b - _ref_rows(b, b_ref, m, reverse)
                mask_q, mask_k = (ml_ref, mr_ref) if reverse else (mr_ref, ml_ref)
                qe = (q * jnp.exp2(d + mask_q[level])).astype(BF16)
                ke = (k * jnp.exp2(mask_k[level] - d)).astype(BF16)
                a_s[ci] = jnp.where(xr < 2 * m, _dot_nt(qe, ke), a_s[ci])

    out = []
    for (reverse, q, v, k, b, st, b_ref), a, o, b_tot in zip(chains, a_s, os_, b_tots):
        if fast:
            a = jnp.where((col >= row) if reverse else (col <= row), a, 0.0)
        else:
            diag = jnp.where(xr == 0, _dot_nt(q.astype(BF16), k.astype(BF16)), 0.0)
            a = jnp.where((col > row) if reverse else (col < row), a, diag)
        o = o + _dot(a.astype(BF16), v.astype(BF16))
        k_end = (k * jnp.exp2(b_tot - b)).astype(BF16)
        st_new = st * jnp.exp2(b_tot) + _dot(v.astype(F32).T.astype(BF16), k_end)
        out.append((o, st_new))
    return out


def _hgrn_kernel(n_chunks, n_heads, has_s0, *refs):
    refs = list(refs)
    q_ref, v_ref = refs[:2]
    k_in = (refs[2], refs[6])
    parts_in = (refs[3:6], refs[7:10])
    ml_ref, mr_ref, tri_ref = refs[10:13]
    del refs[:13]
    s0_ref = refs.pop(0) if has_s0 else None
    o_out = refs[:2]
    sfin_ref = refs[2]
    del refs[:3]
    n_chains = 2 * n_heads
    st_refs, b_refs = refs[:n_chains], refs[n_chains:2 * n_chains]
    fast_ref = refs[2 * n_chains]
    chains = [(hh, direction) for hh in range(n_heads) for direction in range(2)]
    lanes = lambda hh: slice(hh * HG_DK, (hh + 1) * HG_DK)

    for ci, (hh, direction) in enumerate(chains):
        if has_s0:
            st_refs[ci][...] = s0_ref[0, direction, hh].T
        else:
            st_refs[ci][...] = jnp.zeros((HG_DV, HG_DK), F32)

    def chunk_rows(c):
        return pl.ds(pl.multiple_of(c * SCAN_CHUNK, SCAN_CHUNK), SCAN_CHUNK)

    def gates(c_fwd, c_bwd):
        rows = (chunk_rows(c_fwd), chunk_rows(c_bwd))
        return [_running_sum([p[rows[direction], lanes(hh)] for p in parts_in[direction]], tri_ref[direction])
                for hh, direction in chains]

    def store_gates(gs):
        lo = None
        for (b, lo_c), b_ref in zip(gs, b_refs):
            b_ref[...] = b
            lo = lo_c if lo is None else jnp.minimum(lo, lo_c)
        fast_ref[0] = (lo >= -FAST_LOG2_LIMIT).astype(jnp.int32)

    store_gates(gates(0, n_chunks - 1))

    def trip(c, fast, last):
        rows = (chunk_rows(c), chunk_rows(n_chunks - 1 - c))
        nxt = None if last else gates(c + 1, n_chunks - 2 - c)
        outs = _scan_chunks(
            [(direction == 1, q_ref[rows[direction], lanes(hh)], v_ref[rows[direction], lanes(hh)],
              k_in[direction][rows[direction], lanes(hh)], b_refs[ci][...], st_refs[ci][...], b_refs[ci])
             for ci, (hh, direction) in enumerate(chains)],
            ml_ref, mr_ref, fast)
        for ci, ((hh, direction), (o, st)) in enumerate(zip(chains, outs)):
            st_refs[ci][...] = st
            o_out[direction][rows[direction], lanes(hh)] = o
        if not last:
            store_gates(nxt)

    def guarded_trip(c, last):
        use_fast = fast_ref[0] == 1
        pl.when(use_fast)(lambda: trip(c, True, last))
        pl.when(jnp.logical_not(use_fast))(lambda: trip(c, False, last))

    def scan_body(c, carry):
        guarded_trip(c, False)
        return carry

    lax.fori_loop(0, n_chunks - 1, scan_body, 0)
    guarded_trip(n_chunks - 1, True)

    for ci, (hh, direction) in enumerate(chains):
        sfin_ref[0, direction, hh] = st_refs[ci][...].T


def _hgrn_bidir(seqs, q, v, gates, s0):
    n = q.shape[0]
    t = n // seqs
    n_chunks = t // SCAN_CHUNK
    bytes_per_elem = (4 + 2 + 2 * (4 + 3 * 2)) + 2 * 4
    n_heads = max(1, min(SCAN_HEADS_MAX, SCAN_WINDOW_BYTES // (2 * t * HG_DK * bytes_per_elem)))
    n_chains = 2 * n_heads
    ml, mr = _level_masks()
    tri = _tri_tables()
    tok = lambda: pl.BlockSpec((t, n_heads * HG_DK), lambda s, h: (s, h))
    state = lambda: pl.BlockSpec((1, 2, n_heads, HG_DK, HG_DV), lambda s, h: (s, 0, h, 0, 0))
    in_specs = [tok() for _ in range(10)] + [_resident(ml.shape), _resident(mr.shape), _resident(tri.shape)]
    args = [q, v, *gates, jnp.asarray(ml), jnp.asarray(mr), jnp.asarray(tri).astype(BF16)]
    if s0 is not None:
        in_specs.append(state())
        args.append(s0)
    return pl.pallas_call(
        functools.partial(_hgrn_kernel, n_chunks, n_heads, s0 is not None),
        grid=(seqs, HG_HEADS // n_heads),
        in_specs=in_specs,
        out_specs=[tok(), tok(), state()],
        out_shape=[jax.ShapeDtypeStruct((n, HG_KW), F32), jax.ShapeDtypeStruct((n, HG_KW), F32),
                   jax.ShapeDtypeStruct((seqs, 2, HG_HEADS, HG_DK, HG_DV), F32)],
        scratch_shapes=([pltpu.VMEM((HG_DV, HG_DK), F32)] * n_chains
                        + [pltpu.VMEM((SCAN_CHUNK, HG_DK), F32)] * n_chains
                        + [pltpu.SMEM((1,), jnp.int32)]),
        compiler_params=_params(2),
        name="hgrn2_bidir",
    )(*args)


def kernel(x_prompt, x_sample, cache_k, cache_v, state_hgrn, c, c_ctx, ada_w, ada_b, norm_g, ffn_w1, ffn_w2,
           mix0_w_in, mix0_w_out, na_rpb, mix1_w_in, mix1_w_out, hg_lb_logits, hg_norm_g, norm_f):
    bp, sp, d = x_prompt.shape
    bs, ts, _ = x_sample.shape
    xp = x_prompt.reshape(bp * sp, d)
    xs = x_sample.reshape(bs * ts, d)

    cond = jnp.zeros((COND_ROWS, d), F32).at[:bs].set(c).at[CTX_ROW].set(c_ctx)
    mod = _ada_mod(cond, ada_w, ada_b)

    w1 = ffn_w1.astype(BF16)
    w2 = ffn_w2.astype(BF16)

    new_k, new_v, new_s = [], [], []
    for l in range(DEPTH):
        last = l == DEPTH - 1

        def ffn(x, seq_len, j, half, final, mixed=None, scan=None):
            return _half_ffn(x, mod, l, seq_len, j, half, norm_g[l, j], w1, w2, norm_f, final, mixed, scan)

        xp = ffn(xp, None, 0, 0, False)
        xs = ffn(xs, ts, 0, 0, False)

        if l % 2 == 0:
            e = l // 2
            w_in = mix0_w_in[e].astype(BF16)
            w_out = mix0_w_out[e].astype(BF16)
            qk_scale = float(NA_HD) ** -0.5
            qp, kp, vp, fp = _in_proj(xp, mod, l, None, norm_g[l, 1], w_in,
                                      [(NA_W, BF16, qk_scale), (NA_W, F32, 1.0), (NA_W, F32, 1.0), (FN_W, BF16, 1.0)])
            qs, ks, vs, fs = _in_proj(xs, mod, l, ts, norm_g[l, 1], w_in,
                                      [(NA_W, BF16, qk_scale), (NA_W, BF16, 1.0), (NA_W, BF16, 1.0), (FN_W, BF16, 1.0)])
            new_k.append(kp.reshape(bp, sp, NA_HEADS, NA_HD))
            new_v.append(vp.reshape(bp, sp, NA_HEADS, NA_HD))

            r3 = lambda a, b_, t_: a.reshape(b_, t_, a.shape[-1])
            ap = _context_attention(r3(qp, bp, sp), r3(kp, bp, sp), r3(vp, bp, sp))
            ck = cache_k[:, e].reshape(bs, -1, NA_W).astype(BF16)
            cv = cache_v[:, e].reshape(bs, -1, NA_W).astype(BF16)
            a_s = _neighbourhood_attention(r3(qs, bs, ts), r3(ks, bs, ts), r3(vs, bs, ts), ck, cv,
                                           _na_bias_table(na_rpb[e]))
            cgsg = None
            fn = []
            for f3 in (r3(fp, bp, sp), r3(fs, bs, ts)):
                dft_np, cgsg_np = _dft_tables(f3.shape[1])
                fn.append(_fourier_mix(f3, jnp.asarray(dft_np).astype(BF16), jnp.asarray(cgsg_np).astype(BF16)))
            mixed_p = ([ap.reshape(bp * sp, NA_W), fn[0].reshape(bp * sp, FN_W)], w_out)
            mixed_s = ([a_s.reshape(bs * ts, NA_W), fn[1].reshape(bs * ts, FN_W)], w_out)
            scan_p = scan_s = None
        else:
            o = l // 2
            w_in = mix1_w_in[o].astype(BF16)
            w_out = mix1_w_out[o].astype(BF16)
            qp, vp, gate_p, *gates_p = _scan_in_proj(xp, mod, l, None, norm_g[l, 1], w_in, hg_lb_logits)
            qs, vs, gate_s, *gates_s = _scan_in_proj(xs, mod, l, ts, norm_g[l, 1], w_in, hg_lb_logits)
            ofp, obp, s_fin = _hgrn_bidir(bp, qp, vp, gates_p, None)
            ofs, obs, _ = _hgrn_bidir(bs, qs, vs, gates_s, state_hgrn[:, o])
            new_s.append(s_fin)
            mixed_p = mixed_s = None
            scan_p = (ofp, obp, gate_p, hg_norm_g[o], w_out)
            scan_s = (ofs, obs, gate_s, hg_norm_g[o], w_out)

        xp = ffn(xp, None, 2, 1, last, mixed_p, scan_p)
        xs = ffn(xs, ts, 2, 1, last, mixed_s, scan_s)

    stack = lambda parts: jnp.expand_dims(parts[0], 1) if len(parts) == 1 else jnp.stack(parts, axis=1)
    return (xp.reshape(bp, sp, d), xs.reshape(bs, ts, d), stack(new_k), stack(new_v), stack(new_s))
```

```python
import functools

import numpy as np
import jax
import jax.numpy as jnp
from jax import lax
from jax.experimental import pallas as pl
from jax.experimental.pallas import tpu as pltpu

F32 = jnp.float32
BF16 = jnp.bfloat16

D_MODEL = 1024
DEPTH = 2
GRID_W = 64
NA_HEADS = 8
NA_HD = 64
NA_W = NA_HEADS * NA_HD
NA_WIN_R = 8
NA_WIN_C = 16
CTX_SEQS_PER_STEP = 2
NA_ROWS_PER_STEP = 4
FN_GROUPS = 4
FN_GW = 128
FN_W = FN_GROUPS * FN_GW
HG_HEADS = 8
HG_DK = 128
HG_DV = 128
HG_KW = HG_HEADS * HG_DK
D_FF = 2816
N_MOD = 9
EPS = 1e-6

COND_ROWS = 8
CTX_ROW = 4
TOKEN_TILE = 512
PROJ_TILE = 512
SCAN_CHUNK = 128
SUBLANES = 8
SMALL_LEVELS = (4, 2, 1)
SCAN_HEADS_MAX = 4
SCAN_WINDOW_BYTES = 40 * 1024 * 1024
FAST_BLOCK = 32
FAST_LOG2_LIMIT = 125.0 / (FAST_BLOCK // 2)
MASK_VALUE = -1e30
NEG_BIG = -1e30
VMEM_LIMIT = 56 * 1024 * 1024


def _params(n_axes, vmem=VMEM_LIMIT):
    return pltpu.CompilerParams(dimension_semantics=("arbitrary",) * n_axes, vmem_limit_bytes=vmem)


def _resident(shape):
    nd = len(shape)
    return pl.BlockSpec(shape, lambda *_: (0,) * nd, pipeline_mode=pl.Buffered(1))


def _sigmoid(a):
    return 1.0 / (1.0 + jnp.exp(-a))


def _rms_mod(x, g, shift, scale):
    y = x * lax.rsqrt(jnp.mean(x * x, axis=-1, keepdims=True) + EPS)
    return (y * g) * (1.0 + scale) + shift


def _split2(x):
    hi = x.astype(BF16)
    lo = (x - hi.astype(F32)).astype(BF16)
    return hi, lo


def _split3(x):
    hi = x.astype(BF16)
    r = x - hi.astype(F32)
    mid = r.astype(BF16)
    lo = (r - mid.astype(F32)).astype(BF16)
    return hi, mid, lo


def _dot(a, b):
    return jnp.dot(a, b, preferred_element_type=F32)


def _dot_nt(a, b):
    return lax.dot_general(a, b, (((1,), (1,)), ((), ())), preferred_element_type=F32)


def _ada_kernel(cond_ref, w_ref, b_ref, o_ref):
    c = cond_ref[...]
    s = c * _sigmoid(c)
    s_hi, s_lo = _split2(s)
    w_hi, w_lo = _split2(w_ref[0])
    o_ref[0] = _dot(s_hi, w_hi) + _dot(s_hi, w_lo) + _dot(s_lo, w_hi) + b_ref[0]


def _ada_mod(cond, ada_w, ada_b):
    depth, d, n = ada_w.shape
    tn = 1152
    out = pl.pallas_call(
        _ada_kernel,
        grid=(depth, n // tn),
        in_specs=[
            pl.BlockSpec((COND_ROWS, d), lambda l, j: (0, 0)),
            pl.BlockSpec((1, d, tn), lambda l, j: (l, 0, j)),
            pl.BlockSpec((1, 1, tn), lambda l, j: (l, 0, j)),
        ],
        out_specs=pl.BlockSpec((1, COND_ROWS, tn), lambda l, j: (l, 0, j)),
        out_shape=jax.ShapeDtypeStruct((depth, COND_ROWS, n), F32),
        compiler_params=_params(2),
        name="ada_mod",
    )(cond, ada_w, ada_b.reshape(depth, 1, n))
    return out.reshape(depth, COND_ROWS, N_MOD, d)


def _mod_spec(layer, seq_len, tm):
    if seq_len is None:
        return pl.BlockSpec((1, 1, N_MOD, D_MODEL), lambda i: (layer, CTX_ROW, 0, 0))
    tiles_per_seq = seq_len // tm
    return pl.BlockSpec((1, 1, N_MOD, D_MODEL), lambda i: (layer, i // tiles_per_seq, 0, 0))


def _ffn_kernel(j, final_norm, n_acts, scan_mix, x_ref, mod_ref, g_ref, w1a_ref, w1b_ref, w2_ref, gf_ref, *refs):
    x = x_ref[...]
    m = mod_ref[0, 0]
    if scan_mix:
        of_ref, ob_ref, gate_ref, ng_ref, w_ref, o_ref = refs
        o = of_ref[...] + ob_ref[...]
        ng = ng_ref[...]
        heads = []
        for hh in range(HG_HEADS):
            oh = o[:, hh * HG_DV:(hh + 1) * HG_DV]
            heads.append(oh * lax.rsqrt(jnp.mean(oh * oh, axis=-1, keepdims=True) + EPS) * ng)
        gate = gate_ref[...]
        y = (jnp.concatenate(heads, axis=1) * (gate * _sigmoid(gate))).astype(BF16)
        x = x + m[5:6] * _dot(y, w_ref[...])
    else:
        a_refs, w_refs, o_ref = refs[:n_acts], refs[n_acts:2 * n_acts], refs[2 * n_acts]
        if n_acts:
            y = None
            for a_ref, w_ref in zip(a_refs, w_refs):
                t = _dot(a_ref[...], w_ref[...])
                y = t if y is None else y + t
            x = x + m[5:6] * y
    h = _rms_mod(x, g_ref[...], m[3 * j:3 * j + 1], m[3 * j + 1:3 * j + 2]).astype(BF16)
    a = _dot(h, w1a_ref[...])
    b = _dot(h, w1b_ref[...])
    act = (a * _sigmoid(a) * b).astype(BF16)
    y = x + (0.5 * m[3 * j + 2:3 * j + 3]) * _dot(act, w2_ref[...])
    if final_norm:
        y = y * lax.rsqrt(jnp.mean(y * y, axis=-1, keepdims=True) + EPS) * gf_ref[...]
    o_ref[...] = y


def _half_ffn(x, mod, layer, seq_len, j, half, g, w1, w2, gf, final_norm, mixed=None, scan=None):
    n, d = x.shape
    tm = TOKEN_TILE
    once = pl.Buffered(1)
    acts, w_out = mixed if mixed is not None else ((), None)
    width = acts[0].shape[1] if acts else 0
    mixed_specs = ([pl.BlockSpec((tm, width), lambda i: (i, 0)) for _ in acts]
                   + [pl.BlockSpec((width, d), functools.partial(lambda i, blk: (blk, 0), blk=blk),
                                   pipeline_mode=once) for blk in range(len(acts))])
    mixed_args = [*acts, *([w_out] * len(acts))]
    if scan is not None:
        o_fwd, o_bwd, gate, head_g, w_scan = scan
        mixed_specs = ([pl.BlockSpec((tm, d), lambda i: (i, 0)) for _ in range(3)]
                       + [_resident((1, HG_DV)), _resident(w_scan.shape)])
        mixed_args = [o_fwd, o_bwd, gate, head_g.reshape(1, HG_DV), w_scan]
    return pl.pallas_call(
        functools.partial(_ffn_kernel, j, final_norm, len(acts), scan is not None),
        grid=(n // tm,),
        in_specs=[
            pl.BlockSpec((tm, d), lambda i: (i, 0)),
            _mod_spec(layer, seq_len, tm),
            _resident((1, d)),
            pl.BlockSpec((None, None, d, D_FF), lambda i: (layer, half, 0, 0), pipeline_mode=once),
            pl.BlockSpec((None, None, d, D_FF), lambda i: (layer, half, 0, 1), pipeline_mode=once),
            pl.BlockSpec((None, None, D_FF, d), lambda i: (layer, half, 0, 0), pipeline_mode=once),
            _resident((1, d)),
        ] + mixed_specs,
        out_specs=pl.BlockSpec((tm, d), lambda i: (i, 0)),
        out_shape=jax.ShapeDtypeStruct((n, d), F32),
        compiler_params=_params(1),
        name="half_ffn",
    )(x, mod, g.reshape(1, d), w1, w1, w2, gf.reshape(1, d), *mixed_args)


def _proj_kernel(j, splits, x_ref, mod_ref, g_ref, w_ref, *o_refs):
    m = mod_ref[0, 0]
    h = _rms_mod(x_ref[...], g_ref[...], m[3 * j:3 * j + 1], m[3 * j + 1:3 * j + 2]).astype(BF16)
    u = _dot(h, w_ref[...])
    for (lo, hi, scale), o_ref in zip(splits, o_refs):
        piece = u[:, lo:hi]
        if scale != 1.0:
            piece = piece * scale
        o_ref[...] = piece.astype(o_ref.dtype)


def _in_proj(x, mod, layer, seq_len, g, w, outs):
    n, d = x.shape
    tm = PROJ_TILE
    splits, lo = [], 0
    for width, _, scale in outs:
        splits.append((lo, lo + width, scale))
        lo += width
    return pl.pallas_call(
        functools.partial(_proj_kernel, 1, tuple(splits)),
        grid=(n // tm,),
        in_specs=[
            pl.BlockSpec((tm, d), lambda i: (i, 0)),
            _mod_spec(layer, seq_len, tm),
            _resident((1, d)),
            _resident(w.shape),
        ],
        out_specs=[pl.BlockSpec((tm, width), lambda i: (i, 0)) for width, _, _ in outs],
        out_shape=[jax.ShapeDtypeStruct((n, width), dt) for width, dt, _ in outs],
        compiler_params=_params(1),
        name="mixer_in_proj",
    )(x, mod, g.reshape(1, d), w)


def _forget_gates(z, lb):
    e = jnp.exp(-jnp.abs(z))
    inv = 1.0 / (1.0 + e)
    pos_z = z >= 0
    sig = jnp.where(pos_z, inv, e * inv)
    sig_neg = jnp.where(pos_z, e * inv, inv)
    return (1.0 - lb) * sig_neg, jnp.log2(lb + (1.0 - lb) * sig)


def _head_pair_attention(pairs):
    lane = lax.broadcasted_iota(jnp.int32, (1, 2 * NA_HD), 1)
    first = lane < NA_HD
    scores = []
    for q, key_blocks, _, bias_blocks in pairs:
        zero = jnp.zeros_like(q)
        qm = jnp.concatenate([jnp.where(first, q, zero), jnp.where(first, zero, q)], axis=0)
        ss = []
        for kb, bias in zip(key_blocks, bias_blocks):
            s = _dot_nt(qm, kb)
            ss.append(s if bias is None else s + bias)
        scores.append(ss)
    maxes = []
    for ss in scores:
        mx = ss[0].max(axis=-1, keepdims=True)
        for s in ss[1:]:
            mx = jnp.maximum(mx, s.max(axis=-1, keepdims=True))
        maxes.append(mx)
    outs = []
    for (q, _, value_blocks, _), ss, mx in zip(pairs, scores, maxes):
        den, acc = None, None
        for s, vb in zip(ss, value_blocks):
            p = jnp.exp(s - mx)
            dsum = p.sum(axis=-1, keepdims=True)
            pv = _dot(p.astype(BF16), vb)
            den = dsum if den is None else den + dsum
            acc = pv if acc is None else acc + pv
        o = acc / den
        tq = q.shape[0]
        outs.append(jnp.where(first, o[:tq], o[tq:]))
    return outs


def _ctx_attn_kernel(q_ref, k_ref, v_ref, o_ref):
    dests = [(i, slice(pr * 2 * NA_HD, (pr + 1) * 2 * NA_HD))
             for i in range(CTX_SEQS_PER_STEP) for pr in range(NA_HEADS // 2)]
    outs = _head_pair_attention([(q_ref[i, :, sl], [k_ref[i, :, sl].astype(BF16)],
                                  [v_ref[i, :, sl].astype(BF16)], [None]) for i, sl in dests])
    for (i, sl), o in zip(dests, outs):
        o_ref[i, :, sl] = o.astype(o_ref.dtype)


def _context_attention(q, k, v):
    b, s, w = q.shape
    blk = lambda: pl.BlockSpec((CTX_SEQS_PER_STEP, s, w), lambda i: (i, 0, 0))
    return pl.pallas_call(
        _ctx_attn_kernel,
        grid=(b // CTX_SEQS_PER_STEP,),
        in_specs=[blk(), blk(), blk()],
        out_specs=blk(),
        out_shape=jax.ShapeDtypeStruct((b, s, w), BF16),
        compiler_params=_params(1),
        name="context_attention",
    )(q, k, v)


def _na_kernel(rows, q_ref, k_ref, v_ref, ck_ref, cv_ref, bias_ref, o_ref):
    slices = [slice(pr * 2 * NA_HD, (pr + 1) * 2 * NA_HD) for pr in range(NA_HEADS // 2)]
    pairs, dests = [], []
    for rr in range(NA_ROWS_PER_STEP):
        r = pl.program_id(1) * NA_ROWS_PER_STEP + rr
        r0 = jnp.clip(r - NA_WIN_R // 2, 0, rows - NA_WIN_R)
        start = pl.multiple_of(r0 * GRID_W, GRID_W)
        win = pl.ds(start, NA_WIN_R * GRID_W)
        first = (NA_WIN_R - 1) - (r - r0)
        q_rows = slice(rr * GRID_W, (rr + 1) * GRID_W)
        for pr, sl in enumerate(slices):
            bias = jnp.concatenate(
                [jnp.concatenate([bias_ref[2 * pr + hh, first + 2 * jj] for jj in range(NA_WIN_R // 2)], axis=1)
                 for hh in range(2)], axis=0)
            pairs.append((q_ref[0, q_rows, sl], [k_ref[0, win, sl], ck_ref[0, :, sl]],
                          [v_ref[0, win, sl], cv_ref[0, :, sl]], [bias, None]))
            dests.append((q_rows, sl))
    for (q_rows, sl), o in zip(dests, _head_pair_attention(pairs)):
        o_ref[0, q_rows, sl] = o.astype(o_ref.dtype)


def _na_bias_table(rpb):
    cols = np.arange(GRID_W)
    col_start = np.clip(cols - NA_WIN_C // 2, 0, GRID_W - NA_WIN_C)
    inside = (cols[None, :] >= col_start[:, None]) & (cols[None, :] < col_start[:, None] + NA_WIN_C)
    pad = GRID_W - NA_WIN_C
    padded = jnp.pad(rpb.astype(F32), ((0, 0), (0, 0), (pad, pad)))
    t = jnp.stack([padded[:, :, GRID_W - 1 - w:2 * GRID_W - 1 - w] for w in range(GRID_W)], axis=2)
    t = jnp.where(inside[None, None], t, MASK_VALUE)
    return jnp.concatenate([t[:, :-1], t[:, 1:]], axis=-1)


def _neighbourhood_attention(q, k, v, ck, cv, bias):
    b, t, w = q.shape
    rows = t // GRID_W
    p = ck.shape[1]
    seq = lambda n: pl.BlockSpec((1, n, w), lambda i, r: (i, 0, 0))
    row = lambda: pl.BlockSpec((1, NA_ROWS_PER_STEP * GRID_W, w), lambda i, r: (i, r, 0))
    return pl.pallas_call(
        functools.partial(_na_kernel, rows),
        grid=(b, rows // NA_ROWS_PER_STEP),
        in_specs=[row(), seq(t), seq(t), seq(p), seq(p), _resident(bias.shape)],
        out_specs=row(),
        out_shape=jax.ShapeDtypeStruct((b, t, w), BF16),
        compiler_params=_params(2),
        name="neighbourhood_attention",
    )(q, k, v, ck, cv, bias)


def _dft_tables(t):
    def cs(n):
        idx = (np.arange(n)[:, None] * np.arange(n)[None, :]) % n
        ang = 2.0 * np.pi * idx.astype(np.float64) / n
        return np.cos(ang), np.sin(ang)
    ct, st = cs(t)
    cg, sg = cs(FN_GW)
    return (np.concatenate([ct, -st], axis=1).astype(np.float32),
            np.concatenate([cg, sg], axis=1).astype(np.float32))


def _fnet_kernel(t, scale, f_ref, cgsg_ref, dft_ref, o_ref, xcs_ref):
    @pl.when(pl.program_id(1) == 0)
    def _():
        for g in range(FN_GROUPS):
            sl = slice(g * FN_GW, (g + 1) * FN_GW)
            xcs = _dot(f_ref[0, :, sl], cgsg_ref[...])
            xcs_ref[0:t, sl] = xcs[:, :FN_GW].astype(BF16)
            xcs_ref[t:2 * t, sl] = xcs[:, FN_GW:].astype(BF16)

    o_ref[0] = (_dot(dft_ref[...], xcs_ref[...]) * scale).astype(o_ref.dtype)


def _fourier_mix(f, dft, cgsg):
    b, t, w = f.shape
    tq = min(t, TOKEN_TILE)
    scale = float(1.0 / np.sqrt(float(t * FN_GW)))
    return pl.pallas_call(
        functools.partial(_fnet_kernel, t, scale),
        grid=(b, t // tq),
        in_specs=[
            pl.BlockSpec((1, t, w), lambda i, r: (i, 0, 0)),
            pl.BlockSpec(cgsg.shape, lambda i, r: (0, 0)),
            pl.BlockSpec((tq, 2 * t), lambda i, r: (r, 0)),
        ],
        out_specs=pl.BlockSpec((1, tq, w), lambda i, r: (i, r, 0)),
        out_shape=jax.ShapeDtypeStruct((b, t, w), BF16),
        scratch_shapes=[pltpu.VMEM((2 * t, w), BF16)],
        compiler_params=_params(2),
        name="fourier_mix",
    )(f, cgsg, dft)


def _ref_rows(b, b_ref, m, reverse):
    c = SCAN_CHUNK
    blk = 2 * m
    anchor = m if reverse else m - 1
    if blk >= 8:
        pieces = [jnp.broadcast_to(b_ref[p * blk + anchor:p * blk + anchor + 1, :], (blk, HG_DK))
                  for p in range(c // blk)]
        return pieces[0] if len(pieces) == 1 else jnp.concatenate(pieces, axis=0)
    pos = lax.broadcasted_iota(jnp.int32, (c, HG_DK), 0) % blk
    r = b
    for p in range(blk):
        delta = anchor - p
        if delta != 0:
            r = jnp.where(pos == p, pltpu.roll(b, (-delta) % c, 0), r)
    return r


def _level_masks():
    c = SCAN_CHUNK
    t = np.arange(c)
    left, right = [], []
    for m in SMALL_LEVELS:
        is_right = (t // m) % 2 == 1
        left.append(np.where(is_right, NEG_BIG, 0.0))
        right.append(np.where(is_right, 0.0, NEG_BIG))
    full = lambda rows: np.ascontiguousarray(
        np.broadcast_to(np.stack(rows)[:, :, None], (len(rows), c, HG_DK))).astype(np.float32)
    return full(left), full(right)


def _tri_tables():
    t = np.arange(SCAN_CHUNK)
    return np.stack([t[None, :] <= t[:, None], t[None, :] >= t[:, None]]).astype(np.float32)


def _scan_gates(z, lb, tri):
    k, log2_f = _forget_gates(z, lb)
    b = None
    for part in _split3(log2_f):
        t = _dot(tri, part)
        b = t if b is None else b + t
    return k, b, jnp.min(log2_f)


def _scan_chunks(chains, ml_ref, mr_ref, fast):
    c = SCAN_CHUNK
    n_tiles = c // SUBLANES
    tile = lambda x, i: x[i * SUBLANES:(i + 1) * SUBLANES]
    row = lax.broadcasted_iota(jnp.int32, (c, c), 0)
    col = lax.broadcasted_iota(jnp.int32, (c, c), 1)
    xr = row ^ col
    zero_tile = jnp.zeros((SUBLANES, HG_DK), F32)

    os_, b_tots = [], []
    for reverse, q, v, k, b, st, b_ref in chains:
        edge = 0 if reverse else c - 1
        b_tots.append(b_ref[edge:edge + 1, :])
        os_.append(_dot_nt((q * jnp.exp2(b)).astype(BF16), st.astype(BF16)))

    a_tiles = [[None] * n_tiles for _ in chains]
    m = c // 2
    while m >= (FAST_BLOCK if fast else SUBLANES):
        g = m // SUBLANES
        for ci, (reverse, q, v, k, b, st, b_ref) in enumerate(chains):
            q_ids, qe_tiles, ke_tiles = [], [], [zero_tile] * n_tiles
            for p in range(n_tiles // (2 * g)):
                left = range(2 * g * p, 2 * g * p + g)
                right = range(2 * g * p + g, 2 * g * (p + 1))
                anchor = (2 * g * p + g) * SUBLANES - (0 if reverse else 1)
                r = jnp.broadcast_to(b_ref[anchor:anchor + 1, :], (SUBLANES, HG_DK))
                q_side, k_side = (left, right) if reverse else (right, left)
                for i in q_side:
                    q_ids.append(i)
                    qe_tiles.append(tile(q, i) * jnp.exp2(tile(b, i) - r))
                for i in k_side:
                    ke_tiles[i] = tile(k, i) * jnp.exp2(r - tile(b, i))
            lvl = _dot_nt(jnp.concatenate(qe_tiles, axis=0).astype(BF16),
                          jnp.concatenate(ke_tiles, axis=0).astype(BF16))
            for j, i in enumerate(q_ids):
                old = a_tiles[ci][i]
                a_tiles[ci][i] = tile(lvl, j) if old is None else jnp.where(tile(xr, i) < 2 * m, tile(lvl, j), old)
        m //= 2
    a_s = [jnp.concatenate([t if t is not None else zero_tile for t in tiles], axis=0) for tiles in a_tiles]

    if fast:
        for ci, (reverse, q, v, k, b, st, b_ref) in enumerate(chains):
            mid = FAST_BLOCK // 2
            r = jnp.concatenate(
                [jnp.broadcast_to(b_ref[j * FAST_BLOCK + mid:j * FAST_BLOCK + mid + 1, :], (FAST_BLOCK, HG_DK))
                 for j in range(c // FAST_BLOCK)], axis=0)
            qe = (q * jnp.exp2(b - r)).astype(BF16)
            ke = (k * jnp.exp2(r - b)).astype(BF16)
            a_s[ci] = jnp.where(xr < FAST_BLOCK, _dot_nt(qe, ke), a_s[ci])
    else:
        for level, m in enumerate(SMALL_LEVELS):
            for ci, (reverse, q, v, k, b, st, b_ref) in enumerate(chains):
                d = b - _ref_rows(b, b_ref, m, reverse)
                mask_q, mask_k = (ml_ref, mr_ref) if reverse else (mr_ref, ml_ref)
                qe = (q * jnp.exp2(d + mask_q[level])).astype(BF16)
                ke = (k * jnp.exp2(mask_k[level] - d)).astype(BF16)
                a_s[ci] = jnp.where(xr < 2 * m, _dot_nt(qe, ke), a_s[ci])

    out = []
    for (reverse, q, v, k, b, st, b_ref), a, o, b_tot in zip(chains, a_s, os_, b_tots):
        if fast:
            a = jnp.where((col >= row) if reverse else (col <= row), a, 0.0)
        else:
            diag = jnp.where(xr == 0, _dot_nt(q.astype(BF16), k.astype(BF16)), 0.0)
            a = jnp.where((col > row) if reverse else (col < row), a, diag)
        o = o + _dot(a.astype(BF16), v.astype(BF16))
        k_end = (k * jnp.exp2(b_tot - b)).astype(BF16)
        st_new = st * jnp.exp2(b_tot) + _dot(v.astype(F32).T.astype(BF16), k_end)
        out.append((o, st_new))
    return out


def _hgrn_kernel(layer, n_chunks, n_heads, has_s0, *refs):
    refs = list(refs)
    q_ref, zf_ref, zb_ref, v_ref, lg_ref, ml_ref, mr_ref, tri_ref = refs[:8]
    del refs[:8]
    s0_ref = refs.pop(0) if has_s0 else None
    o_out = refs[:2]
    sfin_ref = refs[2]
    del refs[:3]
    n_chains = 2 * n_heads
    st_refs, b_refs, k_refs = (refs[i * n_chains:(i + 1) * n_chains] for i in range(3))
    fast_ref = refs[3 * n_chains]
    chains = [(hh, direction) for hh in range(n_heads) for direction in range(2)]
    z_refs = (zf_ref, zb_ref)
    lanes = lambda hh: slice(hh * HG_DK, (hh + 1) * HG_DK)

    for ci, (hh, direction) in enumerate(chains):
        if has_s0:
            st_refs[ci][...] = s0_ref[0, direction, hh].T
        else:
            st_refs[ci][...] = jnp.zeros((HG_DV, HG_DK), F32)

    def lower_bound(hh, direction):
        lg = lg_ref[:, direction, hh, 0, :]
        ex = jnp.exp(lg - lg.max(axis=0, keepdims=True))
        return ex[1:layer + 1].sum(axis=0, keepdims=True) / ex.sum(axis=0, keepdims=True)

    lbs = [lower_bound(hh, direction) for hh, direction in chains]

    def chunk_rows(c):
        return pl.ds(pl.multiple_of(c * SCAN_CHUNK, SCAN_CHUNK), SCAN_CHUNK)

    def gates(c_fwd, c_bwd):
        rows = (chunk_rows(c_fwd), chunk_rows(c_bwd))
        return [_scan_gates(z_refs[direction][rows[direction], lanes(hh)], lb, tri_ref[direction])
                for (hh, direction), lb in zip(chains, lbs)]

    def store_gates(gs):
        lo = None
        for (k, b, lo_c), k_ref, b_ref in zip(gs, k_refs, b_refs):
            k_ref[...] = k
            b_ref[...] = b
            lo = lo_c if lo is None else jnp.minimum(lo, lo_c)
        fast_ref[0] = (lo >= -FAST_LOG2_LIMIT).astype(jnp.int32)

    store_gates(gates(0, n_chunks - 1))

    def trip(c, fast, last):
        rows = (chunk_rows(c), chunk_rows(n_chunks - 1 - c))
        nxt = None if last else gates(c + 1, n_chunks - 2 - c)
        outs = _scan_chunks(
            [(direction == 1, q_ref[rows[direction], lanes(hh)], v_ref[rows[direction], lanes(hh)],
              k_refs[ci][...], b_refs[ci][...], st_refs[ci][...], b_refs[ci])
             for ci, (hh, direction) in enumerate(chains)],
            ml_ref, mr_ref, fast)
        for ci, ((hh, direction), (o, st)) in enumerate(zip(chains, outs)):
            st_refs[ci][...] = st
            o_out[direction][rows[direction], lanes(hh)] = o
        if not last:
            store_gates(nxt)

    def guarded_trip(c, last):
        use_fast = fast_ref[0] == 1
        pl.when(use_fast)(lambda: trip(c, True, last))
        pl.when(jnp.logical_not(use_fast))(lambda: trip(c, False, last))

    def scan_body(c, carry):
        guarded_trip(c, False)
        return carry

    lax.fori_loop(0, n_chunks - 1, scan_body, 0)
    guarded_trip(n_chunks - 1, True)

    for ci, (hh, direction) in enumerate(chains):
        sfin_ref[0, direction, hh] = st_refs[ci][...].T


def _hgrn_bidir(layer, seqs, q, zf, zb, v, logits, s0):
    n = q.shape[0]
    t = n // seqs
    n_chunks = t // SCAN_CHUNK
    bytes_per_elem = (3 * 4 + 2) + 2 * 4
    fits = lambda heads: 2 * t * heads * HG_DK * bytes_per_elem <= SCAN_WINDOW_BYTES
    n_heads = next((heads for heads in (4, 2) if heads <= SCAN_HEADS_MAX and fits(heads)), 1)
    n_chains = 2 * n_heads
    ml, mr = _level_masks()
    tri = _tri_tables()
    tok = lambda: pl.BlockSpec((t, n_heads * HG_DK), lambda s, h: (s, h))
    state = lambda: pl.BlockSpec((1, 2, n_heads, HG_DK, HG_DV), lambda s, h: (s, 0, h, 0, 0))
    in_specs = [tok(), tok(), tok(), tok(),
                pl.BlockSpec((DEPTH, 2, n_heads, 1, HG_DK), lambda s, h: (0, 0, h, 0, 0)),
                _resident(ml.shape), _resident(mr.shape), _resident(tri.shape)]
    args = [q, zf, zb, v, logits.reshape(DEPTH, 2, HG_HEADS, 1, HG_DK),
            jnp.asarray(ml), jnp.asarray(mr), jnp.asarray(tri).astype(BF16)]
    if s0 is not None:
        in_specs.append(state())
        args.append(s0)
    return pl.pallas_call(
        functools.partial(_hgrn_kernel, layer, n_chunks, n_heads, s0 is not None),
        grid=(seqs, HG_HEADS // n_heads),
        in_specs=in_specs,
        out_specs=[tok(), tok(), state()],
        out_shape=[jax.ShapeDtypeStruct((n, HG_KW), F32), jax.ShapeDtypeStruct((n, HG_KW), F32),
                   jax.ShapeDtypeStruct((seqs, 2, HG_HEADS, HG_DK, HG_DV), F32)],
        scratch_shapes=([pltpu.VMEM((HG_DV, HG_DK), F32)] * n_chains
                        + [pltpu.VMEM((SCAN_CHUNK, HG_DK), F32)] * n_chains
                        + [pltpu.VMEM((SCAN_CHUNK, HG_DK), F32)] * n_chains
                        + [pltpu.SMEM((1,), jnp.int32)]),
        compiler_params=_params(2),
        name="hgrn2_bidir",
    )(*args)


def kernel(x_prompt, x_sample, cache_k, cache_v, state_hgrn, c, c_ctx, ada_w, ada_b, norm_g, ffn_w1, ffn_w2,
           mix0_w_in, mix0_w_out, na_rpb, mix1_w_in, mix1_w_out, hg_lb_logits, hg_norm_g, norm_f):
    bp, sp, d = x_prompt.shape
    bs, ts, _ = x_sample.shape
    xp = x_prompt.reshape(bp * sp, d)
    xs = x_sample.reshape(bs * ts, d)

    cond = jnp.zeros((COND_ROWS, d), F32).at[:bs].set(c).at[CTX_ROW].set(c_ctx)
    mod = _ada_mod(cond, ada_w, ada_b)

    w1 = ffn_w1.astype(BF16)
    w2 = ffn_w2.astype(BF16)

    new_k, new_v, new_s = [], [], []
    for l in range(DEPTH):
        last = l == DEPTH - 1

        def ffn(x, seq_len, j, half, final, mixed=None, scan=None):
            return _half_ffn(x, mod, l, seq_len, j, half, norm_g[l, j], w1, w2, norm_f, final, mixed, scan)

        xp = ffn(xp, None, 0, 0, False)
        xs = ffn(xs, ts, 0, 0, False)

        if l % 2 == 0:
            e = l // 2
            w_in = mix0_w_in[e].astype(BF16)
            w_out = mix0_w_out[e].astype(BF16)
            qk_scale = float(NA_HD) ** -0.5
            qp, kp, vp, fp = _in_proj(xp, mod, l, None, norm_g[l, 1], w_in,
                                      [(NA_W, BF16, qk_scale), (NA_W, F32, 1.0), (NA_W, F32, 1.0), (FN_W, BF16, 1.0)])
            qs, ks, vs, fs = _in_proj(xs, mod, l, ts, norm_g[l, 1], w_in,
                                      [(NA_W, BF16, qk_scale), (NA_W, BF16, 1.0), (NA_W, BF16, 1.0), (FN_W, BF16, 1.0)])
            new_k.append(kp.reshape(bp, sp, NA_HEADS, NA_HD))
            new_v.append(vp.reshape(bp, sp, NA_HEADS, NA_HD))

            r3 = lambda a, b_, t_: a.reshape(b_, t_, a.shape[-1])
            ap = _context_attention(r3(qp, bp, sp), r3(kp, bp, sp), r3(vp, bp, sp))
            ck = cache_k[:, e].reshape(bs, -1, NA_W).astype(BF16)
            cv = cache_v[:, e].reshape(bs, -1, NA_W).astype(BF16)
            a_s = _neighbourhood_attention(r3(qs, bs, ts), r3(ks, bs, ts), r3(vs, bs, ts), ck, cv,
                                           _na_bias_table(na_rpb[e]))
            cgsg = None
            fn = []
            for f3 in (r3(fp, bp, sp), r3(fs, bs, ts)):
                dft_np, cgsg_np = _dft_tables(f3.shape[1])
                fn.append(_fourier_mix(f3, jnp.asarray(dft_np).astype(BF16), jnp.asarray(cgsg_np).astype(BF16)))
            mixed_p = ([ap.reshape(bp * sp, NA_W), fn[0].reshape(bp * sp, FN_W)], w_out)
            mixed_s = ([a_s.reshape(bs * ts, NA_W), fn[1].reshape(bs * ts, FN_W)], w_out)
            scan_p = scan_s = None
        else:
            o = l // 2
            w_in = mix1_w_in[o].astype(BF16)
            w_out = mix1_w_out[o].astype(BF16)
            outs = [(HG_KW, F32, 1.0)] * 3 + [(HG_KW, BF16, 1.0), (HG_KW, F32, 1.0)]
            qp, zfp, zbp, vp, gate_p = _in_proj(xp, mod, l, None, norm_g[l, 1], w_in, outs)
            qs, zfs, zbs, vs, gate_s = _in_proj(xs, mod, l, ts, norm_g[l, 1], w_in, outs)
            ofp, obp, s_fin = _hgrn_bidir(l, bp, qp, zfp, zbp, vp, hg_lb_logits, None)
            ofs, obs, _ = _hgrn_bidir(l, bs, qs, zfs, zbs, vs, hg_lb_logits, state_hgrn[:, o])
            new_s.append(s_fin)
            mixed_p = mixed_s = None
            scan_p = (ofp, obp, gate_p, hg_norm_g[o], w_out)
            scan_s = (ofs, obs, gate_s, hg_norm_g[o], w_out)

        xp = ffn(xp, None, 2, 1, last, mixed_p, scan_p)
        xs = ffn(xs, ts, 2, 1, last, mixed_s, scan_s)

    stack = lambda parts: jnp.expand_dims(parts[0], 1) if len(parts) == 1 else jnp.stack(parts, axis=1)
    return (xp.reshape(bp, sp, d), xs.reshape(bs, ts, d), stack(new_k), stack(new_v), stack(new_s))
```

```python
import functools

import numpy as np
import jax
import jax.numpy as jnp
from jax import lax
from jax.experimental import pallas as pl
from jax.experimental.pallas import tpu as pltpu

F32 = jnp.float32
BF16 = jnp.bfloat16

D_MODEL = 1024
DEPTH = 2
GRID_W = 64
NA_HEADS = 8
NA_HD = 64
NA_W = NA_HEADS * NA_HD
NA_WIN_R = 8
NA_WIN_C = 16
CTX_SEQS_PER_STEP = 2
NA_ROWS_PER_STEP = 4
FN_GROUPS = 4
FN_GW = 128
FN_W = FN_GROUPS * FN_GW
FN_ROWS_PER_STEP = 1024
HG_HEADS = 8
HG_DK = 128
HG_DV = 128
HG_KW = HG_HEADS * HG_DK
D_FF = 2816
N_MOD = 9
EPS = 1e-6

COND_ROWS = 8
CTX_ROW = 4
TOKEN_TILE = 512
PROJ_TILE = 512
SCAN_CHUNK = 128
SUBLANES = 8
SMALL_LEVELS = (4, 2, 1)
SCAN_HEADS_MAX = 4
SCAN_WINDOW_BYTES = 40 * 1024 * 1024
FAST_BLOCK = 32
FAST_LOG2_LIMIT = 125.0 / (FAST_BLOCK // 2)
MASK_VALUE = -1e30
NEG_BIG = -1e30
VMEM_LIMIT = 56 * 1024 * 1024


def _params(n_axes, vmem=VMEM_LIMIT):
    return pltpu.CompilerParams(dimension_semantics=("arbitrary",) * n_axes, vmem_limit_bytes=vmem)


def _resident(shape):
    nd = len(shape)
    return pl.BlockSpec(shape, lambda *_: (0,) * nd, pipeline_mode=pl.Buffered(1))


def _sigmoid(a):
    return 1.0 / (1.0 + jnp.exp(-a))


def _rms_mod(x, g, shift, scale):
    y = x * lax.rsqrt(jnp.mean(x * x, axis=-1, keepdims=True) + EPS)
    return (y * g) * (1.0 + scale) + shift


def _split2(x):
    hi = x.astype(BF16)
    lo = (x - hi.astype(F32)).astype(BF16)
    return hi, lo


def _split3(x):
    hi = x.astype(BF16)
    r = x - hi.astype(F32)
    mid = r.astype(BF16)
    lo = (r - mid.astype(F32)).astype(BF16)
    return hi, mid, lo


def _dot(a, b):
    return jnp.dot(a, b, preferred_element_type=F32)


def _dot_nt(a, b):
    return lax.dot_general(a, b, (((1,), (1,)), ((), ())), preferred_element_type=F32)


def _ada_kernel(cond_ref, w_ref, b_ref, o_ref):
    c = cond_ref[...]
    s = c * _sigmoid(c)
    s_hi, s_lo = _split2(s)
    w_hi, w_lo = _split2(w_ref[0])
    o_ref[0] = _dot(s_hi, w_hi) + _dot(s_hi, w_lo) + _dot(s_lo, w_hi) + b_ref[0]


def _ada_mod(cond, ada_w, ada_b):
    depth, d, n = ada_w.shape
    tn = 1152
    out = pl.pallas_call(
        _ada_kernel,
        grid=(depth, n // tn),
        in_specs=[
            pl.BlockSpec((COND_ROWS, d), lambda l, j: (0, 0)),
            pl.BlockSpec((1, d, tn), lambda l, j: (l, 0, j)),
            pl.BlockSpec((1, 1, tn), lambda l, j: (l, 0, j)),
        ],
        out_specs=pl.BlockSpec((1, COND_ROWS, tn), lambda l, j: (l, 0, j)),
        out_shape=jax.ShapeDtypeStruct((depth, COND_ROWS, n), F32),
        compiler_params=_params(2),
        name="ada_mod",
    )(cond, ada_w, ada_b.reshape(depth, 1, n))
    return out.reshape(depth, COND_ROWS, N_MOD, d)


def _mod_spec(layer, seq_len, tm):
    if seq_len is None:
        return pl.BlockSpec((1, 1, N_MOD, D_MODEL), lambda i: (layer, CTX_ROW, 0, 0))
    tiles_per_seq = seq_len // tm
    return pl.BlockSpec((1, 1, N_MOD, D_MODEL), lambda i: (layer, i // tiles_per_seq, 0, 0))


def _ffn_kernel(j, final_norm, n_acts, scan_mix, x_ref, mod_ref, g_ref, w1a_ref, w1b_ref, w2_ref, gf_ref, *refs):
    x = x_ref[...]
    m = mod_ref[0, 0]
    if scan_mix:
        of_ref, ob_ref, gate_ref, ng_ref, w_ref, o_ref = refs
        o = of_ref[...] + ob_ref[...]
        ng = ng_ref[...]
        heads = []
        for hh in range(HG_HEADS):
            oh = o[:, hh * HG_DV:(hh + 1) * HG_DV]
            heads.append(oh * lax.rsqrt(jnp.mean(oh * oh, axis=-1, keepdims=True) + EPS) * ng)
        gate = gate_ref[...]
        y = (jnp.concatenate(heads, axis=1) * (gate * _sigmoid(gate))).astype(BF16)
        x = x + m[5:6] * _dot(y, w_ref[...])
    else:
        a_refs, w_refs, o_ref = refs[:n_acts], refs[n_acts:2 * n_acts], refs[2 * n_acts]
        if n_acts:
            y = None
            for a_ref, w_ref in zip(a_refs, w_refs):
                t = _dot(a_ref[...], w_ref[...])
                y = t if y is None else y + t
            x = x + m[5:6] * y
    h = _rms_mod(x, g_ref[...], m[3 * j:3 * j + 1], m[3 * j + 1:3 * j + 2]).astype(BF16)
    a = _dot(h, w1a_ref[...])
    b = _dot(h, w1b_ref[...])
    act = (a * _sigmoid(a) * b).astype(BF16)
    y = x + (0.5 * m[3 * j + 2:3 * j + 3]) * _dot(act, w2_ref[...])
    if final_norm:
        y = y * lax.rsqrt(jnp.mean(y * y, axis=-1, keepdims=True) + EPS) * gf_ref[...]
    o_ref[...] = y


def _half_ffn(x, mod, layer, seq_len, j, half, g, w1, w2, gf, final_norm, mixed=None, scan=None):
    n, d = x.shape
    tm = TOKEN_TILE
    once = pl.Buffered(1)
    acts, w_out = mixed if mixed is not None else ((), None)
    width = acts[0].shape[1] if acts else 0
    mixed_specs = ([pl.BlockSpec((tm, width), lambda i: (i, 0)) for _ in acts]
                   + [pl.BlockSpec((width, d), functools.partial(lambda i, blk: (blk, 0), blk=blk),
                                   pipeline_mode=once) for blk in range(len(acts))])
    mixed_args = [*acts, *([w_out] * len(acts))]
    if scan is not None:
        o_fwd, o_bwd, gate, head_g, w_scan = scan
        mixed_specs = ([pl.BlockSpec((tm, d), lambda i: (i, 0)) for _ in range(3)]
                       + [_resident((1, HG_DV)), _resident(w_scan.shape)])
        mixed_args = [o_fwd, o_bwd, gate, head_g.reshape(1, HG_DV), w_scan]
    return pl.pallas_call(
        functools.partial(_ffn_kernel, j, final_norm, len(acts), scan is not None),
        grid=(n // tm,),
        in_specs=[
            pl.BlockSpec((tm, d), lambda i: (i, 0)),
            _mod_spec(layer, seq_len, tm),
            _resident((1, d)),
            pl.BlockSpec((None, None, d, D_FF), lambda i: (layer, half, 0, 0), pipeline_mode=once),
            pl.BlockSpec((None, None, d, D_FF), lambda i: (layer, half, 0, 1), pipeline_mode=once),
            pl.BlockSpec((None, None, D_FF, d), lambda i: (layer, half, 0, 0), pipeline_mode=once),
            _resident((1, d)),
        ] + mixed_specs,
        out_specs=pl.BlockSpec((tm, d), lambda i: (i, 0)),
        out_shape=jax.ShapeDtypeStruct((n, d), F32),
        compiler_params=_params(1),
        name="half_ffn",
    )(x, mod, g.reshape(1, d), w1, w1, w2, gf.reshape(1, d), *mixed_args)


def _proj_kernel(j, splits, x_ref, mod_ref, g_ref, w_ref, *o_refs):
    m = mod_ref[0, 0]
    h = _rms_mod(x_ref[...], g_ref[...], m[3 * j:3 * j + 1], m[3 * j + 1:3 * j + 2]).astype(BF16)
    u = _dot(h, w_ref[...])
    for (lo, hi, scale), o_ref in zip(splits, o_refs):
        piece = u[:, lo:hi]
        if scale != 1.0:
            piece = piece * scale
        o_ref[...] = piece.astype(o_ref.dtype)


def _in_proj(x, mod, layer, seq_len, g, w, outs):
    n, d = x.shape
    tm = PROJ_TILE
    splits, lo = [], 0
    for width, _, scale in outs:
        splits.append((lo, lo + width, scale))
        lo += width
    return pl.pallas_call(
        functools.partial(_proj_kernel, 1, tuple(splits)),
        grid=(n // tm,),
        in_specs=[
            pl.BlockSpec((tm, d), lambda i: (i, 0)),
            _mod_spec(layer, seq_len, tm),
            _resident((1, d)),
            _resident(w.shape),
        ],
        out_specs=[pl.BlockSpec((tm, width), lambda i: (i, 0)) for width, _, _ in outs],
        out_shape=[jax.ShapeDtypeStruct((n, width), dt) for width, dt, _ in outs],
        compiler_params=_params(1),
        name="mixer_in_proj",
    )(x, mod, g.reshape(1, d), w)


def _forget_gates(z, lb):
    e = jnp.exp(-jnp.abs(z))
    inv = 1.0 / (1.0 + e)
    pos_z = z >= 0
    sig = jnp.where(pos_z, inv, e * inv)
    sig_neg = jnp.where(pos_z, e * inv, inv)
    return (1.0 - lb) * sig_neg, jnp.log2(lb + (1.0 - lb) * sig)


def _head_pair_attention(pairs):
    lane = lax.broadcasted_iota(jnp.int32, (1, 2 * NA_HD), 1)
    first = lane < NA_HD
    scores = []
    for q, key_blocks, _, bias_blocks in pairs:
        zero = jnp.zeros_like(q)
        qm = jnp.concatenate([jnp.where(first, q, zero), jnp.where(first, zero, q)], axis=0)
        ss = []
        for kb, bias in zip(key_blocks, bias_blocks):
            s = _dot_nt(qm, kb)
            ss.append(s if bias is None else s + bias)
        scores.append(ss)
    maxes = []
    for ss in scores:
        mx = ss[0].max(axis=-1, keepdims=True)
        for s in ss[1:]:
            mx = jnp.maximum(mx, s.max(axis=-1, keepdims=True))
        maxes.append(mx)
    outs = []
    for (q, _, value_blocks, _), ss, mx in zip(pairs, scores, maxes):
        den, acc = None, None
        for s, vb in zip(ss, value_blocks):
            p = jnp.exp(s - mx)
            dsum = p.sum(axis=-1, keepdims=True)
            pv = _dot(p.astype(BF16), vb)
            den = dsum if den is None else den + dsum
            acc = pv if acc is None else acc + pv
        o = acc / den
        tq = q.shape[0]
        outs.append(jnp.where(first, o[:tq], o[tq:]))
    return outs


def _ctx_attn_kernel(q_ref, k_ref, v_ref, o_ref):
    dests = [(i, slice(pr * 2 * NA_HD, (pr + 1) * 2 * NA_HD))
             for i in range(CTX_SEQS_PER_STEP) for pr in range(NA_HEADS // 2)]
    outs = _head_pair_attention([(q_ref[i, :, sl], [k_ref[i, :, sl].astype(BF16)],
                                  [v_ref[i, :, sl].astype(BF16)], [None]) for i, sl in dests])
    for (i, sl), o in zip(dests, outs):
        o_ref[i, :, sl] = o.astype(o_ref.dtype)


def _context_attention(q, k, v):
    b, s, w = q.shape
    blk = lambda: pl.BlockSpec((CTX_SEQS_PER_STEP, s, w), lambda i: (i, 0, 0))
    return pl.pallas_call(
        _ctx_attn_kernel,
        grid=(b // CTX_SEQS_PER_STEP,),
        in_specs=[blk(), blk(), blk()],
        out_specs=blk(),
        out_shape=jax.ShapeDtypeStruct((b, s, w), BF16),
        compiler_params=_params(1),
        name="context_attention",
    )(q, k, v)


def _na_kernel(rows, q_ref, k_ref, v_ref, ck_ref, cv_ref, bias_ref, o_ref):
    slices = [slice(pr * 2 * NA_HD, (pr + 1) * 2 * NA_HD) for pr in range(NA_HEADS // 2)]
    pairs, dests = [], []
    for rr in range(NA_ROWS_PER_STEP):
        r = pl.program_id(1) * NA_ROWS_PER_STEP + rr
        r0 = jnp.clip(r - NA_WIN_R // 2, 0, rows - NA_WIN_R)
        start = pl.multiple_of(r0 * GRID_W, GRID_W)
        win = pl.ds(start, NA_WIN_R * GRID_W)
        first = (NA_WIN_R - 1) - (r - r0)
        q_rows = slice(rr * GRID_W, (rr + 1) * GRID_W)
        for pr, sl in enumerate(slices):
            bias = jnp.concatenate(
                [jnp.concatenate([bias_ref[2 * pr + hh, first + 2 * jj] for jj in range(NA_WIN_R // 2)], axis=1)
                 for hh in range(2)], axis=0)
            pairs.append((q_ref[0, q_rows, sl], [k_ref[0, win, sl], ck_ref[0, :, sl]],
                          [v_ref[0, win, sl], cv_ref[0, :, sl]], [bias, None]))
            dests.append((q_rows, sl))
    for (q_rows, sl), o in zip(dests, _head_pair_attention(pairs)):
        o_ref[0, q_rows, sl] = o.astype(o_ref.dtype)


def _na_bias_table(rpb):
    cols = np.arange(GRID_W)
    col_start = np.clip(cols - NA_WIN_C // 2, 0, GRID_W - NA_WIN_C)
    inside = (cols[None, :] >= col_start[:, None]) & (cols[None, :] < col_start[:, None] + NA_WIN_C)
    pad = GRID_W - NA_WIN_C
    period = 2 * GRID_W
    v = jnp.pad(rpb.astype(F32), ((0, 0), (0, 0), (pad, pad + 1)))
    t = jnp.tile(v, (1, 1, GRID_W))[:, :, :GRID_W * (period - 1)]
    t = t.reshape(*rpb.shape[:2], GRID_W, period - 1)[:, :, :, GRID_W - 1:]
    t = jnp.where(inside[None, None], t, MASK_VALUE)
    return jnp.concatenate([t[:, :-1], t[:, 1:]], axis=-1)


def _neighbourhood_attention(q, k, v, ck, cv, bias):
    b, t, w = q.shape
    rows = t // GRID_W
    p = ck.shape[1]
    seq = lambda n: pl.BlockSpec((1, n, w), lambda i, r: (i, 0, 0))
    row = lambda: pl.BlockSpec((1, NA_ROWS_PER_STEP * GRID_W, w), lambda i, r: (i, r, 0))
    return pl.pallas_call(
        functools.partial(_na_kernel, rows),
        grid=(b, rows // NA_ROWS_PER_STEP),
        in_specs=[row(), seq(t), seq(t), seq(p), seq(p), _resident(bias.shape)],
        out_specs=row(),
        out_shape=jax.ShapeDtypeStruct((b, t, w), BF16),
        compiler_params=_params(2),
        name="neighbourhood_attention",
    )(q, k, v, ck, cv, bias)


def _dft_tables(t):
    def cs(n):
        idx = (np.arange(n)[:, None] * np.arange(n)[None, :]) % n
        ang = 2.0 * np.pi * idx.astype(np.float64) / n
        return np.cos(ang), np.sin(ang)
    ct, st = cs(t)
    cg, sg = cs(FN_GW)
    return (np.concatenate([ct, -st], axis=1).astype(np.float32),
            np.concatenate([cg, sg], axis=1).astype(np.float32))


def _fnet_kernel(t, seqs, scale, f_ref, cgsg_ref, dft_ref, o_ref, xcs_ref):
    first = pl.program_id(1) * seqs

    @pl.when(pl.program_id(0) == 0)
    def _():
        for s in range(seqs):
            for g in range(FN_GROUPS):
                sl = slice(g * FN_GW, (g + 1) * FN_GW)
                xcs = _dot(f_ref[s, :, sl], cgsg_ref[...])
                xcs_ref[first + s, 0:t, sl] = xcs[:, :FN_GW].astype(BF16)
                xcs_ref[first + s, t:2 * t, sl] = xcs[:, FN_GW:].astype(BF16)

    for s in range(seqs):
        o_ref[s] = (_dot(dft_ref[...], xcs_ref[first + s]) * scale).astype(o_ref.dtype)


def _fourier_mix(f, dft, cgsg):
    b, t, w = f.shape
    tq = min(t, TOKEN_TILE)
    seqs = max(1, min(b, FN_ROWS_PER_STEP // t))
    groups = b // seqs
    scale = float(1.0 / np.sqrt(float(t * FN_GW)))
    return pl.pallas_call(
        functools.partial(_fnet_kernel, t, seqs, scale),
        grid=(t // tq, groups),
        in_specs=[
            pl.BlockSpec((seqs, t, w), lambda r, i: (jnp.where(r == 0, i, groups - 1), 0, 0)),
            pl.BlockSpec(cgsg.shape, lambda r, i: (0, 0)),
            pl.BlockSpec((tq, 2 * t), lambda r, i: (r, 0)),
        ],
        out_specs=pl.BlockSpec((seqs, tq, w), lambda r, i: (i, r, 0)),
        out_shape=jax.ShapeDtypeStruct((b, t, w), BF16),
        scratch_shapes=[pltpu.VMEM((b, 2 * t, w), BF16)],
        compiler_params=_params(2),
        name="fourier_mix",
    )(f, cgsg, dft)


def _ref_rows(b, b_ref, m, reverse):
    c = SCAN_CHUNK
    blk = 2 * m
    anchor = m if reverse else m - 1
    if blk >= 8:
        pieces = [jnp.broadcast_to(b_ref[p * blk + anchor:p * blk + anchor + 1, :], (blk, HG_DK))
                  for p in range(c // blk)]
        return pieces[0] if len(pieces) == 1 else jnp.concatenate(pieces, axis=0)
    pos = lax.broadcasted_iota(jnp.int32, (c, HG_DK), 0) % blk
    r = b
    for p in range(blk):
        delta = anchor - p
        if delta != 0:
            r = jnp.where(pos == p, pltpu.roll(b, (-delta) % c, 0), r)
    return r


def _level_masks():
    c = SCAN_CHUNK
    t = np.arange(c)
    left, right = [], []
    for m in SMALL_LEVELS:
        is_right = (t // m) % 2 == 1
        left.append(np.where(is_right, NEG_BIG, 0.0))
        right.append(np.where(is_right, 0.0, NEG_BIG))
    full = lambda rows: np.ascontiguousarray(
        np.broadcast_to(np.stack(rows)[:, :, None], (len(rows), c, HG_DK))).astype(np.float32)
    return full(left), full(right)


def _tri_tables():
    t = np.arange(SCAN_CHUNK)
    return np.stack([t[None, :] <= t[:, None], t[None, :] >= t[:, None]]).astype(np.float32)


def _scan_gates(z, lb, tri):
    k, log2_f = _forget_gates(z, lb)
    b = None
    for part in _split3(log2_f):
        t = _dot(tri, part)
        b = t if b is None else b + t
    return k, b, jnp.min(log2_f)


def _scan_chunks(chains, ml_ref, mr_ref, fast):
    c = SCAN_CHUNK
    n_tiles = c // SUBLANES
    tile = lambda x, i: x[i * SUBLANES:(i + 1) * SUBLANES]
    row = lax.broadcasted_iota(jnp.int32, (c, c), 0)
    col = lax.broadcasted_iota(jnp.int32, (c, c), 1)
    xr = row ^ col
    zero_tile = jnp.zeros((SUBLANES, HG_DK), F32)

    os_, b_tots = [], []
    for reverse, q, v, k, b, st, b_ref in chains:
        edge = 0 if reverse else c - 1
        b_tots.append(b_ref[edge:edge + 1, :])
        os_.append(_dot_nt((q * jnp.exp2(b)).astype(BF16), st.astype(BF16)))

    a_tiles = [[None] * n_tiles for _ in chains]
    m = c // 2
    while m >= (FAST_BLOCK if fast else SUBLANES):
        g = m // SUBLANES
        for ci, (reverse, q, v, k, b, st, b_ref) in enumerate(chains):
            q_ids, qe_tiles, ke_tiles = [], [], [zero_tile] * n_tiles
            for p in range(n_tiles // (2 * g)):
                left = range(2 * g * p, 2 * g * p + g)
                right = range(2 * g * p + g, 2 * g * (p + 1))
                anchor = (2 * g * p + g) * SUBLANES - (0 if reverse else 1)
                r = jnp.broadcast_to(b_ref[anchor:anchor + 1, :], (SUBLANES, HG_DK))
                q_side, k_side = (left, right) if reverse else (right, left)
                for i in q_side:
                    q_ids.append(i)
                    qe_tiles.append(tile(q, i) * jnp.exp2(tile(b, i) - r))
                for i in k_side:
                    ke_tiles[i] = tile(k, i) * jnp.exp2(r - tile(b, i))
            lvl = _dot_nt(jnp.concatenate(qe_tiles, axis=0).astype(BF16),
                          jnp.concatenate(ke_tiles, axis=0).astype(BF16))
            for j, i in enumerate(q_ids):
                old = a_tiles[ci][i]
                a_tiles[ci][i] = tile(lvl, j) if old is None else jnp.where(tile(xr, i) < 2 * m, tile(lvl, j), old)
        m //= 2
    a_s = [jnp.concatenate([t if t is not None else zero_tile for t in tiles], axis=0) for tiles in a_tiles]

    if fast:
        for ci, (reverse, q, v, k, b, st, b_ref) in enumerate(chains):
            mid = FAST_BLOCK // 2
            r = jnp.concatenate(
                [jnp.broadcast_to(b_ref[j * FAST_BLOCK + mid:j * FAST_BLOCK + mid + 1, :], (FAST_BLOCK, HG_DK))
                 for j in range(c // FAST_BLOCK)], axis=0)
            qe = (q * jnp.exp2(b - r)).astype(BF16)
            ke = (k * jnp.exp2(r - b)).astype(BF16)
            a_s[ci] = jnp.where(xr < FAST_BLOCK, _dot_nt(qe, ke), a_s[ci])
    else:
        for level, m in enumerate(SMALL_LEVELS):
            for ci, (reverse, q, v, k, b, st, b_ref) in enumerate(chains):
                d = b - _ref_rows(b, b_ref, m, reverse)
                mask_q, mask_k = (ml_ref, mr_ref) if reverse else (mr_ref, ml_ref)
                qe = (q * jnp.exp2(d + mask_q[level])).astype(BF16)
                ke = (k * jnp.exp2(mask_k[level] - d)).astype(BF16)
                a_s[ci] = jnp.where(xr < 2 * m, _dot_nt(qe, ke), a_s[ci])

    out = []
    for (reverse, q, v, k, b, st, b_ref), a, o, b_tot in zip(chains, a_s, os_, b_tots):
        if fast:
            a = jnp.where((col >= row) if reverse else (col <= row), a, 0.0)
        else:
            diag = jnp.where(xr == 0, _dot_nt(q.astype(BF16), k.astype(BF16)), 0.0)
            a = jnp.where((col > row) if reverse else (col < row), a, diag)
        o = o + _dot(a.astype(BF16), v.astype(BF16))
        k_end = (k * jnp.exp2(b_tot - b)).astype(BF16)
        st_new = st * jnp.exp2(b_tot) + _dot(v.astype(F32).T.astype(BF16), k_end)
        out.append((o, st_new))
    return out


def _hgrn_kernel(layer, n_chunks, n_heads, has_s0, *refs):
    refs = list(refs)
    q_ref, zf_ref, zb_ref, v_ref, lg_ref, ml_ref, mr_ref, tri_ref = refs[:8]
    del refs[:8]
    s0_ref = refs.pop(0) if has_s0 else None
    o_out = refs[:2]
    sfin_ref = refs[2]
    del refs[:3]
    n_chains = 2 * n_heads
    st_refs, b_refs, k_refs = (refs[i * n_chains:(i + 1) * n_chains] for i in range(3))
    fast_ref = refs[3 * n_chains]
    chains = [(hh, direction) for hh in range(n_heads) for direction in range(2)]
    z_refs = (zf_ref, zb_ref)
    lanes = lambda hh: slice(hh * HG_DK, (hh + 1) * HG_DK)

    for ci, (hh, direction) in enumerate(chains):
        if has_s0:
            st_refs[ci][...] = s0_ref[0, direction, hh].T
        else:
            st_refs[ci][...] = jnp.zeros((HG_DV, HG_DK), F32)

    def lower_bound(hh, direction):
        lg = lg_ref[:, direction, hh, 0, :]
        ex = jnp.exp(lg - lg.max(axis=0, keepdims=True))
        return ex[1:layer + 1].sum(axis=0, keepdims=True) / ex.sum(axis=0, keepdims=True)

    lbs = [lower_bound(hh, direction) for hh, direction in chains]

    def chunk_rows(c):
        return pl.ds(pl.multiple_of(c * SCAN_CHUNK, SCAN_CHUNK), SCAN_CHUNK)

    def gates(c_fwd, c_bwd):
        rows = (chunk_rows(c_fwd), chunk_rows(c_bwd))
        return [_scan_gates(z_refs[direction][rows[direction], lanes(hh)], lb, tri_ref[direction])
                for (hh, direction), lb in zip(chains, lbs)]

    def store_gates(gs):
        lo = None
        for (k, b, lo_c), k_ref, b_ref in zip(gs, k_refs, b_refs):
            k_ref[...] = k
            b_ref[...] = b
            lo = lo_c if lo is None else jnp.minimum(lo, lo_c)
        fast_ref[0] = (lo >= -FAST_LOG2_LIMIT).astype(jnp.int32)

    store_gates(gates(0, n_chunks - 1))

    def trip(c, fast, last):
        rows = (chunk_rows(c), chunk_rows(n_chunks - 1 - c))
        nxt = None if last else gates(c + 1, n_chunks - 2 - c)
        outs = _scan_chunks(
            [(direction == 1, q_ref[rows[direction], lanes(hh)], v_ref[rows[direction], lanes(hh)],
              k_refs[ci][...], b_refs[ci][...], st_refs[ci][...], b_refs[ci])
             for ci, (hh, direction) in enumerate(chains)],
            ml_ref, mr_ref, fast)
        for ci, ((hh, direction), (o, st)) in enumerate(zip(chains, outs)):
            st_refs[ci][...] = st
            o_out[direction][rows[direction], lanes(hh)] = o
        if not last:
            store_gates(nxt)

    def guarded_trip(c, last):
        use_fast = fast_ref[0] == 1
        pl.when(use_fast)(lambda: trip(c, True, last))
        pl.when(jnp.logical_not(use_fast))(lambda: trip(c, False, last))

    def scan_body(c, carry):
        guarded_trip(c, False)
        return carry

    lax.fori_loop(0, n_chunks - 1, scan_body, 0)
    guarded_trip(n_chunks - 1, True)

    for ci, (hh, direction) in enumerate(chains):
        sfin_ref[0, direction, hh] = st_refs[ci][...].T


def _hgrn_bidir(layer, seqs, q, zf, zb, v, logits, s0):
    n = q.shape[0]
    t = n // seqs
    n_chunks = t // SCAN_CHUNK
    bytes_per_elem = (3 * 4 + 2) + 2 * 4
    fits = lambda heads: 2 * t * heads * HG_DK * bytes_per_elem <= SCAN_WINDOW_BYTES
    n_heads = next((heads for heads in (4, 2) if heads <= SCAN_HEADS_MAX and fits(heads)), 1)
    n_chains = 2 * n_heads
    ml, mr = _level_masks()
    tri = _tri_tables()
    tok = lambda: pl.BlockSpec((t, n_heads * HG_DK), lambda s, h: (s, h))
    state = lambda: pl.BlockSpec((1, 2, n_heads, HG_DK, HG_DV), lambda s, h: (s, 0, h, 0, 0))
    in_specs = [tok(), tok(), tok(), tok(),
                pl.BlockSpec((DEPTH, 2, n_heads, 1, HG_DK), lambda s, h: (0, 0, h, 0, 0)),
                _resident(ml.shape), _resident(mr.shape), _resident(tri.shape)]
    args = [q, zf, zb, v, logits.reshape(DEPTH, 2, HG_HEADS, 1, HG_DK),
            jnp.asarray(ml), jnp.asarray(mr), jnp.asarray(tri).astype(BF16)]
    if s0 is not None:
        in_specs.append(state())
        args.append(s0)
    return pl.pallas_call(
        functools.partial(_hgrn_kernel, layer, n_chunks, n_heads, s0 is not None),
        grid=(seqs, HG_HEADS // n_heads),
        in_specs=in_specs,
        out_specs=[tok(), tok(), state()],
        out_shape=[jax.ShapeDtypeStruct((n, HG_KW), F32), jax.ShapeDtypeStruct((n, HG_KW), F32),
                   jax.ShapeDtypeStruct((seqs, 2, HG_HEADS, HG_DK, HG_DV), F32)],
        scratch_shapes=([pltpu.VMEM((HG_DV, HG_DK), F32)] * n_chains
                        + [pltpu.VMEM((SCAN_CHUNK, HG_DK), F32)] * n_chains
                        + [pltpu.VMEM((SCAN_CHUNK, HG_DK), F32)] * n_chains
                        + [pltpu.SMEM((1,), jnp.int32)]),
        compiler_params=_params(2),
        name="hgrn2_bidir",
    )(*args)


def kernel(x_prompt, x_sample, cache_k, cache_v, state_hgrn, c, c_ctx, ada_w, ada_b, norm_g, ffn_w1, ffn_w2,
           mix0_w_in, mix0_w_out, na_rpb, mix1_w_in, mix1_w_out, hg_lb_logits, hg_norm_g, norm_f):
    bp, sp, d = x_prompt.shape
    bs, ts, _ = x_sample.shape
    xp = x_prompt.reshape(bp * sp, d)
    xs = x_sample.reshape(bs * ts, d)

    cond = jnp.zeros((COND_ROWS, d), F32).at[:bs].set(c).at[CTX_ROW].set(c_ctx)
    mod = _ada_mod(cond, ada_w, ada_b)

    w1 = ffn_w1.astype(BF16)
    w2 = ffn_w2.astype(BF16)

    new_k, new_v, new_s = [], [], []
    for l in range(DEPTH):
        last = l == DEPTH - 1

        def ffn(x, seq_len, j, half, final, mixed=None, scan=None):
            return _half_ffn(x, mod, l, seq_len, j, half, norm_g[l, j], w1, w2, norm_f, final, mixed, scan)

        xp = ffn(xp, None, 0, 0, False)
        xs = ffn(xs, ts, 0, 0, False)

        if l % 2 == 0:
            e = l // 2
            w_in = mix0_w_in[e].astype(BF16)
            w_out = mix0_w_out[e].astype(BF16)
            qk_scale = float(NA_HD) ** -0.5
            qp, kp, vp, fp = _in_proj(xp, mod, l, None, norm_g[l, 1], w_in,
                                      [(NA_W, BF16, qk_scale), (NA_W, F32, 1.0), (NA_W, F32, 1.0), (FN_W, BF16, 1.0)])
            qs, ks, vs, fs = _in_proj(xs, mod, l, ts, norm_g[l, 1], w_in,
                                      [(NA_W, BF16, qk_scale), (NA_W, BF16, 1.0), (NA_W, BF16, 1.0), (FN_W, BF16, 1.0)])
            new_k.append(kp.reshape(bp, sp, NA_HEADS, NA_HD))
            new_v.append(vp.reshape(bp, sp, NA_HEADS, NA_HD))

            r3 = lambda a, b_, t_: a.reshape(b_, t_, a.shape[-1])
            ap = _context_attention(r3(qp, bp, sp), r3(kp, bp, sp), r3(vp, bp, sp))
            ck = cache_k[:, e].reshape(bs, -1, NA_W).astype(BF16)
            cv = cache_v[:, e].reshape(bs, -1, NA_W).astype(BF16)
            a_s = _neighbourhood_attention(r3(qs, bs, ts), r3(ks, bs, ts), r3(vs, bs, ts), ck, cv,
                                           _na_bias_table(na_rpb[e]))
            cgsg = None
            fn = []
            for f3 in (r3(fp, bp, sp), r3(fs, bs, ts)):
                dft_np, cgsg_np = _dft_tables(f3.shape[1])
                fn.append(_fourier_mix(f3, jnp.asarray(dft_np).astype(BF16), jnp.asarray(cgsg_np).astype(BF16)))
            mixed_p = ([ap.reshape(bp * sp, NA_W), fn[0].reshape(bp * sp, FN_W)], w_out)
            mixed_s = ([a_s.reshape(bs * ts, NA_W), fn[1].reshape(bs * ts, FN_W)], w_out)
            scan_p = scan_s = None
        else:
            o = l // 2
            w_in = mix1_w_in[o].astype(BF16)
            w_out = mix1_w_out[o].astype(BF16)
            outs = [(HG_KW, F32, 1.0)] * 3 + [(HG_KW, BF16, 1.0), (HG_KW, F32, 1.0)]
            qp, zfp, zbp, vp, gate_p = _in_proj(xp, mod, l, None, norm_g[l, 1], w_in, outs)
            qs, zfs, zbs, vs, gate_s = _in_proj(xs, mod, l, ts, norm_g[l, 1], w_in, outs)
            ofp, obp, s_fin = _hgrn_bidir(l, bp, qp, zfp, zbp, vp, hg_lb_logits, None)
            ofs, obs, _ = _hgrn_bidir(l, bs, qs, zfs, zbs, vs, hg_lb_logits, state_hgrn[:, o])
            new_s.append(s_fin)
            mixed_p = mixed_s = None
            scan_p = (ofp, obp, gate_p, hg_norm_g[o], w_out)
            scan_s = (ofs, obs, gate_s, hg_norm_g[o], w_out)

        xp = ffn(xp, None, 2, 1, last, mixed_p, scan_p)
        xs = ffn(xs, ts, 2, 1, last, mixed_s, scan_s)

    stack = lambda parts: jnp.expand_dims(parts[0], 1) if len(parts) == 1 else jnp.stack(parts, axis=1)
    return (xp.reshape(bp, sp, d), xs.reshape(bs, ts, d), stack(new_k), stack(new_v), stack(new_s))
```

```python
import functools

import numpy as np
import jax
import jax.numpy as jnp
from jax import lax
from jax.experimental import pallas as pl
from jax.experimental.pallas import tpu as pltpu

F32 = jnp.float32
BF16 = jnp.bfloat16

D_MODEL = 1024
DEPTH = 2
GRID_W = 64
NA_HEADS = 8
NA_HD = 64
NA_W = NA_HEADS * NA_HD
NA_WIN_R = 8
NA_WIN_C = 16
CTX_SEQS_PER_STEP = 2
NA_ROWS_PER_STEP = 4
FN_GROUPS = 4
FN_GW = 128
FN_W = FN_GROUPS * FN_GW
FN_ROWS_PER_STEP = 1024
HG_HEADS = 8
HG_DK = 128
HG_DV = 128
HG_KW = HG_HEADS * HG_DK
D_FF = 2816
N_MOD = 9
EPS = 1e-6

COND_ROWS = 8
CTX_ROW = 4
TOKEN_TILE = 512
PROJ_TILE = 512
SCAN_CHUNK = 128
SUBLANES = 8
SMALL_LEVELS = (4, 2, 1)
SCAN_HEADS_MAX = 8
SCAN_TIME_BLOCK = 1024
SCAN_WINDOW_BYTES = 40 * 1024 * 1024
FAST_BLOCK = 32
FAST_LOG2_LIMIT = 125.0 / (FAST_BLOCK // 2)
MASK_VALUE = -1e30
NEG_BIG = -1e30
VMEM_LIMIT = 56 * 1024 * 1024


def _params(n_axes, vmem=VMEM_LIMIT):
    return pltpu.CompilerParams(dimension_semantics=("arbitrary",) * n_axes, vmem_limit_bytes=vmem)


def _resident(shape):
    nd = len(shape)
    return pl.BlockSpec(shape, lambda *_: (0,) * nd, pipeline_mode=pl.Buffered(1))


def _sigmoid(a):
    return 1.0 / (1.0 + jnp.exp(-a))


def _rms_mod(x, g, shift, scale):
    y = x * lax.rsqrt(jnp.mean(x * x, axis=-1, keepdims=True) + EPS)
    return (y * g) * (1.0 + scale) + shift


def _split2(x):
    hi = x.astype(BF16)
    lo = (x - hi.astype(F32)).astype(BF16)
    return hi, lo


def _split3(x):
    hi = x.astype(BF16)
    r = x - hi.astype(F32)
    mid = r.astype(BF16)
    lo = (r - mid.astype(F32)).astype(BF16)
    return hi, mid, lo


def _dot(a, b):
    return jnp.dot(a, b, preferred_element_type=F32)


def _dot_nt(a, b):
    return lax.dot_general(a, b, (((1,), (1,)), ((), ())), preferred_element_type=F32)


def _ada_kernel(cond_ref, w_ref, b_ref, o_ref):
    c = cond_ref[...]
    s = c * _sigmoid(c)
    s_hi, s_lo = _split2(s)
    w_hi, w_lo = _split2(w_ref[0])
    o_ref[0] = _dot(s_hi, w_hi) + _dot(s_hi, w_lo) + _dot(s_lo, w_hi) + b_ref[0]


def _ada_mod(cond, ada_w, ada_b):
    depth, d, n = ada_w.shape
    tn = 1152
    out = pl.pallas_call(
        _ada_kernel,
        grid=(depth, n // tn),
        in_specs=[
            pl.BlockSpec((COND_ROWS, d), lambda l, j: (0, 0)),
            pl.BlockSpec((1, d, tn), lambda l, j: (l, 0, j)),
            pl.BlockSpec((1, 1, tn), lambda l, j: (l, 0, j)),
        ],
        out_specs=pl.BlockSpec((1, COND_ROWS, tn), lambda l, j: (l, 0, j)),
        out_shape=jax.ShapeDtypeStruct((depth, COND_ROWS, n), F32),
        compiler_params=_params(2),
        name="ada_mod",
    )(cond, ada_w, ada_b.reshape(depth, 1, n))
    return out.reshape(depth, COND_ROWS, N_MOD, d)


def _mod_spec(layer, seq_len, tm):
    if seq_len is None:
        return pl.BlockSpec((1, 1, N_MOD, D_MODEL), lambda i: (layer, CTX_ROW, 0, 0))
    tiles_per_seq = seq_len // tm
    return pl.BlockSpec((1, 1, N_MOD, D_MODEL), lambda i: (layer, i // tiles_per_seq, 0, 0))


def _ffn_kernel(j, final_norm, n_acts, scan_mix, x_ref, mod_ref, g_ref, w1a_ref, w1b_ref, w2_ref, gf_ref, *refs):
    x = x_ref[...]
    m = mod_ref[0, 0]
    if scan_mix:
        of_ref, ob_ref, gate_ref, ng_ref, w_ref, o_ref = refs
        o = of_ref[...] + ob_ref[...]
        ng = ng_ref[...]
        heads = []
        for hh in range(HG_HEADS):
            oh = o[:, hh * HG_DV:(hh + 1) * HG_DV]
            heads.append(oh * lax.rsqrt(jnp.mean(oh * oh, axis=-1, keepdims=True) + EPS) * ng)
        gate = gate_ref[...]
        y = (jnp.concatenate(heads, axis=1) * (gate * _sigmoid(gate))).astype(BF16)
        x = x + m[5:6] * _dot(y, w_ref[...])
    else:
        a_refs, w_refs, o_ref = refs[:n_acts], refs[n_acts:2 * n_acts], refs[2 * n_acts]
        if n_acts:
            y = None
            for a_ref, w_ref in zip(a_refs, w_refs):
                t = _dot(a_ref[...], w_ref[...])
                y = t if y is None else y + t
            x = x + m[5:6] * y
    h = _rms_mod(x, g_ref[...], m[3 * j:3 * j + 1], m[3 * j + 1:3 * j + 2]).astype(BF16)
    a = _dot(h, w1a_ref[...])
    b = _dot(h, w1b_ref[...])
    act = (a * _sigmoid(a) * b).astype(BF16)
    y = x + (0.5 * m[3 * j + 2:3 * j + 3]) * _dot(act, w2_ref[...])
    if final_norm:
        y = y * lax.rsqrt(jnp.mean(y * y, axis=-1, keepdims=True) + EPS) * gf_ref[...]
    o_ref[...] = y


def _half_ffn(x, mod, layer, seq_len, j, half, g, w1, w2, gf, final_norm, mixed=None, scan=None):
    n, d = x.shape
    tm = TOKEN_TILE
    once = pl.Buffered(1)
    acts, w_out = mixed if mixed is not None else ((), None)
    width = acts[0].shape[1] if acts else 0
    mixed_specs = ([pl.BlockSpec((tm, width), lambda i: (i, 0)) for _ in acts]
                   + [pl.BlockSpec((width, d), functools.partial(lambda i, blk: (blk, 0), blk=blk),
                                   pipeline_mode=once) for blk in range(len(acts))])
    mixed_args = [*acts, *([w_out] * len(acts))]
    if scan is not None:
        o_fwd, o_bwd, gate, head_g, w_scan = scan
        mixed_specs = ([pl.BlockSpec((tm, d), lambda i: (i, 0)) for _ in range(3)]
                       + [_resident((1, HG_DV)), _resident(w_scan.shape)])
        mixed_args = [o_fwd, o_bwd, gate, head_g.reshape(1, HG_DV), w_scan]
    return pl.pallas_call(
        functools.partial(_ffn_kernel, j, final_norm, len(acts), scan is not None),
        grid=(n // tm,),
        in_specs=[
            pl.BlockSpec((tm, d), lambda i: (i, 0)),
            _mod_spec(layer, seq_len, tm),
            _resident((1, d)),
            pl.BlockSpec((None, None, d, D_FF), lambda i: (layer, half, 0, 0), pipeline_mode=once),
            pl.BlockSpec((None, None, d, D_FF), lambda i: (layer, half, 0, 1), pipeline_mode=once),
            pl.BlockSpec((None, None, D_FF, d), lambda i: (layer, half, 0, 0), pipeline_mode=once),
            _resident((1, d)),
        ] + mixed_specs,
        out_specs=pl.BlockSpec((tm, d), lambda i: (i, 0)),
        out_shape=jax.ShapeDtypeStruct((n, d), F32),
        compiler_params=_params(1),
        name="half_ffn",
    )(x, mod, g.reshape(1, d), w1, w1, w2, gf.reshape(1, d), *mixed_args)


def _proj_kernel(j, splits, x_ref, mod_ref, g_ref, w_ref, *o_refs):
    m = mod_ref[0, 0]
    h = _rms_mod(x_ref[...], g_ref[...], m[3 * j:3 * j + 1], m[3 * j + 1:3 * j + 2]).astype(BF16)
    u = _dot(h, w_ref[...])
    for (lo, hi, scale), o_ref in zip(splits, o_refs):
        piece = u[:, lo:hi]
        if scale != 1.0:
            piece = piece * scale
        o_ref[...] = piece.astype(o_ref.dtype)


def _in_proj(x, mod, layer, seq_len, g, w, outs):
    n, d = x.shape
    tm = PROJ_TILE
    splits, lo = [], 0
    for width, _, scale in outs:
        splits.append((lo, lo + width, scale))
        lo += width
    return pl.pallas_call(
        functools.partial(_proj_kernel, 1, tuple(splits)),
        grid=(n // tm,),
        in_specs=[
            pl.BlockSpec((tm, d), lambda i: (i, 0)),
            _mod_spec(layer, seq_len, tm),
            _resident((1, d)),
            _resident(w.shape),
        ],
        out_specs=[pl.BlockSpec((tm, width), lambda i: (i, 0)) for width, _, _ in outs],
        out_shape=[jax.ShapeDtypeStruct((n, width), dt) for width, dt, _ in outs],
        compiler_params=_params(1),
        name="mixer_in_proj",
    )(x, mod, g.reshape(1, d), w)


def _forget_gates(z, lb):
    e = jnp.exp(-jnp.abs(z))
    inv = 1.0 / (1.0 + e)
    pos_z = z >= 0
    sig = jnp.where(pos_z, inv, e * inv)
    sig_neg = jnp.where(pos_z, e * inv, inv)
    return (1.0 - lb) * sig_neg, jnp.log2(lb + (1.0 - lb) * sig)


def _head_pair_attention(pairs):
    lane = lax.broadcasted_iota(jnp.int32, (1, 2 * NA_HD), 1)
    first = lane < NA_HD
    scores = []
    for q, key_blocks, _, bias_blocks in pairs:
        zero = jnp.zeros_like(q)
        qm = jnp.concatenate([jnp.where(first, q, zero), jnp.where(first, zero, q)], axis=0)
        ss = []
        for kb, bias in zip(key_blocks, bias_blocks):
            s = _dot_nt(qm, kb)
            ss.append(s if bias is None else s + bias)
        scores.append(ss)
    maxes = []
    for ss in scores:
        mx = ss[0].max(axis=-1, keepdims=True)
        for s in ss[1:]:
            mx = jnp.maximum(mx, s.max(axis=-1, keepdims=True))
        maxes.append(mx)
    outs = []
    for (q, _, value_blocks, _), ss, mx in zip(pairs, scores, maxes):
        den, acc = None, None
        for s, vb in zip(ss, value_blocks):
            p = jnp.exp(s - mx)
            dsum = p.sum(axis=-1, keepdims=True)
            pv = _dot(p.astype(BF16), vb)
            den = dsum if den is None else den + dsum
            acc = pv if acc is None else acc + pv
        o = acc / den
        tq = q.shape[0]
        outs.append(jnp.where(first, o[:tq], o[tq:]))
    return outs


def _ctx_attn_kernel(q_ref, k_ref, v_ref, o_ref):
    dests = [(i, slice(pr * 2 * NA_HD, (pr + 1) * 2 * NA_HD))
             for i in range(CTX_SEQS_PER_STEP) for pr in range(NA_HEADS // 2)]
    outs = _head_pair_attention([(q_ref[i, :, sl], [k_ref[i, :, sl].astype(BF16)],
                                  [v_ref[i, :, sl].astype(BF16)], [None]) for i, sl in dests])
    for (i, sl), o in zip(dests, outs):
        o_ref[i, :, sl] = o.astype(o_ref.dtype)


def _context_attention(q, k, v):
    b, s, w = q.shape
    blk = lambda: pl.BlockSpec((CTX_SEQS_PER_STEP, s, w), lambda i: (i, 0, 0))
    return pl.pallas_call(
        _ctx_attn_kernel,
        grid=(b // CTX_SEQS_PER_STEP,),
        in_specs=[blk(), blk(), blk()],
        out_specs=blk(),
        out_shape=jax.ShapeDtypeStruct((b, s, w), BF16),
        compiler_params=_params(1),
        name="context_attention",
    )(q, k, v)


def _na_kernel(rows, q_ref, k_ref, v_ref, ck_ref, cv_ref, bias_ref, o_ref):
    slices = [slice(pr * 2 * NA_HD, (pr + 1) * 2 * NA_HD) for pr in range(NA_HEADS // 2)]
    pairs, dests = [], []
    for rr in range(NA_ROWS_PER_STEP):
        r = pl.program_id(1) * NA_ROWS_PER_STEP + rr
        r0 = jnp.clip(r - NA_WIN_R // 2, 0, rows - NA_WIN_R)
        start = pl.multiple_of(r0 * GRID_W, GRID_W)
        win = pl.ds(start, NA_WIN_R * GRID_W)
        first = (NA_WIN_R - 1) - (r - r0)
        q_rows = slice(rr * GRID_W, (rr + 1) * GRID_W)
        for pr, sl in enumerate(slices):
            bias = jnp.concatenate(
                [jnp.concatenate([bias_ref[2 * pr + hh, first + 2 * jj] for jj in range(NA_WIN_R // 2)], axis=1)
                 for hh in range(2)], axis=0)
            pairs.append((q_ref[0, q_rows, sl], [k_ref[0, win, sl], ck_ref[0, :, sl]],
                          [v_ref[0, win, sl], cv_ref[0, :, sl]], [bias, None]))
            dests.append((q_rows, sl))
    for (q_rows, sl), o in zip(dests, _head_pair_attention(pairs)):
        o_ref[0, q_rows, sl] = o.astype(o_ref.dtype)


def _na_bias_table(rpb):
    cols = np.arange(GRID_W)
    col_start = np.clip(cols - NA_WIN_C // 2, 0, GRID_W - NA_WIN_C)
    inside = (cols[None, :] >= col_start[:, None]) & (cols[None, :] < col_start[:, None] + NA_WIN_C)
    pad = GRID_W - NA_WIN_C
    period = 2 * GRID_W
    v = jnp.pad(rpb.astype(F32), ((0, 0), (0, 0), (pad, pad + 1)))
    t = jnp.tile(v, (1, 1, GRID_W))[:, :, :GRID_W * (period - 1)]
    t = t.reshape(*rpb.shape[:2], GRID_W, period - 1)[:, :, :, GRID_W - 1:]
    t = jnp.where(inside[None, None], t, MASK_VALUE)
    return jnp.concatenate([t[:, :-1], t[:, 1:]], axis=-1)


def _neighbourhood_attention(q, k, v, ck, cv, bias):
    b, t, w = q.shape
    rows = t // GRID_W
    p = ck.shape[1]
    seq = lambda n: pl.BlockSpec((1, n, w), lambda i, r: (i, 0, 0))
    row = lambda: pl.BlockSpec((1, NA_ROWS_PER_STEP * GRID_W, w), lambda i, r: (i, r, 0))
    return pl.pallas_call(
        functools.partial(_na_kernel, rows),
        grid=(b, rows // NA_ROWS_PER_STEP),
        in_specs=[row(), seq(t), seq(t), seq(p), seq(p), _resident(bias.shape)],
        out_specs=row(),
        out_shape=jax.ShapeDtypeStruct((b, t, w), BF16),
        compiler_params=_params(2),
        name="neighbourhood_attention",
    )(q, k, v, ck, cv, bias)


def _dft_tables(t):
    def cs(n):
        idx = (np.arange(n)[:, None] * np.arange(n)[None, :]) % n
        ang = 2.0 * np.pi * idx.astype(np.float64) / n
        return np.cos(ang), np.sin(ang)
    ct, st = cs(t)
    cg, sg = cs(FN_GW)
    return (np.concatenate([ct, -st], axis=1).astype(np.float32),
            np.concatenate([cg, sg], axis=1).astype(np.float32))


def _fnet_kernel(t, seqs, scale, f_ref, cgsg_ref, dft_ref, o_ref, xcs_ref):
    first = pl.program_id(1) * seqs

    @pl.when(pl.program_id(0) == 0)
    def _():
        for s in range(seqs):
            for g in range(FN_GROUPS):
                sl = slice(g * FN_GW, (g + 1) * FN_GW)
                xcs = _dot(f_ref[s, :, sl], cgsg_ref[...])
                xcs_ref[first + s, 0:t, sl] = xcs[:, :FN_GW].astype(BF16)
                xcs_ref[first + s, t:2 * t, sl] = xcs[:, FN_GW:].astype(BF16)

    for s in range(seqs):
        o_ref[s] = (_dot(dft_ref[...], xcs_ref[first + s]) * scale).astype(o_ref.dtype)


def _fourier_mix(f, dft, cgsg):
    b, t, w = f.shape
    tq = min(t, TOKEN_TILE)
    seqs = max(1, min(b, FN_ROWS_PER_STEP // t))
    groups = b // seqs
    scale = float(1.0 / np.sqrt(float(t * FN_GW)))
    return pl.pallas_call(
        functools.partial(_fnet_kernel, t, seqs, scale),
        grid=(t // tq, groups),
        in_specs=[
            pl.BlockSpec((seqs, t, w), lambda r, i: (jnp.where(r == 0, i, groups - 1), 0, 0)),
            pl.BlockSpec(cgsg.shape, lambda r, i: (0, 0)),
            pl.BlockSpec((tq, 2 * t), lambda r, i: (r, 0)),
        ],
        out_specs=pl.BlockSpec((seqs, tq, w), lambda r, i: (i, r, 0)),
        out_shape=jax.ShapeDtypeStruct((b, t, w), BF16),
        scratch_shapes=[pltpu.VMEM((b, 2 * t, w), BF16)],
        compiler_params=_params(2),
        name="fourier_mix",
    )(f, cgsg, dft)


def _ref_rows(b, b_ref, m, reverse):
    c = SCAN_CHUNK
    blk = 2 * m
    anchor = m if reverse else m - 1
    if blk >= 8:
        pieces = [jnp.broadcast_to(b_ref[p * blk + anchor:p * blk + anchor + 1, :], (blk, HG_DK))
                  for p in range(c // blk)]
        return pieces[0] if len(pieces) == 1 else jnp.concatenate(pieces, axis=0)
    pos = lax.broadcasted_iota(jnp.int32, (c, HG_DK), 0) % blk
    r = b
    for p in range(blk):
        delta = anchor - p
        if delta != 0:
            r = jnp.where(pos == p, pltpu.roll(b, (-delta) % c, 0), r)
    return r


def _level_masks():
    c = SCAN_CHUNK
    t = np.arange(c)
    left, right = [], []
    for m in SMALL_LEVELS:
        is_right = (t // m) % 2 == 1
        left.append(np.where(is_right, NEG_BIG, 0.0))
        right.append(np.where(is_right, 0.0, NEG_BIG))
    full = lambda rows: np.ascontiguousarray(
        np.broadcast_to(np.stack(rows)[:, :, None], (len(rows), c, HG_DK))).astype(np.float32)
    return full(left), full(right)


def _tri_tables():
    t = np.arange(SCAN_CHUNK)
    return np.stack([t[None, :] <= t[:, None], t[None, :] >= t[:, None]]).astype(np.float32)


def _scan_gates(z, lb, tri):
    k, log2_f = _forget_gates(z, lb)
    b = None
    for part in _split3(log2_f):
        t = _dot(tri, part)
        b = t if b is None else b + t
    return k, b, jnp.min(log2_f)


def _scan_chunks(chains, ml_ref, mr_ref, fast):
    c = SCAN_CHUNK
    n_tiles = c // SUBLANES
    tile = lambda x, i: x[i * SUBLANES:(i + 1) * SUBLANES]
    row = lax.broadcasted_iota(jnp.int32, (c, c), 0)
    col = lax.broadcasted_iota(jnp.int32, (c, c), 1)
    xr = row ^ col
    zero_tile = jnp.zeros((SUBLANES, HG_DK), F32)

    os_, b_tots = [], []
    for reverse, q, v, k, b, st, b_ref in chains:
        edge = 0 if reverse else c - 1
        b_tots.append(b_ref[edge:edge + 1, :])
        os_.append(_dot_nt((q * jnp.exp2(b)).astype(BF16), st.astype(BF16)))

    a_tiles = [[None] * n_tiles for _ in chains]
    m = c // 2
    while m >= (FAST_BLOCK if fast else SUBLANES):
        g = m // SUBLANES
        for ci, (reverse, q, v, k, b, st, b_ref) in enumerate(chains):
            q_ids, qe_tiles, ke_tiles = [], [], [zero_tile] * n_tiles
            for p in range(n_tiles // (2 * g)):
                left = range(2 * g * p, 2 * g * p + g)
                right = range(2 * g * p + g, 2 * g * (p + 1))
                anchor = (2 * g * p + g) * SUBLANES - (0 if reverse else 1)
                r = jnp.broadcast_to(b_ref[anchor:anchor + 1, :], (SUBLANES, HG_DK))
                q_side, k_side = (left, right) if reverse else (right, left)
                for i in q_side:
                    q_ids.append(i)
                    qe_tiles.append(tile(q, i) * jnp.exp2(tile(b, i) - r))
                for i in k_side:
                    ke_tiles[i] = tile(k, i) * jnp.exp2(r - tile(b, i))
            lvl = _dot_nt(jnp.concatenate(qe_tiles, axis=0).astype(BF16),
                          jnp.concatenate(ke_tiles, axis=0).astype(BF16))
            for j, i in enumerate(q_ids):
                old = a_tiles[ci][i]
                a_tiles[ci][i] = tile(lvl, j) if old is None else jnp.where(tile(xr, i) < 2 * m, tile(lvl, j), old)
        m //= 2
    a_s = [jnp.concatenate([t if t is not None else zero_tile for t in tiles], axis=0) for tiles in a_tiles]

    if fast:
        for ci, (reverse, q, v, k, b, st, b_ref) in enumerate(chains):
            mid = FAST_BLOCK // 2
            r = jnp.concatenate(
                [jnp.broadcast_to(b_ref[j * FAST_BLOCK + mid:j * FAST_BLOCK + mid + 1, :], (FAST_BLOCK, HG_DK))
                 for j in range(c // FAST_BLOCK)], axis=0)
            qe = (q * jnp.exp2(b - r)).astype(BF16)
            ke = (k * jnp.exp2(r - b)).astype(BF16)
            a_s[ci] = jnp.where(xr < FAST_BLOCK, _dot_nt(qe, ke), a_s[ci])
    else:
        for level, m in enumerate(SMALL_LEVELS):
            for ci, (reverse, q, v, k, b, st, b_ref) in enumerate(chains):
                d = b - _ref_rows(b, b_ref, m, reverse)
                mask_q, mask_k = (ml_ref, mr_ref) if reverse else (mr_ref, ml_ref)
                qe = (q * jnp.exp2(d + mask_q[level])).astype(BF16)
                ke = (k * jnp.exp2(mask_k[level] - d)).astype(BF16)
                a_s[ci] = jnp.where(xr < 2 * m, _dot_nt(qe, ke), a_s[ci])

    out = []
    for (reverse, q, v, k, b, st, b_ref), a, o, b_tot in zip(chains, a_s, os_, b_tots):
        if fast:
            a = jnp.where((col >= row) if reverse else (col <= row), a, 0.0)
        else:
            diag = jnp.where(xr == 0, _dot_nt(q.astype(BF16), k.astype(BF16)), 0.0)
            a = jnp.where((col > row) if reverse else (col < row), a, diag)
        o = o + _dot(a.astype(BF16), v.astype(BF16))
        k_end = (k * jnp.exp2(b_tot - b)).astype(BF16)
        st_new = st * jnp.exp2(b_tot) + _dot(v.astype(F32).T.astype(BF16), k_end)
        out.append((o, st_new))
    return out


def _hgrn_kernel(layer, n_chunks, n_heads, split_views, has_s0, *refs):
    refs = list(refs)
    if split_views:
        qf_ref, qb_ref, zf_ref, zb_ref, vf_ref, vb_ref = refs[:6]
        del refs[:6]
    else:
        qf_ref, zf_ref, zb_ref, vf_ref = refs[:4]
        qb_ref, vb_ref = qf_ref, vf_ref
        del refs[:4]
    lg_ref, ml_ref, mr_ref, tri_ref = refs[:4]
    del refs[:4]
    s0_ref = refs.pop(0) if has_s0 else None
    o_out = refs[:2]
    sfin_ref = refs[2]
    del refs[:3]
    n_chains = 2 * n_heads
    st_refs, b_refs, k_refs = (refs[i * n_chains:(i + 1) * n_chains] for i in range(3))
    fast_ref = refs[3 * n_chains]
    chains = [(hh, direction) for hh in range(n_heads) for direction in range(2)]
    q_refs, z_refs, v_refs = (qf_ref, qb_ref), (zf_ref, zb_ref), (vf_ref, vb_ref)
    lanes = lambda hh: slice(hh * HG_DK, (hh + 1) * HG_DK)

    @pl.when(pl.program_id(2) == 0)
    def _():
        for ci, (hh, direction) in enumerate(chains):
            if has_s0:
                st_refs[ci][...] = s0_ref[0, direction, hh].T
            else:
                st_refs[ci][...] = jnp.zeros((HG_DV, HG_DK), F32)

    def lower_bound(hh, direction):
        lg = lg_ref[:, direction, hh, 0, :]
        ex = jnp.exp(lg - lg.max(axis=0, keepdims=True))
        return ex[1:layer + 1].sum(axis=0, keepdims=True) / ex.sum(axis=0, keepdims=True)

    lbs = [lower_bound(hh, direction) for hh, direction in chains]

    def chunk_rows(c):
        return pl.ds(pl.multiple_of(c * SCAN_CHUNK, SCAN_CHUNK), SCAN_CHUNK)

    def gates(c_fwd, c_bwd):
        rows = (chunk_rows(c_fwd), chunk_rows(c_bwd))
        return [_scan_gates(z_refs[direction][rows[direction], lanes(hh)], lb, tri_ref[direction])
                for (hh, direction), lb in zip(chains, lbs)]

    def store_gates(gs):
        lo = None
        for (k, b, lo_c), k_ref, b_ref in zip(gs, k_refs, b_refs):
            k_ref[...] = k
            b_ref[...] = b
            lo = lo_c if lo is None else jnp.minimum(lo, lo_c)
        fast_ref[0] = (lo >= -FAST_LOG2_LIMIT).astype(jnp.int32)

    store_gates(gates(0, n_chunks - 1))

    def trip(c, fast, last):
        rows = (chunk_rows(c), chunk_rows(n_chunks - 1 - c))
        nxt = None if last else gates(c + 1, n_chunks - 2 - c)
        outs = _scan_chunks(
            [(direction == 1, q_refs[direction][rows[direction], lanes(hh)],
              v_refs[direction][rows[direction], lanes(hh)],
              k_refs[ci][...], b_refs[ci][...], st_refs[ci][...], b_refs[ci])
             for ci, (hh, direction) in enumerate(chains)],
            ml_ref, mr_ref, fast)
        for ci, ((hh, direction), (o, st)) in enumerate(zip(chains, outs)):
            st_refs[ci][...] = st
            o_out[direction][rows[direction], lanes(hh)] = o
        if not last:
            store_gates(nxt)

    def guarded_trip(c, last):
        use_fast = fast_ref[0] == 1
        pl.when(use_fast)(lambda: trip(c, True, last))
        pl.when(jnp.logical_not(use_fast))(lambda: trip(c, False, last))

    def scan_body(c, carry):
        guarded_trip(c, False)
        return carry

    lax.fori_loop(0, n_chunks - 1, scan_body, 0)
    guarded_trip(n_chunks - 1, True)

    @pl.when(pl.program_id(2) == pl.num_programs(2) - 1)
    def _():
        for ci, (hh, direction) in enumerate(chains):
            sfin_ref[0, direction, hh] = st_refs[ci][...].T


def _hgrn_bidir(layer, seqs, q, zf, zb, v, logits, s0):
    n = q.shape[0]
    t = n // seqs
    tb = min(t, SCAN_TIME_BLOCK)
    n_tb = t // tb
    split = n_tb > 1
    n_chunks = tb // SCAN_CHUNK
    in_bytes = (2 if split else 1) * (4 + 2) + 2 * 4
    bytes_per_elem = in_bytes + 2 * 4
    fits = lambda heads: 2 * tb * heads * HG_DK * bytes_per_elem <= SCAN_WINDOW_BYTES
    n_heads = next((heads for heads in (8, 4, 2) if heads <= SCAN_HEADS_MAX and fits(heads)), 1)
    n_chains = 2 * n_heads
    ml, mr = _level_masks()
    tri = _tri_tables()
    fwd = pl.BlockSpec((tb, n_heads * HG_DK), lambda s, h, j: (s * n_tb + j, h))
    rev = pl.BlockSpec((tb, n_heads * HG_DK), lambda s, h, j: (s * n_tb + n_tb - 1 - j, h))
    state = lambda: pl.BlockSpec((1, 2, n_heads, HG_DK, HG_DV), lambda s, h, j: (s, 0, h, 0, 0))
    tok_specs, tok_args = ([fwd, rev, fwd, rev, fwd, rev], [q, q, zf, zb, v, v]) if split else \
                          ([fwd, fwd, fwd, fwd], [q, zf, zb, v])
    in_specs = tok_specs + [pl.BlockSpec((DEPTH, 2, n_heads, 1, HG_DK), lambda s, h, j: (0, 0, h, 0, 0)),
                            _resident(ml.shape), _resident(mr.shape), _resident(tri.shape)]
    args = tok_args + [logits.reshape(DEPTH, 2, HG_HEADS, 1, HG_DK),
                       jnp.asarray(ml), jnp.asarray(mr), jnp.asarray(tri).astype(BF16)]
    if s0 is not None:
        in_specs.append(state())
        args.append(s0)
    return pl.pallas_call(
        functools.partial(_hgrn_kernel, layer, n_chunks, n_heads, split, s0 is not None),
        grid=(seqs, HG_HEADS // n_heads, n_tb),
        in_specs=in_specs,
        out_specs=[fwd, rev, state()],
        out_shape=[jax.ShapeDtypeStruct((n, HG_KW), F32), jax.ShapeDtypeStruct((n, HG_KW), F32),
                   jax.ShapeDtypeStruct((seqs, 2, HG_HEADS, HG_DK, HG_DV), F32)],
        scratch_shapes=([pltpu.VMEM((HG_DV, HG_DK), F32)] * n_chains
                        + [pltpu.VMEM((SCAN_CHUNK, HG_DK), F32)] * n_chains
                        + [pltpu.VMEM((SCAN_CHUNK, HG_DK), F32)] * n_chains
                        + [pltpu.SMEM((1,), jnp.int32)]),
        compiler_params=_params(3),
        name="hgrn2_bidir",
    )(*args)


def kernel(x_prompt, x_sample, cache_k, cache_v, state_hgrn, c, c_ctx, ada_w, ada_b, norm_g, ffn_w1, ffn_w2,
           mix0_w_in, mix0_w_out, na_rpb, mix1_w_in, mix1_w_out, hg_lb_logits, hg_norm_g, norm_f):
    bp, sp, d = x_prompt.shape
    bs, ts, _ = x_sample.shape
    xp = x_prompt.reshape(bp * sp, d)
    xs = x_sample.reshape(bs * ts, d)

    cond = jnp.zeros((COND_ROWS, d), F32).at[:bs].set(c).at[CTX_ROW].set(c_ctx)
    mod = _ada_mod(cond, ada_w, ada_b)

    w1 = ffn_w1.astype(BF16)
    w2 = ffn_w2.astype(BF16)

    new_k, new_v, new_s = [], [], []
    for l in range(DEPTH):
        last = l == DEPTH - 1

        def ffn(x, seq_len, j, half, final, mixed=None, scan=None):
            return _half_ffn(x, mod, l, seq_len, j, half, norm_g[l, j], w1, w2, norm_f, final, mixed, scan)

        xp = ffn(xp, None, 0, 0, False)
        xs = ffn(xs, ts, 0, 0, False)

        if l % 2 == 0:
            e = l // 2
            w_in = mix0_w_in[e].astype(BF16)
            w_out = mix0_w_out[e].astype(BF16)
            qk_scale = float(NA_HD) ** -0.5
            qp, kp, vp, fp = _in_proj(xp, mod, l, None, norm_g[l, 1], w_in,
                                      [(NA_W, BF16, qk_scale), (NA_W, F32, 1.0), (NA_W, F32, 1.0), (FN_W, BF16, 1.0)])
            qs, ks, vs, fs = _in_proj(xs, mod, l, ts, norm_g[l, 1], w_in,
                                      [(NA_W, BF16, qk_scale), (NA_W, BF16, 1.0), (NA_W, BF16, 1.0), (FN_W, BF16, 1.0)])
            new_k.append(kp.reshape(bp, sp, NA_HEADS, NA_HD))
            new_v.append(vp.reshape(bp, sp, NA_HEADS, NA_HD))

            r3 = lambda a, b_, t_: a.reshape(b_, t_, a.shape[-1])
            ap = _context_attention(r3(qp, bp, sp), r3(kp, bp, sp), r3(vp, bp, sp))
            ck = cache_k[:, e].reshape(bs, -1, NA_W).astype(BF16)
            cv = cache_v[:, e].reshape(bs, -1, NA_W).astype(BF16)
            a_s = _neighbourhood_attention(r3(qs, bs, ts), r3(ks, bs, ts), r3(vs, bs, ts), ck, cv,
                                           _na_bias_table(na_rpb[e]))
            cgsg = None
            fn = []
            for f3 in (r3(fp, bp, sp), r3(fs, bs, ts)):
                dft_np, cgsg_np = _dft_tables(f3.shape[1])
                fn.append(_fourier_mix(f3, jnp.asarray(dft_np).astype(BF16), jnp.asarray(cgsg_np).astype(BF16)))
            mixed_p = ([ap.reshape(bp * sp, NA_W), fn[0].reshape(bp * sp, FN_W)], w_out)
            mixed_s = ([a_s.reshape(bs * ts, NA_W), fn[1].reshape(bs * ts, FN_W)], w_out)
            scan_p = scan_s = None
        else:
            o = l // 2
            w_in = mix1_w_in[o].astype(BF16)
            w_out = mix1_w_out[o].astype(BF16)
            outs = [(HG_KW, F32, 1.0)] * 3 + [(HG_KW, BF16, 1.0), (HG_KW, F32, 1.0)]
            qp, zfp, zbp, vp, gate_p = _in_proj(xp, mod, l, None, norm_g[l, 1], w_in, outs)
            qs, zfs, zbs, vs, gate_s = _in_proj(xs, mod, l, ts, norm_g[l, 1], w_in, outs)
            ofp, obp, s_fin = _hgrn_bidir(l, bp, qp, zfp, zbp, vp, hg_lb_logits, None)
            ofs, obs, _ = _hgrn_bidir(l, bs, qs, zfs, zbs, vs, hg_lb_logits, state_hgrn[:, o])
            new_s.append(s_fin)
            mixed_p = mixed_s = None
            scan_p = (ofp, obp, gate_p, hg_norm_g[o], w_out)
            scan_s = (ofs, obs, gate_s, hg_norm_g[o], w_out)

        xp = ffn(xp, None, 2, 1, last, mixed_p, scan_p)
        xs = ffn(xs, ts, 2, 1, last, mixed_s, scan_s)

    stack = lambda parts: jnp.expand_dims(parts[0], 1) if len(parts) == 1 else jnp.stack(parts, axis=1)
    return (xp.reshape(bp, sp, d), xs.reshape(bs, ts, d), stack(new_k), stack(new_v), stack(new_s))
```

```python
import functools

import numpy as np
import jax
import jax.numpy as jnp
from jax import lax
from jax.experimental import pallas as pl
from jax.experimental.pallas import tpu as pltpu

F32 = jnp.float32
BF16 = jnp.bfloat16

D_MODEL = 1024
DEPTH = 2
GRID_W = 64
NA_HEADS = 8
NA_HD = 64
NA_W = NA_HEADS * NA_HD
NA_WIN_R = 8
NA_WIN_C = 16
CTX_SEQS_PER_STEP = 2
NA_ROWS_PER_STEP = 4
FN_GROUPS = 4
FN_GW = 128
FN_W = FN_GROUPS * FN_GW
FN_ROWS_PER_STEP = 1024
HG_HEADS = 8
HG_DK = 128
HG_DV = 128
HG_KW = HG_HEADS * HG_DK
D_FF = 2816
N_MOD = 9
EPS = 1e-6

COND_ROWS = 8
CTX_ROW = 4
TOKEN_TILE = 512
PROJ_TILE = 512
SCAN_CHUNK = 128
SUBLANES = 8
SMALL_LEVELS = (4, 2, 1)
SCAN_HEADS_MAX = 8
SCAN_TIME_BLOCK = 1024
SCAN_WINDOW_BYTES = 40 * 1024 * 1024
FAST_BLOCK = 32
FAST_LOG2_LIMIT = 125.0 / (FAST_BLOCK // 2)
MASK_VALUE = -1e30
NEG_BIG = -1e30
VMEM_LIMIT = 56 * 1024 * 1024


def _params(n_axes, vmem=VMEM_LIMIT):
    return pltpu.CompilerParams(dimension_semantics=("arbitrary",) * n_axes, vmem_limit_bytes=vmem)


def _resident(shape):
    nd = len(shape)
    return pl.BlockSpec(shape, lambda *_: (0,) * nd, pipeline_mode=pl.Buffered(1))


def _sigmoid(a):
    return 1.0 / (1.0 + jnp.exp(-a))


def _rms_mod(x, g, shift, scale):
    y = x * lax.rsqrt(jnp.mean(x * x, axis=-1, keepdims=True) + EPS)
    return (y * g) * (1.0 + scale) + shift


def _split2(x):
    hi = x.astype(BF16)
    lo = (x - hi.astype(F32)).astype(BF16)
    return hi, lo


def _dot(a, b):
    return jnp.dot(a, b, preferred_element_type=F32)


def _dot_nt(a, b):
    return lax.dot_general(a, b, (((1,), (1,)), ((), ())), preferred_element_type=F32)


def _ada_kernel(cond_ref, w_ref, b_ref, o_ref):
    c = cond_ref[...]
    s = c * _sigmoid(c)
    s_hi, s_lo = _split2(s)
    w_hi, w_lo = _split2(w_ref[0])
    o_ref[0] = _dot(s_hi, w_hi) + _dot(s_hi, w_lo) + _dot(s_lo, w_hi) + b_ref[0]


def _ada_mod(cond, ada_w, ada_b):
    depth, d, n = ada_w.shape
    tn = 1152
    out = pl.pallas_call(
        _ada_kernel,
        grid=(depth, n // tn),
        in_specs=[
            pl.BlockSpec((COND_ROWS, d), lambda l, j: (0, 0)),
            pl.BlockSpec((1, d, tn), lambda l, j: (l, 0, j)),
            pl.BlockSpec((1, 1, tn), lambda l, j: (l, 0, j)),
        ],
        out_specs=pl.BlockSpec((1, COND_ROWS, tn), lambda l, j: (l, 0, j)),
        out_shape=jax.ShapeDtypeStruct((depth, COND_ROWS, n), F32),
        compiler_params=_params(2),
        name="ada_mod",
    )(cond, ada_w, ada_b.reshape(depth, 1, n))
    return out.reshape(depth, COND_ROWS, N_MOD, d)


def _mod_spec(layer, seq_len, tm):
    if seq_len is None:
        return pl.BlockSpec((1, 1, N_MOD, D_MODEL), lambda i: (layer, CTX_ROW, 0, 0))
    tiles_per_seq = seq_len // tm
    return pl.BlockSpec((1, 1, N_MOD, D_MODEL), lambda i: (layer, i // tiles_per_seq, 0, 0))


def _ffn_kernel(j, final_norm, n_acts, scan_mix, x_ref, mod_ref, g_ref, w1a_ref, w1b_ref, w2_ref, gf_ref, *refs):
    x = x_ref[...]
    m = mod_ref[0, 0]
    if scan_mix:
        of_ref, ob_ref, gate_ref, ng_ref, w_ref, o_ref = refs
        o = of_ref[...] + ob_ref[...]
        ng = ng_ref[...]
        heads = []
        for hh in range(HG_HEADS):
            oh = o[:, hh * HG_DV:(hh + 1) * HG_DV]
            heads.append(oh * lax.rsqrt(jnp.mean(oh * oh, axis=-1, keepdims=True) + EPS) * ng)
        gate = gate_ref[...]
        y = (jnp.concatenate(heads, axis=1) * (gate * _sigmoid(gate))).astype(BF16)
        x = x + m[5:6] * _dot(y, w_ref[...])
    else:
        a_refs, w_refs, o_ref = refs[:n_acts], refs[n_acts:2 * n_acts], refs[2 * n_acts]
        if n_acts:
            y = None
            for a_ref, w_ref in zip(a_refs, w_refs):
                t = _dot(a_ref[...], w_ref[...])
                y = t if y is None else y + t
            x = x + m[5:6] * y
    h = _rms_mod(x, g_ref[...], m[3 * j:3 * j + 1], m[3 * j + 1:3 * j + 2]).astype(BF16)
    a = _dot(h, w1a_ref[...])
    b = _dot(h, w1b_ref[...])
    act = (a * _sigmoid(a) * b).astype(BF16)
    y = x + (0.5 * m[3 * j + 2:3 * j + 3]) * _dot(act, w2_ref[...])
    if final_norm:
        y = y * lax.rsqrt(jnp.mean(y * y, axis=-1, keepdims=True) + EPS) * gf_ref[...]
    o_ref[...] = y


def _half_ffn(x, mod, layer, seq_len, j, half, g, w1, w2, gf, final_norm, mixed=None, scan=None):
    n, d = x.shape
    tm = TOKEN_TILE
    once = pl.Buffered(1)
    acts, w_out = mixed if mixed is not None else ((), None)
    width = acts[0].shape[1] if acts else 0
    mixed_specs = ([pl.BlockSpec((tm, width), lambda i: (i, 0)) for _ in acts]
                   + [pl.BlockSpec((width, d), functools.partial(lambda i, blk: (blk, 0), blk=blk),
                                   pipeline_mode=once) for blk in range(len(acts))])
    mixed_args = [*acts, *([w_out] * len(acts))]
    if scan is not None:
        o_fwd, o_bwd, gate, head_g, w_scan = scan
        mixed_specs = ([pl.BlockSpec((tm, d), lambda i: (i, 0)) for _ in range(3)]
                       + [_resident((1, HG_DV)), _resident(w_scan.shape)])
        mixed_args = [o_fwd, o_bwd, gate, head_g.reshape(1, HG_DV), w_scan]
    return pl.pallas_call(
        functools.partial(_ffn_kernel, j, final_norm, len(acts), scan is not None),
        grid=(n // tm,),
        in_specs=[
            pl.BlockSpec((tm, d), lambda i: (i, 0)),
            _mod_spec(layer, seq_len, tm),
            _resident((1, d)),
            pl.BlockSpec((None, None, d, D_FF), lambda i: (layer, half, 0, 0), pipeline_mode=once),
            pl.BlockSpec((None, None, d, D_FF), lambda i: (layer, half, 0, 1), pipeline_mode=once),
            pl.BlockSpec((None, None, D_FF, d), lambda i: (layer, half, 0, 0), pipeline_mode=once),
            _resident((1, d)),
        ] + mixed_specs,
        out_specs=pl.BlockSpec((tm, d), lambda i: (i, 0)),
        out_shape=jax.ShapeDtypeStruct((n, d), F32),
        compiler_params=_params(1),
        name="half_ffn",
    )(x, mod, g.reshape(1, d), w1, w1, w2, gf.reshape(1, d), *mixed_args)


def _proj_kernel(j, splits, x_ref, mod_ref, g_ref, w_ref, *o_refs):
    m = mod_ref[0, 0]
    h = _rms_mod(x_ref[...], g_ref[...], m[3 * j:3 * j + 1], m[3 * j + 1:3 * j + 2]).astype(BF16)
    u = _dot(h, w_ref[...])
    for (lo, hi, scale), o_ref in zip(splits, o_refs):
        piece = u[:, lo:hi]
        if scale != 1.0:
            piece = piece * scale
        o_ref[...] = piece.astype(o_ref.dtype)


def _in_proj(x, mod, layer, seq_len, g, w, outs):
    n, d = x.shape
    tm = PROJ_TILE
    splits, lo = [], 0
    for width, _, scale in outs:
        splits.append((lo, lo + width, scale))
        lo += width
    return pl.pallas_call(
        functools.partial(_proj_kernel, 1, tuple(splits)),
        grid=(n // tm,),
        in_specs=[
            pl.BlockSpec((tm, d), lambda i: (i, 0)),
            _mod_spec(layer, seq_len, tm),
            _resident((1, d)),
            _resident(w.shape),
        ],
        out_specs=[pl.BlockSpec((tm, width), lambda i: (i, 0)) for width, _, _ in outs],
        out_shape=[jax.ShapeDtypeStruct((n, width), dt) for width, dt, _ in outs],
        compiler_params=_params(1),
        name="mixer_in_proj",
    )(x, mod, g.reshape(1, d), w)


def _forget_gates(z, lb):
    g = (1.0 - lb) * _sigmoid(z)
    return (1.0 - lb) - g, jnp.log2(lb + g)


def _head_pair_attention(pairs):
    lane = lax.broadcasted_iota(jnp.int32, (1, 2 * NA_HD), 1)
    first = lane < NA_HD
    scores = []
    for q, key_blocks, _, bias_blocks in pairs:
        zero = jnp.zeros_like(q)
        qm = jnp.concatenate([jnp.where(first, q, zero), jnp.where(first, zero, q)], axis=0)
        ss = []
        for kb, bias in zip(key_blocks, bias_blocks):
            s = _dot_nt(qm, kb)
            ss.append(s if bias is None else s + bias)
        scores.append(ss)
    maxes = []
    for ss in scores:
        mx = ss[0].max(axis=-1, keepdims=True)
        for s in ss[1:]:
            mx = jnp.maximum(mx, s.max(axis=-1, keepdims=True))
        maxes.append(mx)
    outs = []
    for (q, _, value_blocks, _), ss, mx in zip(pairs, scores, maxes):
        den, acc = None, None
        for s, vb in zip(ss, value_blocks):
            p = jnp.exp(s - mx)
            dsum = p.sum(axis=-1, keepdims=True)
            pv = _dot(p.astype(BF16), vb)
            den = dsum if den is None else den + dsum
            acc = pv if acc is None else acc + pv
        o = acc / den
        tq = q.shape[0]
        outs.append(jnp.where(first, o[:tq], o[tq:]))
    return outs


def _ctx_attn_kernel(q_ref, k_ref, v_ref, o_ref):
    dests = [(i, slice(pr * 2 * NA_HD, (pr + 1) * 2 * NA_HD))
             for i in range(CTX_SEQS_PER_STEP) for pr in range(NA_HEADS // 2)]
    outs = _head_pair_attention([(q_ref[i, :, sl], [k_ref[i, :, sl].astype(BF16)],
                                  [v_ref[i, :, sl].astype(BF16)], [None]) for i, sl in dests])
    for (i, sl), o in zip(dests, outs):
        o_ref[i, :, sl] = o.astype(o_ref.dtype)


def _context_attention(q, k, v):
    b, s, w = q.shape
    blk = lambda: pl.BlockSpec((CTX_SEQS_PER_STEP, s, w), lambda i: (i, 0, 0))
    return pl.pallas_call(
        _ctx_attn_kernel,
        grid=(b // CTX_SEQS_PER_STEP,),
        in_specs=[blk(), blk(), blk()],
        out_specs=blk(),
        out_shape=jax.ShapeDtypeStruct((b, s, w), BF16),
        compiler_params=_params(1),
        name="context_attention",
    )(q, k, v)


def _na_kernel(rows, q_ref, k_ref, v_ref, ck_ref, cv_ref, bias_ref, o_ref):
    slices = [slice(pr * 2 * NA_HD, (pr + 1) * 2 * NA_HD) for pr in range(NA_HEADS // 2)]
    pairs, dests = [], []
    for rr in range(NA_ROWS_PER_STEP):
        r = pl.program_id(1) * NA_ROWS_PER_STEP + rr
        r0 = jnp.clip(r - NA_WIN_R // 2, 0, rows - NA_WIN_R)
        start = pl.multiple_of(r0 * GRID_W, GRID_W)
        win = pl.ds(start, NA_WIN_R * GRID_W)
        first = (NA_WIN_R - 1) - (r - r0)
        q_rows = slice(rr * GRID_W, (rr + 1) * GRID_W)
        for pr, sl in enumerate(slices):
            bias = jnp.concatenate(
                [jnp.concatenate([bias_ref[2 * pr + hh, first + 2 * jj] for jj in range(NA_WIN_R // 2)], axis=1)
                 for hh in range(2)], axis=0)
            pairs.append((q_ref[0, q_rows, sl], [k_ref[0, win, sl], ck_ref[0, :, sl]],
                          [v_ref[0, win, sl], cv_ref[0, :, sl]], [bias, None]))
            dests.append((q_rows, sl))
    for (q_rows, sl), o in zip(dests, _head_pair_attention(pairs)):
        o_ref[0, q_rows, sl] = o.astype(o_ref.dtype)


def _na_bias_table(rpb):
    cols = np.arange(GRID_W)
    col_start = np.clip(cols - NA_WIN_C // 2, 0, GRID_W - NA_WIN_C)
    inside = (cols[None, :] >= col_start[:, None]) & (cols[None, :] < col_start[:, None] + NA_WIN_C)
    pad = GRID_W - NA_WIN_C
    period = 2 * GRID_W
    v = jnp.pad(rpb.astype(F32), ((0, 0), (0, 0), (pad, pad + 1)))
    t = jnp.tile(v, (1, 1, GRID_W))[:, :, :GRID_W * (period - 1)]
    t = t.reshape(*rpb.shape[:2], GRID_W, period - 1)[:, :, :, GRID_W - 1:]
    t = jnp.where(inside[None, None], t, MASK_VALUE)
    return jnp.concatenate([t[:, :-1], t[:, 1:]], axis=-1)


def _neighbourhood_attention(q, k, v, ck, cv, bias):
    b, t, w = q.shape
    rows = t // GRID_W
    p = ck.shape[1]
    seq = lambda n: pl.BlockSpec((1, n, w), lambda i, r: (i, 0, 0))
    row = lambda: pl.BlockSpec((1, NA_ROWS_PER_STEP * GRID_W, w), lambda i, r: (i, r, 0))
    return pl.pallas_call(
        functools.partial(_na_kernel, rows),
        grid=(b, rows // NA_ROWS_PER_STEP),
        in_specs=[row(), seq(t), seq(t), seq(p), seq(p), _resident(bias.shape)],
        out_specs=row(),
        out_shape=jax.ShapeDtypeStruct((b, t, w), BF16),
        compiler_params=_params(2),
        name="neighbourhood_attention",
    )(q, k, v, ck, cv, bias)


def _dft_tables(t):
    def cs(n):
        idx = (np.arange(n)[:, None] * np.arange(n)[None, :]) % n
        ang = 2.0 * np.pi * idx.astype(np.float64) / n
        return np.cos(ang), np.sin(ang)
    ct, st = cs(t)
    cg, sg = cs(FN_GW)
    return (np.concatenate([ct, -st], axis=1).astype(np.float32),
            np.concatenate([cg, sg], axis=1).astype(np.float32))


def _fnet_kernel(t, seqs, scale, f_ref, cgsg_ref, dft_ref, o_ref, xcs_ref):
    first = pl.program_id(1) * seqs

    @pl.when(pl.program_id(0) == 0)
    def _():
        for s in range(seqs):
            for g in range(FN_GROUPS):
                sl = slice(g * FN_GW, (g + 1) * FN_GW)
                xcs = _dot(f_ref[s, :, sl], cgsg_ref[...])
                xcs_ref[first + s, 0:t, sl] = xcs[:, :FN_GW].astype(BF16)
                xcs_ref[first + s, t:2 * t, sl] = xcs[:, FN_GW:].astype(BF16)

    for s in range(seqs):
        o_ref[s] = (_dot(dft_ref[...], xcs_ref[first + s]) * scale).astype(o_ref.dtype)


def _fourier_mix(f, dft, cgsg):
    b, t, w = f.shape
    tq = min(t, TOKEN_TILE)
    seqs = max(1, min(b, FN_ROWS_PER_STEP // t))
    groups = b // seqs
    scale = float(1.0 / np.sqrt(float(t * FN_GW)))
    return pl.pallas_call(
        functools.partial(_fnet_kernel, t, seqs, scale),
        grid=(t // tq, groups),
        in_specs=[
            pl.BlockSpec((seqs, t, w), lambda r, i: (jnp.where(r == 0, i, groups - 1), 0, 0)),
            pl.BlockSpec(cgsg.shape, lambda r, i: (0, 0)),
            pl.BlockSpec((tq, 2 * t), lambda r, i: (r, 0)),
        ],
        out_specs=pl.BlockSpec((seqs, tq, w), lambda r, i: (i, r, 0)),
        out_shape=jax.ShapeDtypeStruct((b, t, w), BF16),
        scratch_shapes=[pltpu.VMEM((b, 2 * t, w), BF16)],
        compiler_params=_params(2),
        name="fourier_mix",
    )(f, cgsg, dft)


def _ref_rows(b, b_ref, m, reverse):
    c = SCAN_CHUNK
    blk = 2 * m
    anchor = m if reverse else m - 1
    if blk >= 8:
        pieces = [jnp.broadcast_to(b_ref[p * blk + anchor:p * blk + anchor + 1, :], (blk, HG_DK))
                  for p in range(c // blk)]
        return pieces[0] if len(pieces) == 1 else jnp.concatenate(pieces, axis=0)
    pos = lax.broadcasted_iota(jnp.int32, (c, HG_DK), 0) % blk
    r = b
    for p in range(blk):
        delta = anchor - p
        if delta != 0:
            r = jnp.where(pos == p, pltpu.roll(b, (-delta) % c, 0), r)
    return r


def _level_masks():
    c = SCAN_CHUNK
    t = np.arange(c)
    left, right = [], []
    for m in SMALL_LEVELS:
        is_right = (t // m) % 2 == 1
        left.append(np.where(is_right, NEG_BIG, 0.0))
        right.append(np.where(is_right, 0.0, NEG_BIG))
    full = lambda rows: np.ascontiguousarray(
        np.broadcast_to(np.stack(rows)[:, :, None], (len(rows), c, HG_DK))).astype(np.float32)
    return full(left), full(right)


def _tri_tables():
    t = np.arange(SCAN_CHUNK)
    return np.stack([t[None, :] <= t[:, None], t[None, :] >= t[:, None]]).astype(np.float32)


def _scan_gates(z, lb, tri):
    k, log2_f = _forget_gates(z, lb)
    b = None
    for part in _split2(log2_f):
        t = _dot(tri, part)
        b = t if b is None else b + t
    return k, b, jnp.min(log2_f)


def _scan_chunks(chains, ml_ref, mr_ref, fast):
    c = SCAN_CHUNK
    n_tiles = c // SUBLANES
    tile = lambda x, i: x[i * SUBLANES:(i + 1) * SUBLANES]
    row = lax.broadcasted_iota(jnp.int32, (c, c), 0)
    col = lax.broadcasted_iota(jnp.int32, (c, c), 1)
    xr = row ^ col
    zero_tile = jnp.zeros((SUBLANES, HG_DK), F32)

    os_, b_tots = [], []
    for reverse, q, v, k, b, st, b_ref in chains:
        edge = 0 if reverse else c - 1
        b_tots.append(b_ref[edge:edge + 1, :])
        os_.append(_dot_nt((q * jnp.exp2(b)).astype(BF16), st.astype(BF16)))

    a_tiles = [[None] * n_tiles for _ in chains]
    m = c // 2
    while m >= (FAST_BLOCK if fast else SUBLANES):
        g = m // SUBLANES
        for ci, (reverse, q, v, k, b, st, b_ref) in enumerate(chains):
            q_ids, qe_tiles, ke_tiles = [], [], [zero_tile] * n_tiles
            for p in range(n_tiles // (2 * g)):
                left = range(2 * g * p, 2 * g * p + g)
                right = range(2 * g * p + g, 2 * g * (p + 1))
                anchor = (2 * g * p + g) * SUBLANES - (0 if reverse else 1)
                r = jnp.broadcast_to(b_ref[anchor:anchor + 1, :], (SUBLANES, HG_DK))
                q_side, k_side = (left, right) if reverse else (right, left)
                for i in q_side:
                    q_ids.append(i)
                    qe_tiles.append(tile(q, i) * jnp.exp2(tile(b, i) - r))
                for i in k_side:
                    ke_tiles[i] = tile(k, i) * jnp.exp2(r - tile(b, i))
            lvl = _dot_nt(jnp.concatenate(qe_tiles, axis=0).astype(BF16),
                          jnp.concatenate(ke_tiles, axis=0).astype(BF16))
            for j, i in enumerate(q_ids):
                old = a_tiles[ci][i]
                a_tiles[ci][i] = tile(lvl, j) if old is None else jnp.where(tile(xr, i) < 2 * m, tile(lvl, j), old)
        m //= 2
    a_s = [jnp.concatenate([t if t is not None else zero_tile for t in tiles], axis=0) for tiles in a_tiles]

    if fast:
        for ci, (reverse, q, v, k, b, st, b_ref) in enumerate(chains):
            mid = FAST_BLOCK // 2
            r = jnp.concatenate(
                [jnp.broadcast_to(b_ref[j * FAST_BLOCK + mid:j * FAST_BLOCK + mid + 1, :], (FAST_BLOCK, HG_DK))
                 for j in range(c // FAST_BLOCK)], axis=0)
            qe = (q * jnp.exp2(b - r)).astype(BF16)
            ke = (k * jnp.exp2(r - b)).astype(BF16)
            a_s[ci] = jnp.where(xr < FAST_BLOCK, _dot_nt(qe, ke), a_s[ci])
    else:
        for level, m in enumerate(SMALL_LEVELS):
            for ci, (reverse, q, v, k, b, st, b_ref) in enumerate(chains):
                d = b - _ref_rows(b, b_ref, m, reverse)
                mask_q, mask_k = (ml_ref, mr_ref) if reverse else (mr_ref, ml_ref)
                qe = (q * jnp.exp2(d + mask_q[level])).astype(BF16)
                ke = (k * jnp.exp2(mask_k[level] - d)).astype(BF16)
                a_s[ci] = jnp.where(xr < 2 * m, _dot_nt(qe, ke), a_s[ci])

    out = []
    for (reverse, q, v, k, b, st, b_ref), a, o, b_tot in zip(chains, a_s, os_, b_tots):
        if fast:
            a = jnp.where((col >= row) if reverse else (col <= row), a, 0.0)
        else:
            diag = jnp.where(xr == 0, _dot_nt(q.astype(BF16), k.astype(BF16)), 0.0)
            a = jnp.where((col > row) if reverse else (col < row), a, diag)
        o = o + _dot(a.astype(BF16), v.astype(BF16))
        k_end = (k * jnp.exp2(b_tot - b)).astype(BF16)
        st_new = st * jnp.exp2(b_tot) + _dot(v.astype(F32).T.astype(BF16), k_end)
        out.append((o, st_new))
    return out


def _hgrn_kernel(layer, n_chunks, n_heads, split_views, has_s0, *refs):
    refs = list(refs)
    if split_views:
        qf_ref, qb_ref, zf_ref, zb_ref, vf_ref, vb_ref = refs[:6]
        del refs[:6]
    else:
        qf_ref, zf_ref, zb_ref, vf_ref = refs[:4]
        qb_ref, vb_ref = qf_ref, vf_ref
        del refs[:4]
    lg_ref, ml_ref, mr_ref, tri_ref = refs[:4]
    del refs[:4]
    s0_ref = refs.pop(0) if has_s0 else None
    o_out = refs[:2]
    sfin_ref = refs[2]
    del refs[:3]
    n_chains = 2 * n_heads
    st_refs, b_refs, k_refs = (refs[i * n_chains:(i + 1) * n_chains] for i in range(3))
    fast_ref = refs[3 * n_chains]
    chains = [(hh, direction) for hh in range(n_heads) for direction in range(2)]
    q_refs, z_refs, v_refs = (qf_ref, qb_ref), (zf_ref, zb_ref), (vf_ref, vb_ref)
    lanes = lambda hh: slice(hh * HG_DK, (hh + 1) * HG_DK)

    @pl.when(pl.program_id(2) == 0)
    def _():
        for ci, (hh, direction) in enumerate(chains):
            if has_s0:
                st_refs[ci][...] = s0_ref[0, direction, hh].T
            else:
                st_refs[ci][...] = jnp.zeros((HG_DV, HG_DK), F32)

    def lower_bound(hh, direction):
        lg = lg_ref[:, direction, hh, 0, :]
        ex = jnp.exp(lg - lg.max(axis=0, keepdims=True))
        return ex[1:layer + 1].sum(axis=0, keepdims=True) / ex.sum(axis=0, keepdims=True)

    lbs = [lower_bound(hh, direction) for hh, direction in chains]

    def chunk_rows(c):
        return pl.ds(pl.multiple_of(c * SCAN_CHUNK, SCAN_CHUNK), SCAN_CHUNK)

    def gates(c_fwd, c_bwd):
        rows = (chunk_rows(c_fwd), chunk_rows(c_bwd))
        return [_scan_gates(z_refs[direction][rows[direction], lanes(hh)], lb, tri_ref[direction])
                for (hh, direction), lb in zip(chains, lbs)]

    def store_gates(gs):
        lo = None
        for (k, b, lo_c), k_ref, b_ref in zip(gs, k_refs, b_refs):
            k_ref[...] = k
            b_ref[...] = b
            lo = lo_c if lo is None else jnp.minimum(lo, lo_c)
        fast_ref[0] = (lo >= -FAST_LOG2_LIMIT).astype(jnp.int32)

    store_gates(gates(0, n_chunks - 1))

    def trip(c, fast, last):
        rows = (chunk_rows(c), chunk_rows(n_chunks - 1 - c))
        nxt = None if last else gates(c + 1, n_chunks - 2 - c)
        outs = _scan_chunks(
            [(direction == 1, q_refs[direction][rows[direction], lanes(hh)],
              v_refs[direction][rows[direction], lanes(hh)],
              k_refs[ci][...], b_refs[ci][...], st_refs[ci][...], b_refs[ci])
             for ci, (hh, direction) in enumerate(chains)],
            ml_ref, mr_ref, fast)
        for ci, ((hh, direction), (o, st)) in enumerate(zip(chains, outs)):
            st_refs[ci][...] = st
            o_out[direction][rows[direction], lanes(hh)] = o
        if not last:
            store_gates(nxt)

    def guarded_trip(c, last):
        use_fast = fast_ref[0] == 1
        pl.when(use_fast)(lambda: trip(c, True, last))
        pl.when(jnp.logical_not(use_fast))(lambda: trip(c, False, last))

    def scan_body(c, carry):
        guarded_trip(c, False)
        return carry

    lax.fori_loop(0, n_chunks - 1, scan_body, 0)
    guarded_trip(n_chunks - 1, True)

    @pl.when(pl.program_id(2) == pl.num_programs(2) - 1)
    def _():
        for ci, (hh, direction) in enumerate(chains):
            sfin_ref[0, direction, hh] = st_refs[ci][...].T


def _hgrn_bidir(layer, seqs, q, zf, zb, v, logits, s0):
    n = q.shape[0]
    t = n // seqs
    tb = min(t, SCAN_TIME_BLOCK)
    n_tb = t // tb
    split = n_tb > 1
    n_chunks = tb // SCAN_CHUNK
    in_bytes = (2 if split else 1) * (4 + 2) + 2 * 4
    bytes_per_elem = in_bytes + 2 * 4
    fits = lambda heads: 2 * tb * heads * HG_DK * bytes_per_elem <= SCAN_WINDOW_BYTES
    n_heads = next((heads for heads in (8, 4, 2) if heads <= SCAN_HEADS_MAX and fits(heads)), 1)
    n_chains = 2 * n_heads
    ml, mr = _level_masks()
    tri = _tri_tables()
    fwd = pl.BlockSpec((tb, n_heads * HG_DK), lambda s, h, j: (s * n_tb + j, h))
    rev = pl.BlockSpec((tb, n_heads * HG_DK), lambda s, h, j: (s * n_tb + n_tb - 1 - j, h))
    state = lambda: pl.BlockSpec((1, 2, n_heads, HG_DK, HG_DV), lambda s, h, j: (s, 0, h, 0, 0))
    tok_specs, tok_args = ([fwd, rev, fwd, rev, fwd, rev], [q, q, zf, zb, v, v]) if split else \
                          ([fwd, fwd, fwd, fwd], [q, zf, zb, v])
    in_specs = tok_specs + [pl.BlockSpec((DEPTH, 2, n_heads, 1, HG_DK), lambda s, h, j: (0, 0, h, 0, 0)),
                            _resident(ml.shape), _resident(mr.shape), _resident(tri.shape)]
    args = tok_args + [logits.reshape(DEPTH, 2, HG_HEADS, 1, HG_DK),
                       jnp.asarray(ml), jnp.asarray(mr), jnp.asarray(tri).astype(BF16)]
    if s0 is not None:
        in_specs.append(state())
        args.append(s0)
    return pl.pallas_call(
        functools.partial(_hgrn_kernel, layer, n_chunks, n_heads, split, s0 is not None),
        grid=(seqs, HG_HEADS // n_heads, n_tb),
        in_specs=in_specs,
        out_specs=[fwd, rev, state()],
        out_shape=[jax.ShapeDtypeStruct((n, HG_KW), F32), jax.ShapeDtypeStruct((n, HG_KW), F32),
                   jax.ShapeDtypeStruct((seqs, 2, HG_HEADS, HG_DK, HG_DV), F32)],
        scratch_shapes=([pltpu.VMEM((HG_DV, HG_DK), F32)] * n_chains
                        + [pltpu.VMEM((SCAN_CHUNK, HG_DK), F32)] * n_chains
                        + [pltpu.VMEM((SCAN_CHUNK, HG_DK), F32)] * n_chains
                        + [pltpu.SMEM((1,), jnp.int32)]),
        compiler_params=_params(3),
        name="hgrn2_bidir",
    )(*args)


def kernel(x_prompt, x_sample, cache_k, cache_v, state_hgrn, c, c_ctx, ada_w, ada_b, norm_g, ffn_w1, ffn_w2,
           mix0_w_in, mix0_w_out, na_rpb, mix1_w_in, mix1_w_out, hg_lb_logits, hg_norm_g, norm_f):
    bp, sp, d = x_prompt.shape
    bs, ts, _ = x_sample.shape
    xp = x_prompt.reshape(bp * sp, d)
    xs = x_sample.reshape(bs * ts, d)

    cond = jnp.zeros((COND_ROWS, d), F32).at[:bs].set(c).at[CTX_ROW].set(c_ctx)
    mod = _ada_mod(cond, ada_w, ada_b)

    w1 = ffn_w1.astype(BF16)
    w2 = ffn_w2.astype(BF16)

    new_k, new_v, new_s = [], [], []
    for l in range(DEPTH):
        last = l == DEPTH - 1

        def ffn(x, seq_len, j, half, final, mixed=None, scan=None):
            return _half_ffn(x, mod, l, seq_len, j, half, norm_g[l, j], w1, w2, norm_f, final, mixed, scan)

        xp = ffn(xp, None, 0, 0, False)
        xs = ffn(xs, ts, 0, 0, False)

        if l % 2 == 0:
            e = l // 2
            w_in = mix0_w_in[e].astype(BF16)
            w_out = mix0_w_out[e].astype(BF16)
            qk_scale = float(NA_HD) ** -0.5
            qp, kp, vp, fp = _in_proj(xp, mod, l, None, norm_g[l, 1], w_in,
                                      [(NA_W, BF16, qk_scale), (NA_W, F32, 1.0), (NA_W, F32, 1.0), (FN_W, BF16, 1.0)])
            qs, ks, vs, fs = _in_proj(xs, mod, l, ts, norm_g[l, 1], w_in,
                                      [(NA_W, BF16, qk_scale), (NA_W, BF16, 1.0), (NA_W, BF16, 1.0), (FN_W, BF16, 1.0)])
            new_k.append(kp.reshape(bp, sp, NA_HEADS, NA_HD))
            new_v.append(vp.reshape(bp, sp, NA_HEADS, NA_HD))

            r3 = lambda a, b_, t_: a.reshape(b_, t_, a.shape[-1])
            ap = _context_attention(r3(qp, bp, sp), r3(kp, bp, sp), r3(vp, bp, sp))
            ck = cache_k[:, e].reshape(bs, -1, NA_W).astype(BF16)
            cv = cache_v[:, e].reshape(bs, -1, NA_W).astype(BF16)
            a_s = _neighbourhood_attention(r3(qs, bs, ts), r3(ks, bs, ts), r3(vs, bs, ts), ck, cv,
                                           _na_bias_table(na_rpb[e]))
            cgsg = None
            fn = []
            for f3 in (r3(fp, bp, sp), r3(fs, bs, ts)):
                dft_np, cgsg_np = _dft_tables(f3.shape[1])
                fn.append(_fourier_mix(f3, jnp.asarray(dft_np).astype(BF16), jnp.asarray(cgsg_np).astype(BF16)))
            mixed_p = ([ap.reshape(bp * sp, NA_W), fn[0].reshape(bp * sp, FN_W)], w_out)
            mixed_s = ([a_s.reshape(bs * ts, NA_W), fn[1].reshape(bs * ts, FN_W)], w_out)
            scan_p = scan_s = None
        else:
            o = l // 2
            w_in = mix1_w_in[o].astype(BF16)
            w_out = mix1_w_out[o].astype(BF16)
            outs = [(HG_KW, F32, 1.0)] * 3 + [(HG_KW, BF16, 1.0), (HG_KW, F32, 1.0)]
            qp, zfp, zbp, vp, gate_p = _in_proj(xp, mod, l, None, norm_g[l, 1], w_in, outs)
            qs, zfs, zbs, vs, gate_s = _in_proj(xs, mod, l, ts, norm_g[l, 1], w_in, outs)
            ofp, obp, s_fin = _hgrn_bidir(l, bp, qp, zfp, zbp, vp, hg_lb_logits, None)
            ofs, obs, _ = _hgrn_bidir(l, bs, qs, zfs, zbs, vs, hg_lb_logits, state_hgrn[:, o])
            new_s.append(s_fin)
            mixed_p = mixed_s = None
            scan_p = (ofp, obp, gate_p, hg_norm_g[o], w_out)
            scan_s = (ofs, obs, gate_s, hg_norm_g[o], w_out)

        xp = ffn(xp, None, 2, 1, last, mixed_p, scan_p)
        xs = ffn(xs, ts, 2, 1, last, mixed_s, scan_s)

    stack = lambda parts: jnp.expand_dims(parts[0], 1) if len(parts) == 1 else jnp.stack(parts, axis=1)
    return (xp.reshape(bp, sp, d), xs.reshape(bs, ts, d), stack(new_k), stack(new_v), stack(new_s))
```

```python
import functools

import numpy as np
import jax
import jax.numpy as jnp
from jax import lax
from jax.experimental import pallas as pl
from jax.experimental.pallas import tpu as pltpu

F32 = jnp.float32
BF16 = jnp.bfloat16

D_MODEL = 1024
DEPTH = 2
GRID_W = 64
NA_HEADS = 8
NA_HD = 64
NA_W = NA_HEADS * NA_HD
NA_WIN_R = 8
NA_WIN_C = 16
CTX_SEQS_PER_STEP = 2
NA_ROWS_PER_STEP = 4
FN_GROUPS = 4
FN_GW = 128
FN_W = FN_GROUPS * FN_GW
FN_ROWS_PER_STEP = 1024
HG_HEADS = 8
HG_DK = 128
HG_DV = 128
HG_KW = HG_HEADS * HG_DK
D_FF = 2816
N_MOD = 9
EPS = 1e-6

COND_ROWS = 8
CTX_ROW = 4
TOKEN_TILE = 512
PROJ_TILE = 512
SCAN_CHUNK = 128
SUBLANES = 8
SMALL_LEVELS = (4, 2, 1)
SCAN_HEADS_MAX = 8
SCAN_TIME_BLOCK = 1024
SCAN_WINDOW_BYTES = 40 * 1024 * 1024
FAST_BLOCK = 32
FAST_LOG2_LIMIT = 125.0 / (FAST_BLOCK // 2)
MASK_VALUE = -1e30
NEG_BIG = -1e30
VMEM_LIMIT = 56 * 1024 * 1024


def _params(n_axes, vmem=VMEM_LIMIT):
    return pltpu.CompilerParams(dimension_semantics=("arbitrary",) * n_axes, vmem_limit_bytes=vmem)


def _resident(shape):
    nd = len(shape)
    return pl.BlockSpec(shape, lambda *_: (0,) * nd, pipeline_mode=pl.Buffered(1))


def _sigmoid(a):
    return 1.0 / (1.0 + jnp.exp(-a))


def _rms_mod(x, g, shift, scale):
    y = x * lax.rsqrt(jnp.mean(x * x, axis=-1, keepdims=True) + EPS)
    return (y * g) * (1.0 + scale) + shift


def _split2(x):
    hi = x.astype(BF16)
    lo = (x - hi.astype(F32)).astype(BF16)
    return hi, lo


def _dot(a, b):
    return jnp.dot(a, b, preferred_element_type=F32)


def _dot_nt(a, b):
    return lax.dot_general(a, b, (((1,), (1,)), ((), ())), preferred_element_type=F32)


def _ada_kernel(cond_ref, w_ref, b_ref, o_ref):
    c = cond_ref[...]
    s = c * _sigmoid(c)
    s_hi, s_lo = _split2(s)
    w_hi, w_lo = _split2(w_ref[0])
    o_ref[0] = _dot(s_hi, w_hi) + _dot(s_hi, w_lo) + _dot(s_lo, w_hi) + b_ref[0]


def _ada_mod(cond, ada_w, ada_b):
    depth, d, n = ada_w.shape
    tn = 1152
    out = pl.pallas_call(
        _ada_kernel,
        grid=(depth, n // tn),
        in_specs=[
            pl.BlockSpec((COND_ROWS, d), lambda l, j: (0, 0)),
            pl.BlockSpec((1, d, tn), lambda l, j: (l, 0, j)),
            pl.BlockSpec((1, 1, tn), lambda l, j: (l, 0, j)),
        ],
        out_specs=pl.BlockSpec((1, COND_ROWS, tn), lambda l, j: (l, 0, j)),
        out_shape=jax.ShapeDtypeStruct((depth, COND_ROWS, n), F32),
        compiler_params=_params(2),
        name="ada_mod",
    )(cond, ada_w, ada_b.reshape(depth, 1, n))
    return out.reshape(depth, COND_ROWS, N_MOD, d)


def _mod_spec(layer, seq_len, tm):
    if seq_len is None:
        return pl.BlockSpec((1, 1, N_MOD, D_MODEL), lambda i: (layer, CTX_ROW, 0, 0))
    tiles_per_seq = seq_len // tm
    return pl.BlockSpec((1, 1, N_MOD, D_MODEL), lambda i: (layer, i // tiles_per_seq, 0, 0))


def _ffn_kernel(j, final_norm, n_acts, scan_mix, x_ref, mod_ref, g_ref, w1a_ref, w1b_ref, w2_ref, gf_ref, *refs):
    x = x_ref[...]
    m = mod_ref[0, 0]
    if scan_mix:
        of_ref, ob_ref, gate_ref, ng_ref, w_ref, o_ref = refs
        o = of_ref[...] + ob_ref[...]
        ng = ng_ref[...]
        heads = []
        for hh in range(HG_HEADS):
            oh = o[:, hh * HG_DV:(hh + 1) * HG_DV]
            heads.append(oh * lax.rsqrt(jnp.mean(oh * oh, axis=-1, keepdims=True) + EPS) * ng)
        gate = gate_ref[...]
        y = (jnp.concatenate(heads, axis=1) * (gate * _sigmoid(gate))).astype(BF16)
        x = x + m[5:6] * _dot(y, w_ref[...])
    else:
        a_refs, w_refs, o_ref = refs[:n_acts], refs[n_acts:2 * n_acts], refs[2 * n_acts]
        if n_acts:
            y = None
            for a_ref, w_ref in zip(a_refs, w_refs):
                t = _dot(a_ref[...], w_ref[...])
                y = t if y is None else y + t
            x = x + m[5:6] * y
    h = _rms_mod(x, g_ref[...], m[3 * j:3 * j + 1], m[3 * j + 1:3 * j + 2]).astype(BF16)
    a = _dot(h, w1a_ref[...])
    b = _dot(h, w1b_ref[...])
    act = (a * _sigmoid(a) * b).astype(BF16)
    y = x + (0.5 * m[3 * j + 2:3 * j + 3]) * _dot(act, w2_ref[...])
    if final_norm:
        y = y * lax.rsqrt(jnp.mean(y * y, axis=-1, keepdims=True) + EPS) * gf_ref[...]
    o_ref[...] = y


def _half_ffn(x, mod, layer, seq_len, j, half, g, w1, w2, gf, final_norm, mixed=None, scan=None):
    n, d = x.shape
    tm = TOKEN_TILE if scan is not None else 2 * TOKEN_TILE
    once = pl.Buffered(1)
    acts, w_out = mixed if mixed is not None else ((), None)
    width = acts[0].shape[1] if acts else 0
    mixed_specs = ([pl.BlockSpec((tm, width), lambda i: (i, 0)) for _ in acts]
                   + [pl.BlockSpec((width, d), functools.partial(lambda i, blk: (blk, 0), blk=blk),
                                   pipeline_mode=once) for blk in range(len(acts))])
    mixed_args = [*acts, *([w_out] * len(acts))]
    if scan is not None:
        o_fwd, o_bwd, gate, head_g, w_scan = scan
        mixed_specs = ([pl.BlockSpec((tm, d), lambda i: (i, 0)) for _ in range(3)]
                       + [_resident((1, HG_DV)), _resident(w_scan.shape)])
        mixed_args = [o_fwd, o_bwd, gate, head_g.reshape(1, HG_DV), w_scan]
    return pl.pallas_call(
        functools.partial(_ffn_kernel, j, final_norm, len(acts), scan is not None),
        grid=(n // tm,),
        in_specs=[
            pl.BlockSpec((tm, d), lambda i: (i, 0)),
            _mod_spec(layer, seq_len, tm),
            _resident((1, d)),
            pl.BlockSpec((None, None, d, D_FF), lambda i: (layer, half, 0, 0), pipeline_mode=once),
            pl.BlockSpec((None, None, d, D_FF), lambda i: (layer, half, 0, 1), pipeline_mode=once),
            pl.BlockSpec((None, None, D_FF, d), lambda i: (layer, half, 0, 0), pipeline_mode=once),
            _resident((1, d)),
        ] + mixed_specs,
        out_specs=pl.BlockSpec((tm, d), lambda i: (i, 0)),
        out_shape=jax.ShapeDtypeStruct((n, d), F32),
        compiler_params=_params(1),
        name="half_ffn",
    )(x, mod, g.reshape(1, d), w1, w1, w2, gf.reshape(1, d), *mixed_args)


def _proj_kernel(j, splits, x_ref, mod_ref, g_ref, w_ref, *o_refs):
    m = mod_ref[0, 0]
    h = _rms_mod(x_ref[...], g_ref[...], m[3 * j:3 * j + 1], m[3 * j + 1:3 * j + 2]).astype(BF16)
    u = _dot(h, w_ref[...])
    for (lo, hi, scale), o_ref in zip(splits, o_refs):
        piece = u[:, lo:hi]
        if scale != 1.0:
            piece = piece * scale
        o_ref[...] = piece.astype(o_ref.dtype)


def _in_proj(x, mod, layer, seq_len, g, w, outs):
    n, d = x.shape
    tm = PROJ_TILE
    splits, lo = [], 0
    for width, _, scale in outs:
        splits.append((lo, lo + width, scale))
        lo += width
    return pl.pallas_call(
        functools.partial(_proj_kernel, 1, tuple(splits)),
        grid=(n // tm,),
        in_specs=[
            pl.BlockSpec((tm, d), lambda i: (i, 0)),
            _mod_spec(layer, seq_len, tm),
            _resident((1, d)),
            _resident(w.shape),
        ],
        out_specs=[pl.BlockSpec((tm, width), lambda i: (i, 0)) for width, _, _ in outs],
        out_shape=[jax.ShapeDtypeStruct((n, width), dt) for width, dt, _ in outs],
        compiler_params=_params(1),
        name="mixer_in_proj",
    )(x, mod, g.reshape(1, d), w)


def _forget_gates(z, lb):
    g = (1.0 - lb) * _sigmoid(z)
    return (1.0 - lb) - g, jnp.log2(lb + g)


def _head_pair_attention(pairs):
    lane = lax.broadcasted_iota(jnp.int32, (1, 2 * NA_HD), 1)
    first = lane < NA_HD
    scores = []
    for q, key_blocks, _, bias_blocks in pairs:
        zero = jnp.zeros_like(q)
        qm = jnp.concatenate([jnp.where(first, q, zero), jnp.where(first, zero, q)], axis=0)
        ss = []
        for kb, bias in zip(key_blocks, bias_blocks):
            s = _dot_nt(qm, kb)
            ss.append(s if bias is None else s + bias)
        scores.append(ss)
    maxes = []
    for ss in scores:
        mx = ss[0].max(axis=-1, keepdims=True)
        for s in ss[1:]:
            mx = jnp.maximum(mx, s.max(axis=-1, keepdims=True))
        maxes.append(mx)
    outs = []
    for (q, _, value_blocks, _), ss, mx in zip(pairs, scores, maxes):
        den, acc = None, None
        for s, vb in zip(ss, value_blocks):
            p = jnp.exp(s - mx)
            dsum = p.sum(axis=-1, keepdims=True)
            pv = _dot(p.astype(BF16), vb)
            den = dsum if den is None else den + dsum
            acc = pv if acc is None else acc + pv
        o = acc / den
        tq = q.shape[0]
        outs.append(jnp.where(first, o[:tq], o[tq:]))
    return outs


def _ctx_attn_kernel(q_ref, k_ref, v_ref, o_ref):
    dests = [(i, slice(pr * 2 * NA_HD, (pr + 1) * 2 * NA_HD))
             for i in range(CTX_SEQS_PER_STEP) for pr in range(NA_HEADS // 2)]
    outs = _head_pair_attention([(q_ref[i, :, sl], [k_ref[i, :, sl].astype(BF16)],
                                  [v_ref[i, :, sl].astype(BF16)], [None]) for i, sl in dests])
    for (i, sl), o in zip(dests, outs):
        o_ref[i, :, sl] = o.astype(o_ref.dtype)


def _context_attention(q, k, v):
    b, s, w = q.shape
    blk = lambda: pl.BlockSpec((CTX_SEQS_PER_STEP, s, w), lambda i: (i, 0, 0))
    return pl.pallas_call(
        _ctx_attn_kernel,
        grid=(b // CTX_SEQS_PER_STEP,),
        in_specs=[blk(), blk(), blk()],
        out_specs=blk(),
        out_shape=jax.ShapeDtypeStruct((b, s, w), BF16),
        compiler_params=_params(1),
        name="context_attention",
    )(q, k, v)


def _na_kernel(rows, q_ref, k_ref, v_ref, ck_ref, cv_ref, bias_ref, o_ref):
    slices = [slice(pr * 2 * NA_HD, (pr + 1) * 2 * NA_HD) for pr in range(NA_HEADS // 2)]
    pairs, dests = [], []
    for rr in range(NA_ROWS_PER_STEP):
        r = pl.program_id(1) * NA_ROWS_PER_STEP + rr
        r0 = jnp.clip(r - NA_WIN_R // 2, 0, rows - NA_WIN_R)
        start = pl.multiple_of(r0 * GRID_W, GRID_W)
        win = pl.ds(start, NA_WIN_R * GRID_W)
        first = (NA_WIN_R - 1) - (r - r0)
        q_rows = slice(rr * GRID_W, (rr + 1) * GRID_W)
        for pr, sl in enumerate(slices):
            bias = jnp.concatenate(
                [jnp.concatenate([bias_ref[2 * pr + hh, first + 2 * jj] for jj in range(NA_WIN_R // 2)], axis=1)
                 for hh in range(2)], axis=0)
            pairs.append((q_ref[0, q_rows, sl], [k_ref[0, win, sl], ck_ref[0, :, sl]],
                          [v_ref[0, win, sl], cv_ref[0, :, sl]], [bias, None]))
            dests.append((q_rows, sl))
    for (q_rows, sl), o in zip(dests, _head_pair_attention(pairs)):
        o_ref[0, q_rows, sl] = o.astype(o_ref.dtype)


def _na_bias_table(rpb):
    cols = np.arange(GRID_W)
    col_start = np.clip(cols - NA_WIN_C // 2, 0, GRID_W - NA_WIN_C)
    inside = (cols[None, :] >= col_start[:, None]) & (cols[None, :] < col_start[:, None] + NA_WIN_C)
    pad = GRID_W - NA_WIN_C
    period = 2 * GRID_W
    v = jnp.pad(rpb.astype(F32), ((0, 0), (0, 0), (pad, pad + 1)))
    t = jnp.tile(v, (1, 1, GRID_W))[:, :, :GRID_W * (period - 1)]
    t = t.reshape(*rpb.shape[:2], GRID_W, period - 1)[:, :, :, GRID_W - 1:]
    t = jnp.where(inside[None, None], t, MASK_VALUE)
    return jnp.concatenate([t[:, :-1], t[:, 1:]], axis=-1)


def _neighbourhood_attention(q, k, v, ck, cv, bias):
    b, t, w = q.shape
    rows = t // GRID_W
    p = ck.shape[1]
    seq = lambda n: pl.BlockSpec((1, n, w), lambda i, r: (i, 0, 0))
    row = lambda: pl.BlockSpec((1, NA_ROWS_PER_STEP * GRID_W, w), lambda i, r: (i, r, 0))
    return pl.pallas_call(
        functools.partial(_na_kernel, rows),
        grid=(b, rows // NA_ROWS_PER_STEP),
        in_specs=[row(), seq(t), seq(t), seq(p), seq(p), _resident(bias.shape)],
        out_specs=row(),
        out_shape=jax.ShapeDtypeStruct((b, t, w), BF16),
        compiler_params=_params(2),
        name="neighbourhood_attention",
    )(q, k, v, ck, cv, bias)


def _dft_tables(t):
    def cs(n):
        idx = (np.arange(n)[:, None] * np.arange(n)[None, :]) % n
        ang = 2.0 * np.pi * idx.astype(np.float64) / n
        return np.cos(ang), np.sin(ang)
    ct, st = cs(t)
    cg, sg = cs(FN_GW)
    return (np.concatenate([ct, -st], axis=1).astype(np.float32),
            np.concatenate([cg, sg], axis=1).astype(np.float32))


def _fnet_kernel(t, seqs, scale, f_ref, cgsg_ref, dft_ref, o_ref, xcs_ref):
    first = pl.program_id(1) * seqs

    @pl.when(pl.program_id(0) == 0)
    def _():
        for s in range(seqs):
            for g in range(FN_GROUPS):
                sl = slice(g * FN_GW, (g + 1) * FN_GW)
                xcs = _dot(f_ref[s, :, sl], cgsg_ref[...])
                xcs_ref[first + s, 0:t, sl] = xcs[:, :FN_GW].astype(BF16)
                xcs_ref[first + s, t:2 * t, sl] = xcs[:, FN_GW:].astype(BF16)

    for s in range(seqs):
        o_ref[s] = (_dot(dft_ref[...], xcs_ref[first + s]) * scale).astype(o_ref.dtype)


def _fourier_mix(f, dft, cgsg):
    b, t, w = f.shape
    tq = min(t, TOKEN_TILE)
    seqs = max(1, min(b, FN_ROWS_PER_STEP // t))
    groups = b // seqs
    scale = float(1.0 / np.sqrt(float(t * FN_GW)))
    return pl.pallas_call(
        functools.partial(_fnet_kernel, t, seqs, scale),
        grid=(t // tq, groups),
        in_specs=[
            pl.BlockSpec((seqs, t, w), lambda r, i: (jnp.where(r == 0, i, groups - 1), 0, 0)),
            pl.BlockSpec(cgsg.shape, lambda r, i: (0, 0)),
            pl.BlockSpec((tq, 2 * t), lambda r, i: (r, 0)),
        ],
        out_specs=pl.BlockSpec((seqs, tq, w), lambda r, i: (i, r, 0)),
        out_shape=jax.ShapeDtypeStruct((b, t, w), BF16),
        scratch_shapes=[pltpu.VMEM((b, 2 * t, w), BF16)],
        compiler_params=_params(2),
        name="fourier_mix",
    )(f, cgsg, dft)


def _ref_rows(b, b_ref, m, reverse):
    c = SCAN_CHUNK
    blk = 2 * m
    anchor = m if reverse else m - 1
    if blk >= 8:
        pieces = [jnp.broadcast_to(b_ref[p * blk + anchor:p * blk + anchor + 1, :], (blk, HG_DK))
                  for p in range(c // blk)]
        return pieces[0] if len(pieces) == 1 else jnp.concatenate(pieces, axis=0)
    pos = lax.broadcasted_iota(jnp.int32, (c, HG_DK), 0) % blk
    r = b
    for p in range(blk):
        delta = anchor - p
        if delta != 0:
            r = jnp.where(pos == p, pltpu.roll(b, (-delta) % c, 0), r)
    return r


def _level_masks():
    c = SCAN_CHUNK
    t = np.arange(c)
    left, right = [], []
    for m in SMALL_LEVELS:
        is_right = (t // m) % 2 == 1
        left.append(np.where(is_right, NEG_BIG, 0.0))
        right.append(np.where(is_right, 0.0, NEG_BIG))
    full = lambda rows: np.ascontiguousarray(
        np.broadcast_to(np.stack(rows)[:, :, None], (len(rows), c, HG_DK))).astype(np.float32)
    return full(left), full(right)


def _tri_tables():
    t = np.arange(SCAN_CHUNK)
    return np.stack([t[None, :] <= t[:, None], t[None, :] >= t[:, None]]).astype(np.float32)


def _scan_gates(z, lb, tri):
    k, log2_f = _forget_gates(z, lb)
    b = None
    for part in _split2(log2_f):
        t = _dot(tri, part)
        b = t if b is None else b + t
    return k, b, jnp.min(log2_f)


def _scan_chunks(chains, ml_ref, mr_ref, fast):
    c = SCAN_CHUNK
    n_tiles = c // SUBLANES
    tile = lambda x, i: x[i * SUBLANES:(i + 1) * SUBLANES]
    row = lax.broadcasted_iota(jnp.int32, (c, c), 0)
    col = lax.broadcasted_iota(jnp.int32, (c, c), 1)
    xr = row ^ col
    zero_tile = jnp.zeros((SUBLANES, HG_DK), F32)

    os_, b_tots = [], []
    for reverse, q, v, k, b, st, b_ref in chains:
        edge = 0 if reverse else c - 1
        b_tots.append(b_ref[edge:edge + 1, :])
        os_.append(_dot_nt((q * jnp.exp2(b)).astype(BF16), st.astype(BF16)))

    a_tiles = [[None] * n_tiles for _ in chains]
    m = c // 2
    while m >= (FAST_BLOCK if fast else SUBLANES):
        g = m // SUBLANES
        for ci, (reverse, q, v, k, b, st, b_ref) in enumerate(chains):
            q_ids, qe_tiles, ke_tiles = [], [], [zero_tile] * n_tiles
            for p in range(n_tiles // (2 * g)):
                left = range(2 * g * p, 2 * g * p + g)
                right = range(2 * g * p + g, 2 * g * (p + 1))
                anchor = (2 * g * p + g) * SUBLANES - (0 if reverse else 1)
                r = jnp.broadcast_to(b_ref[anchor:anchor + 1, :], (SUBLANES, HG_DK))
                q_side, k_side = (left, right) if reverse else (right, left)
                for i in q_side:
                    q_ids.append(i)
                    qe_tiles.append(tile(q, i) * jnp.exp2(tile(b, i) - r))
                for i in k_side:
                    ke_tiles[i] = tile(k, i) * jnp.exp2(r - tile(b, i))
            lvl = _dot_nt(jnp.concatenate(qe_tiles, axis=0).astype(BF16),
                          jnp.concatenate(ke_tiles, axis=0).astype(BF16))
            for j, i in enumerate(q_ids):
                old = a_tiles[ci][i]
                a_tiles[ci][i] = tile(lvl, j) if old is None else jnp.where(tile(xr, i) < 2 * m, tile(lvl, j), old)
        m //= 2
    a_s = [jnp.concatenate([t if t is not None else zero_tile for t in tiles], axis=0) for tiles in a_tiles]

    if fast:
        for ci, (reverse, q, v, k, b, st, b_ref) in enumerate(chains):
            mid = FAST_BLOCK // 2
            r = jnp.concatenate(
                [jnp.broadcast_to(b_ref[j * FAST_BLOCK + mid:j * FAST_BLOCK + mid + 1, :], (FAST_BLOCK, HG_DK))
                 for j in range(c // FAST_BLOCK)], axis=0)
            qe = (q * jnp.exp2(b - r)).astype(BF16)
            ke = (k * jnp.exp2(r - b)).astype(BF16)
            a_s[ci] = jnp.where(xr < FAST_BLOCK, _dot_nt(qe, ke), a_s[ci])
    else:
        for level, m in enumerate(SMALL_LEVELS):
            for ci, (reverse, q, v, k, b, st, b_ref) in enumerate(chains):
                d = b - _ref_rows(b, b_ref, m, reverse)
                mask_q, mask_k = (ml_ref, mr_ref) if reverse else (mr_ref, ml_ref)
                qe = (q * jnp.exp2(d + mask_q[level])).astype(BF16)
                ke = (k * jnp.exp2(mask_k[level] - d)).astype(BF16)
                a_s[ci] = jnp.where(xr < 2 * m, _dot_nt(qe, ke), a_s[ci])

    out = []
    for (reverse, q, v, k, b, st, b_ref), a, o, b_tot in zip(chains, a_s, os_, b_tots):
        if fast:
            a = jnp.where((col >= row) if reverse else (col <= row), a, 0.0)
        else:
            diag = jnp.where(xr == 0, _dot_nt(q.astype(BF16), k.astype(BF16)), 0.0)
            a = jnp.where((col > row) if reverse else (col < row), a, diag)
        o = o + _dot(a.astype(BF16), v.astype(BF16))
        k_end = (k * jnp.exp2(b_tot - b)).astype(BF16)
        st_new = st * jnp.exp2(b_tot) + _dot(v.astype(F32).T.astype(BF16), k_end)
        out.append((o, st_new))
    return out


def _hgrn_kernel(layer, n_chunks, n_heads, split_views, has_s0, *refs):
    refs = list(refs)
    if split_views:
        qf_ref, qb_ref, zf_ref, zb_ref, vf_ref, vb_ref = refs[:6]
        del refs[:6]
    else:
        qf_ref, zf_ref, zb_ref, vf_ref = refs[:4]
        qb_ref, vb_ref = qf_ref, vf_ref
        del refs[:4]
    lg_ref, ml_ref, mr_ref, tri_ref = refs[:4]
    del refs[:4]
    s0_ref = refs.pop(0) if has_s0 else None
    o_out = refs[:2]
    sfin_ref = refs[2]
    del refs[:3]
    n_chains = 2 * n_heads
    st_refs, b_refs, k_refs = (refs[i * n_chains:(i + 1) * n_chains] for i in range(3))
    fast_ref = refs[3 * n_chains]
    chains = [(hh, direction) for hh in range(n_heads) for direction in range(2)]
    q_refs, z_refs, v_refs = (qf_ref, qb_ref), (zf_ref, zb_ref), (vf_ref, vb_ref)
    lanes = lambda hh: slice(hh * HG_DK, (hh + 1) * HG_DK)

    @pl.when(pl.program_id(2) == 0)
    def _():
        for ci, (hh, direction) in enumerate(chains):
            if has_s0:
                st_refs[ci][...] = s0_ref[0, direction, hh].T
            else:
                st_refs[ci][...] = jnp.zeros((HG_DV, HG_DK), F32)

    def lower_bound(hh, direction):
        lg = lg_ref[:, direction, hh, 0, :]
        ex = jnp.exp(lg - lg.max(axis=0, keepdims=True))
        return ex[1:layer + 1].sum(axis=0, keepdims=True) / ex.sum(axis=0, keepdims=True)

    lbs = [lower_bound(hh, direction) for hh, direction in chains]

    def chunk_rows(c):
        return pl.ds(pl.multiple_of(c * SCAN_CHUNK, SCAN_CHUNK), SCAN_CHUNK)

    def gates(c_fwd, c_bwd):
        rows = (chunk_rows(c_fwd), chunk_rows(c_bwd))
        return [_scan_gates(z_refs[direction][rows[direction], lanes(hh)], lb, tri_ref[direction])
                for (hh, direction), lb in zip(chains, lbs)]

    def store_gates(gs):
        lo = None
        for (k, b, lo_c), k_ref, b_ref in zip(gs, k_refs, b_refs):
            k_ref[...] = k
            b_ref[...] = b
            lo = lo_c if lo is None else jnp.minimum(lo, lo_c)
        fast_ref[0] = (lo >= -FAST_LOG2_LIMIT).astype(jnp.int32)

    store_gates(gates(0, n_chunks - 1))

    def trip(c, fast, last):
        rows = (chunk_rows(c), chunk_rows(n_chunks - 1 - c))
        nxt = None if last else gates(c + 1, n_chunks - 2 - c)
        outs = _scan_chunks(
            [(direction == 1, q_refs[direction][rows[direction], lanes(hh)],
              v_refs[direction][rows[direction], lanes(hh)],
              k_refs[ci][...], b_refs[ci][...], st_refs[ci][...], b_refs[ci])
             for ci, (hh, direction) in enumerate(chains)],
            ml_ref, mr_ref, fast)
        for ci, ((hh, direction), (o, st)) in enumerate(zip(chains, outs)):
            st_refs[ci][...] = st
            o_out[direction][rows[direction], lanes(hh)] = o
        if not last:
            store_gates(nxt)

    def guarded_trip(c, last):
        use_fast = fast_ref[0] == 1
        pl.when(use_fast)(lambda: trip(c, True, last))
        pl.when(jnp.logical_not(use_fast))(lambda: trip(c, False, last))

    def scan_body(c, carry):
        guarded_trip(c, False)
        return carry

    lax.fori_loop(0, n_chunks - 1, scan_body, 0)
    guarded_trip(n_chunks - 1, True)

    @pl.when(pl.program_id(2) == pl.num_programs(2) - 1)
    def _():
        for ci, (hh, direction) in enumerate(chains):
            sfin_ref[0, direction, hh] = st_refs[ci][...].T


def _hgrn_bidir(layer, seqs, q, zf, zb, v, logits, s0):
    n = q.shape[0]
    t = n // seqs
    tb = min(t, SCAN_TIME_BLOCK)
    n_tb = t // tb
    split = n_tb > 1
    n_chunks = tb // SCAN_CHUNK
    in_bytes = (2 if split else 1) * (4 + 2) + 2 * 4
    bytes_per_elem = in_bytes + 2 * 4
    fits = lambda heads: 2 * tb * heads * HG_DK * bytes_per_elem <= SCAN_WINDOW_BYTES
    n_heads = next((heads for heads in (8, 4, 2) if heads <= SCAN_HEADS_MAX and fits(heads)), 1)
    n_chains = 2 * n_heads
    ml, mr = _level_masks()
    tri = _tri_tables()
    fwd = pl.BlockSpec((tb, n_heads * HG_DK), lambda s, h, j: (s * n_tb + j, h))
    rev = pl.BlockSpec((tb, n_heads * HG_DK), lambda s, h, j: (s * n_tb + n_tb - 1 - j, h))
    state = lambda: pl.BlockSpec((1, 2, n_heads, HG_DK, HG_DV), lambda s, h, j: (s, 0, h, 0, 0))
    tok_specs, tok_args = ([fwd, rev, fwd, rev, fwd, rev], [q, q, zf, zb, v, v]) if split else \
                          ([fwd, fwd, fwd, fwd], [q, zf, zb, v])
    in_specs = tok_specs + [pl.BlockSpec((DEPTH, 2, n_heads, 1, HG_DK), lambda s, h, j: (0, 0, h, 0, 0)),
                            _resident(ml.shape), _resident(mr.shape), _resident(tri.shape)]
    args = tok_args + [logits.reshape(DEPTH, 2, HG_HEADS, 1, HG_DK),
                       jnp.asarray(ml), jnp.asarray(mr), jnp.asarray(tri).astype(BF16)]
    if s0 is not None:
        in_specs.append(state())
        args.append(s0)
    return pl.pallas_call(
        functools.partial(_hgrn_kernel, layer, n_chunks, n_heads, split, s0 is not None),
        grid=(seqs, HG_HEADS // n_heads, n_tb),
        in_specs=in_specs,
        out_specs=[fwd, rev, state()],
        out_shape=[jax.ShapeDtypeStruct((n, HG_KW), F32), jax.ShapeDtypeStruct((n, HG_KW), F32),
                   jax.ShapeDtypeStruct((seqs, 2, HG_HEADS, HG_DK, HG_DV), F32)],
        scratch_shapes=([pltpu.VMEM((HG_DV, HG_DK), F32)] * n_chains
                        + [pltpu.VMEM((SCAN_CHUNK, HG_DK), F32)] * n_chains
                        + [pltpu.VMEM((SCAN_CHUNK, HG_DK), F32)] * n_chains
                        + [pltpu.SMEM((1,), jnp.int32)]),
        compiler_params=_params(3),
        name="hgrn2_bidir",
    )(*args)


def kernel(x_prompt, x_sample, cache_k, cache_v, state_hgrn, c, c_ctx, ada_w, ada_b, norm_g, ffn_w1, ffn_w2,
           mix0_w_in, mix0_w_out, na_rpb, mix1_w_in, mix1_w_out, hg_lb_logits, hg_norm_g, norm_f):
    bp, sp, d = x_prompt.shape
    bs, ts, _ = x_sample.shape
    xp = x_prompt.reshape(bp * sp, d)
    xs = x_sample.reshape(bs * ts, d)

    cond = jnp.zeros((COND_ROWS, d), F32).at[:bs].set(c).at[CTX_ROW].set(c_ctx)
    mod = _ada_mod(cond, ada_w, ada_b)

    w1 = ffn_w1.astype(BF16)
    w2 = ffn_w2.astype(BF16)

    new_k, new_v, new_s = [], [], []
    for l in range(DEPTH):
        last = l == DEPTH - 1

        def ffn(x, seq_len, j, half, final, mixed=None, scan=None):
            return _half_ffn(x, mod, l, seq_len, j, half, norm_g[l, j], w1, w2, norm_f, final, mixed, scan)

        xp = ffn(xp, None, 0, 0, False)
        xs = ffn(xs, ts, 0, 0, False)

        if l % 2 == 0:
            e = l // 2
            w_in = mix0_w_in[e].astype(BF16)
            w_out = mix0_w_out[e].astype(BF16)
            qk_scale = float(NA_HD) ** -0.5
            qp, kp, vp, fp = _in_proj(xp, mod, l, None, norm_g[l, 1], w_in,
                                      [(NA_W, BF16, qk_scale), (NA_W, F32, 1.0), (NA_W, F32, 1.0), (FN_W, BF16, 1.0)])
            qs, ks, vs, fs = _in_proj(xs, mod, l, ts, norm_g[l, 1], w_in,
                                      [(NA_W, BF16, qk_scale), (NA_W, BF16, 1.0), (NA_W, BF16, 1.0), (FN_W, BF16, 1.0)])
            new_k.append(kp.reshape(bp, sp, NA_HEADS, NA_HD))
            new_v.append(vp.reshape(bp, sp, NA_HEADS, NA_HD))

            r3 = lambda a, b_, t_: a.reshape(b_, t_, a.shape[-1])
            ap = _context_attention(r3(qp, bp, sp), r3(kp, bp, sp), r3(vp, bp, sp))
            ck = cache_k[:, e].reshape(bs, -1, NA_W).astype(BF16)
            cv = cache_v[:, e].reshape(bs, -1, NA_W).astype(BF16)
            a_s = _neighbourhood_attention(r3(qs, bs, ts), r3(ks, bs, ts), r3(vs, bs, ts), ck, cv,
                                           _na_bias_table(na_rpb[e]))
            cgsg = None
            fn = []
            for f3 in (r3(fp, bp, sp), r3(fs, bs, ts)):
                dft_np, cgsg_np = _dft_tables(f3.shape[1])
                fn.append(_fourier_mix(f3, jnp.asarray(dft_np).astype(BF16), jnp.asarray(cgsg_np).astype(BF16)))
            mixed_p = ([ap.reshape(bp * sp, NA_W), fn[0].reshape(bp * sp, FN_W)], w_out)
            mixed_s = ([a_s.reshape(bs * ts, NA_W), fn[1].reshape(bs * ts, FN_W)], w_out)
            scan_p = scan_s = None
        else:
            o = l // 2
            w_in = mix1_w_in[o].astype(BF16)
            w_out = mix1_w_out[o].astype(BF16)
            outs = [(HG_KW, F32, 1.0)] * 3 + [(HG_KW, BF16, 1.0), (HG_KW, F32, 1.0)]
            qp, zfp, zbp, vp, gate_p = _in_proj(xp, mod, l, None, norm_g[l, 1], w_in, outs)
            qs, zfs, zbs, vs, gate_s = _in_proj(xs, mod, l, ts, norm_g[l, 1], w_in, outs)
            ofp, obp, s_fin = _hgrn_bidir(l, bp, qp, zfp, zbp, vp, hg_lb_logits, None)
            ofs, obs, _ = _hgrn_bidir(l, bs, qs, zfs, zbs, vs, hg_lb_logits, state_hgrn[:, o])
            new_s.append(s_fin)
            mixed_p = mixed_s = None
            scan_p = (ofp, obp, gate_p, hg_norm_g[o], w_out)
            scan_s = (ofs, obs, gate_s, hg_norm_g[o], w_out)

        xp = ffn(xp, None, 2, 1, last, mixed_p, scan_p)
        xs = ffn(xs, ts, 2, 1, last, mixed_s, scan_s)

    stack = lambda parts: jnp.expand_dims(parts[0], 1) if len(parts) == 1 else jnp.stack(parts, axis=1)
    return (xp.reshape(bp, sp, d), xs.reshape(bs, ts, d), stack(new_k), stack(new_v), stack(new_s))
```

```python
import functools

import numpy as np
import jax
import jax.numpy as jnp
from jax import lax
from jax.experimental import pallas as pl
from jax.experimental.pallas import tpu as pltpu

F32 = jnp.float32
BF16 = jnp.bfloat16

D_MODEL = 1024
DEPTH = 2
GRID_W = 64
NA_HEADS = 8
NA_HD = 64
NA_W = NA_HEADS * NA_HD
NA_WIN_R = 8
NA_WIN_C = 16
CTX_SEQS_PER_STEP = 2
NA_ROWS_PER_STEP = 4
FN_GROUPS = 4
FN_GW = 128
FN_W = FN_GROUPS * FN_GW
FN_ROWS_PER_STEP = 1024
HG_HEADS = 8
HG_DK = 128
HG_DV = 128
HG_KW = HG_HEADS * HG_DK
D_FF = 2816
N_MOD = 9
EPS = 1e-6

COND_ROWS = 8
CTX_ROW = 4
ADA_TILE_N = 1152
TOKEN_TILE = 512
PROJ_TILE = 512
SCAN_CHUNK = 128
SUBLANES = 8
SMALL_LEVELS = (4, 2, 1)
SCAN_HEADS_MAX = 8
SCAN_TIME_BLOCK = 1024
SCAN_WINDOW_BYTES = 40 * 1024 * 1024
FAST_BLOCK = 32
FAST_LOG2_LIMIT = 125.0 / (FAST_BLOCK // 2)
MASK_VALUE = -1e30
NEG_BIG = -1e30
VMEM_LIMIT = 56 * 1024 * 1024


def _params(n_axes, vmem=VMEM_LIMIT):
    return pltpu.CompilerParams(dimension_semantics=("arbitrary",) * n_axes, vmem_limit_bytes=vmem)


def _resident(shape):
    nd = len(shape)
    return pl.BlockSpec(shape, lambda *_: (0,) * nd, pipeline_mode=pl.Buffered(1))


def _sigmoid(a):
    return 1.0 / (1.0 + jnp.exp(-a))


def _rms_mod(x, g, shift, scale):
    y = x * lax.rsqrt(jnp.mean(x * x, axis=-1, keepdims=True) + EPS)
    return (y * g) * (1.0 + scale) + shift


def _split2(x):
    hi = x.astype(BF16)
    lo = (x - hi.astype(F32)).astype(BF16)
    return hi, lo


def _dot(a, b):
    return jnp.dot(a, b, preferred_element_type=F32)


def _dot_nt(a, b):
    return lax.dot_general(a, b, (((1,), (1,)), ((), ())), preferred_element_type=F32)


def _ada_kernel(cond_ref, w_ref, b_ref, o_ref):
    c = cond_ref[...]
    s = c * _sigmoid(c)
    s_hi, s_lo = _split2(s)
    w_hi, w_lo = _split2(w_ref[0])
    o_ref[0] = _dot(s_hi, w_hi) + _dot(s_hi, w_lo) + _dot(s_lo, w_hi) + b_ref[0]


def _ada_mod(cond, ada_w, ada_b):
    depth, d, n = ada_w.shape
    tn = ADA_TILE_N
    out = pl.pallas_call(
        _ada_kernel,
        grid=(depth, n // tn),
        in_specs=[
            pl.BlockSpec((COND_ROWS, d), lambda l, j: (0, 0)),
            pl.BlockSpec((1, d, tn), lambda l, j: (l, 0, j)),
            pl.BlockSpec((1, 1, tn), lambda l, j: (l, 0, j)),
        ],
        out_specs=pl.BlockSpec((1, COND_ROWS, tn), lambda l, j: (l, 0, j)),
        out_shape=jax.ShapeDtypeStruct((depth, COND_ROWS, n), F32),
        compiler_params=_params(2),
        name="ada_mod",
    )(cond, ada_w, ada_b.reshape(depth, 1, n))
    return out.reshape(depth, COND_ROWS, N_MOD, d)


def _mod_spec(layer, seq_len, tm):
    if seq_len is None:
        return pl.BlockSpec((1, 1, N_MOD, D_MODEL), lambda i: (layer, CTX_ROW, 0, 0))
    tiles_per_seq = seq_len // tm
    return pl.BlockSpec((1, 1, N_MOD, D_MODEL), lambda i: (layer, i // tiles_per_seq, 0, 0))


def _ffn_kernel(j, final_norm, n_acts, scan_mix, x_ref, mod_ref, g_ref, w1a_ref, w1b_ref, w2_ref, gf_ref, *refs):
    x = x_ref[...]
    m = mod_ref[0, 0]
    if scan_mix:
        of_ref, ob_ref, gate_ref, ng_ref, w_ref, o_ref = refs
        o = of_ref[...] + ob_ref[...]
        ng = ng_ref[...]
        heads = []
        for hh in range(HG_HEADS):
            oh = o[:, hh * HG_DV:(hh + 1) * HG_DV]
            heads.append(oh * lax.rsqrt(jnp.mean(oh * oh, axis=-1, keepdims=True) + EPS) * ng)
        gate = gate_ref[...]
        y = (jnp.concatenate(heads, axis=1) * (gate * _sigmoid(gate))).astype(BF16)
        x = x + m[5:6] * _dot(y, w_ref[...])
    else:
        a_refs, w_refs, o_ref = refs[:n_acts], refs[n_acts:2 * n_acts], refs[2 * n_acts]
        if n_acts:
            y = None
            for a_ref, w_ref in zip(a_refs, w_refs):
                t = _dot(a_ref[...], w_ref[...])
                y = t if y is None else y + t
            x = x + m[5:6] * y
    h = _rms_mod(x, g_ref[...], m[3 * j:3 * j + 1], m[3 * j + 1:3 * j + 2]).astype(BF16)
    a = _dot(h, w1a_ref[...])
    b = _dot(h, w1b_ref[...])
    act = (a * _sigmoid(a) * b).astype(BF16)
    y = x + (0.5 * m[3 * j + 2:3 * j + 3]) * _dot(act, w2_ref[...])
    if final_norm:
        y = y * lax.rsqrt(jnp.mean(y * y, axis=-1, keepdims=True) + EPS) * gf_ref[...]
    o_ref[...] = y


def _half_ffn(x, mod, layer, seq_len, j, half, g, w1, w2, gf, final_norm, mixed=None, scan=None):
    n, d = x.shape
    tm = TOKEN_TILE
    once = pl.Buffered(1)
    acts, w_out = mixed if mixed is not None else ((), None)
    width = acts[0].shape[1] if acts else 0
    mixed_specs = ([pl.BlockSpec((tm, width), lambda i: (i, 0)) for _ in acts]
                   + [pl.BlockSpec((width, d), functools.partial(lambda i, blk: (blk, 0), blk=blk),
                                   pipeline_mode=once) for blk in range(len(acts))])
    mixed_args = [*acts, *([w_out] * len(acts))]
    if scan is not None:
        o_fwd, o_bwd, gate, head_g, w_scan = scan
        mixed_specs = ([pl.BlockSpec((tm, d), lambda i: (i, 0)) for _ in range(3)]
                       + [_resident((1, HG_DV)), _resident(w_scan.shape)])
        mixed_args = [o_fwd, o_bwd, gate, head_g.reshape(1, HG_DV), w_scan]
    return pl.pallas_call(
        functools.partial(_ffn_kernel, j, final_norm, len(acts), scan is not None),
        grid=(n // tm,),
        in_specs=[
            pl.BlockSpec((tm, d), lambda i: (i, 0)),
            _mod_spec(layer, seq_len, tm),
            _resident((1, d)),
            pl.BlockSpec((None, None, d, D_FF), lambda i: (layer, half, 0, 0), pipeline_mode=once),
            pl.BlockSpec((None, None, d, D_FF), lambda i: (layer, half, 0, 1), pipeline_mode=once),
            pl.BlockSpec((None, None, D_FF, d), lambda i: (layer, half, 0, 0), pipeline_mode=once),
            _resident((1, d)),
        ] + mixed_specs,
        out_specs=pl.BlockSpec((tm, d), lambda i: (i, 0)),
        out_shape=jax.ShapeDtypeStruct((n, d), F32),
        compiler_params=_params(1),
        name="half_ffn",
    )(x, mod, g.reshape(1, d), w1, w1, w2, gf.reshape(1, d), *mixed_args)


def _proj_kernel(j, splits, x_ref, mod_ref, g_ref, w_ref, *o_refs):
    m = mod_ref[0, 0]
    h = _rms_mod(x_ref[...], g_ref[...], m[3 * j:3 * j + 1], m[3 * j + 1:3 * j + 2]).astype(BF16)
    u = _dot(h, w_ref[...])
    for (lo, hi, scale), o_ref in zip(splits, o_refs):
        piece = u[:, lo:hi]
        if scale != 1.0:
            piece = piece * scale
        o_ref[...] = piece.astype(o_ref.dtype)


def _in_proj(x, mod, layer, seq_len, g, w, outs):
    n, d = x.shape
    tm = PROJ_TILE
    splits, lo = [], 0
    for width, _, scale in outs:
        splits.append((lo, lo + width, scale))
        lo += width
    return pl.pallas_call(
        functools.partial(_proj_kernel, 1, tuple(splits)),
        grid=(n // tm,),
        in_specs=[
            pl.BlockSpec((tm, d), lambda i: (i, 0)),
            _mod_spec(layer, seq_len, tm),
            _resident((1, d)),
            _resident(w.shape),
        ],
        out_specs=[pl.BlockSpec((tm, width), lambda i: (i, 0)) for width, _, _ in outs],
        out_shape=[jax.ShapeDtypeStruct((n, width), dt) for width, dt, _ in outs],
        compiler_params=_params(1),
        name="mixer_in_proj",
    )(x, mod, g.reshape(1, d), w)


def _forget_gates(z, lb):
    g = (1.0 - lb) * _sigmoid(z)
    return (1.0 - lb) - g, jnp.log2(lb + g)


def _head_pair_attention(pairs):
    lane = lax.broadcasted_iota(jnp.int32, (1, 2 * NA_HD), 1)
    first = lane < NA_HD
    scores = []
    for q, key_blocks, _, bias_blocks in pairs:
        zero = jnp.zeros_like(q)
        qm = jnp.concatenate([jnp.where(first, q, zero), jnp.where(first, zero, q)], axis=0)
        ss = []
        for kb, bias in zip(key_blocks, bias_blocks):
            s = _dot_nt(qm, kb)
            ss.append(s if bias is None else s + bias)
        scores.append(ss)
    maxes = []
    for ss in scores:
        mx = ss[0].max(axis=-1, keepdims=True)
        for s in ss[1:]:
            mx = jnp.maximum(mx, s.max(axis=-1, keepdims=True))
        maxes.append(mx)
    outs = []
    for (q, _, value_blocks, _), ss, mx in zip(pairs, scores, maxes):
        den, acc = None, None
        for s, vb in zip(ss, value_blocks):
            p = jnp.exp(s - mx)
            dsum = p.sum(axis=-1, keepdims=True)
            pv = _dot(p.astype(BF16), vb)
            den = dsum if den is None else den + dsum
            acc = pv if acc is None else acc + pv
        o = acc / den
        tq = q.shape[0]
        outs.append(jnp.where(first, o[:tq], o[tq:]))
    return outs


def _ctx_attn_kernel(q_ref, k_ref, v_ref, o_ref):
    dests = [(i, slice(pr * 2 * NA_HD, (pr + 1) * 2 * NA_HD))
             for i in range(CTX_SEQS_PER_STEP) for pr in range(NA_HEADS // 2)]
    outs = _head_pair_attention([(q_ref[i, :, sl], [k_ref[i, :, sl].astype(BF16)],
                                  [v_ref[i, :, sl].astype(BF16)], [None]) for i, sl in dests])
    for (i, sl), o in zip(dests, outs):
        o_ref[i, :, sl] = o.astype(o_ref.dtype)


def _context_attention(q, k, v):
    b, s, w = q.shape
    blk = lambda: pl.BlockSpec((CTX_SEQS_PER_STEP, s, w), lambda i: (i, 0, 0))
    return pl.pallas_call(
        _ctx_attn_kernel,
        grid=(b // CTX_SEQS_PER_STEP,),
        in_specs=[blk(), blk(), blk()],
        out_specs=blk(),
        out_shape=jax.ShapeDtypeStruct((b, s, w), BF16),
        compiler_params=_params(1),
        name="context_attention",
    )(q, k, v)


def _na_kernel(rows, q_ref, k_ref, v_ref, ck_ref, cv_ref, bias_ref, o_ref):
    slices = [slice(pr * 2 * NA_HD, (pr + 1) * 2 * NA_HD) for pr in range(NA_HEADS // 2)]
    pairs, dests = [], []
    for rr in range(NA_ROWS_PER_STEP):
        r = pl.program_id(1) * NA_ROWS_PER_STEP + rr
        r0 = jnp.clip(r - NA_WIN_R // 2, 0, rows - NA_WIN_R)
        start = pl.multiple_of(r0 * GRID_W, GRID_W)
        win = pl.ds(start, NA_WIN_R * GRID_W)
        first = (NA_WIN_R - 1) - (r - r0)
        q_rows = slice(rr * GRID_W, (rr + 1) * GRID_W)
        for pr, sl in enumerate(slices):
            bias = jnp.concatenate(
                [jnp.concatenate([bias_ref[2 * pr + hh, first + 2 * jj] for jj in range(NA_WIN_R // 2)], axis=1)
                 for hh in range(2)], axis=0)
            pairs.append((q_ref[0, q_rows, sl], [k_ref[0, win, sl], ck_ref[0, :, sl]],
                          [v_ref[0, win, sl], cv_ref[0, :, sl]], [bias, None]))
            dests.append((q_rows, sl))
    for (q_rows, sl), o in zip(dests, _head_pair_attention(pairs)):
        o_ref[0, q_rows, sl] = o.astype(o_ref.dtype)


def _na_bias_table(rpb):
    cols = np.arange(GRID_W)
    col_start = np.clip(cols - NA_WIN_C // 2, 0, GRID_W - NA_WIN_C)
    inside = (cols[None, :] >= col_start[:, None]) & (cols[None, :] < col_start[:, None] + NA_WIN_C)
    pad = GRID_W - NA_WIN_C
    period = 2 * GRID_W
    v = jnp.pad(rpb.astype(F32), ((0, 0), (0, 0), (pad, pad + 1)))
    t = jnp.tile(v, (1, 1, GRID_W))[:, :, :GRID_W * (period - 1)]
    t = t.reshape(*rpb.shape[:2], GRID_W, period - 1)[:, :, :, GRID_W - 1:]
    t = jnp.where(inside[None, None], t, MASK_VALUE)
    return jnp.concatenate([t[:, :-1], t[:, 1:]], axis=-1)


def _neighbourhood_attention(q, k, v, ck, cv, bias):
    b, t, w = q.shape
    rows = t // GRID_W
    p = ck.shape[1]
    seq = lambda n: pl.BlockSpec((1, n, w), lambda i, r: (i, 0, 0))
    row = lambda: pl.BlockSpec((1, NA_ROWS_PER_STEP * GRID_W, w), lambda i, r: (i, r, 0))
    return pl.pallas_call(
        functools.partial(_na_kernel, rows),
        grid=(b, rows // NA_ROWS_PER_STEP),
        in_specs=[row(), seq(t), seq(t), seq(p), seq(p), _resident(bias.shape)],
        out_specs=row(),
        out_shape=jax.ShapeDtypeStruct((b, t, w), BF16),
        compiler_params=_params(2),
        name="neighbourhood_attention",
    )(q, k, v, ck, cv, bias)


def _dft_tables(t):
    def cs(n):
        idx = (np.arange(n)[:, None] * np.arange(n)[None, :]) % n
        ang = 2.0 * np.pi * idx.astype(np.float64) / n
        return np.cos(ang), np.sin(ang)
    ct, st = cs(t)
    cg, sg = cs(FN_GW)
    return (np.concatenate([ct, -st], axis=1).astype(np.float32),
            np.concatenate([cg, sg], axis=1).astype(np.float32))


def _fnet_kernel(t, seqs, scale, f_ref, cgsg_ref, dft_ref, o_ref, xcs_ref):
    first = pl.program_id(1) * seqs

    @pl.when(pl.program_id(0) == 0)
    def _():
        for s in range(seqs):
            for g in range(FN_GROUPS):
                sl = slice(g * FN_GW, (g + 1) * FN_GW)
                xcs = _dot(f_ref[s, :, sl], cgsg_ref[...])
                xcs_ref[first + s, 0:t, sl] = xcs[:, :FN_GW].astype(BF16)
                xcs_ref[first + s, t:2 * t, sl] = xcs[:, FN_GW:].astype(BF16)

    for s in range(seqs):
        o_ref[s] = (_dot(dft_ref[...], xcs_ref[first + s]) * scale).astype(o_ref.dtype)


def _fourier_mix(f, dft, cgsg):
    b, t, w = f.shape
    tq = min(t, TOKEN_TILE)
    seqs = max(1, min(b, FN_ROWS_PER_STEP // t))
    groups = b // seqs
    scale = float(1.0 / np.sqrt(float(t * FN_GW)))
    return pl.pallas_call(
        functools.partial(_fnet_kernel, t, seqs, scale),
        grid=(t // tq, groups),
        in_specs=[
            pl.BlockSpec((seqs, t, w), lambda r, i: (jnp.where(r == 0, i, groups - 1), 0, 0)),
            pl.BlockSpec(cgsg.shape, lambda r, i: (0, 0)),
            pl.BlockSpec((tq, 2 * t), lambda r, i: (r, 0)),
        ],
        out_specs=pl.BlockSpec((seqs, tq, w), lambda r, i: (i, r, 0)),
        out_shape=jax.ShapeDtypeStruct((b, t, w), BF16),
        scratch_shapes=[pltpu.VMEM((b, 2 * t, w), BF16)],
        compiler_params=_params(2),
        name="fourier_mix",
    )(f, cgsg, dft)


def _ref_rows(b, b_ref, m, reverse):
    c = SCAN_CHUNK
    blk = 2 * m
    anchor = m if reverse else m - 1
    if blk >= 8:
        pieces = [jnp.broadcast_to(b_ref[p * blk + anchor:p * blk + anchor + 1, :], (blk, HG_DK))
                  for p in range(c // blk)]
        return pieces[0] if len(pieces) == 1 else jnp.concatenate(pieces, axis=0)
    pos = lax.broadcasted_iota(jnp.int32, (c, HG_DK), 0) % blk
    r = b
    for p in range(blk):
        delta = anchor - p
        if delta != 0:
            r = jnp.where(pos == p, pltpu.roll(b, (-delta) % c, 0), r)
    return r


def _level_masks():
    c = SCAN_CHUNK
    t = np.arange(c)
    left, right = [], []
    for m in SMALL_LEVELS:
        is_right = (t // m) % 2 == 1
        left.append(np.where(is_right, NEG_BIG, 0.0))
        right.append(np.where(is_right, 0.0, NEG_BIG))
    full = lambda rows: np.ascontiguousarray(
        np.broadcast_to(np.stack(rows)[:, :, None], (len(rows), c, HG_DK))).astype(np.float32)
    return full(left), full(right)


def _tri_tables():
    t = np.arange(SCAN_CHUNK)
    return np.stack([t[None, :] <= t[:, None], t[None, :] >= t[:, None]]).astype(np.float32)


def _scan_gates(z, lb, tri):
    k, log2_f = _forget_gates(z, lb)
    b = None
    for part in _split2(log2_f):
        t = _dot(tri, part)
        b = t if b is None else b + t
    return k, b, jnp.min(log2_f)


def _scan_chunks(chains, ml_ref, mr_ref, fast):
    c = SCAN_CHUNK
    n_tiles = c // SUBLANES
    tile = lambda x, i: x[i * SUBLANES:(i + 1) * SUBLANES]
    row = lax.broadcasted_iota(jnp.int32, (c, c), 0)
    col = lax.broadcasted_iota(jnp.int32, (c, c), 1)
    xr = row ^ col
    zero_tile = jnp.zeros((SUBLANES, HG_DK), F32)

    os_, b_tots = [], []
    for reverse, q, v, k, b, st, b_ref in chains:
        edge = 0 if reverse else c - 1
        b_tots.append(b_ref[edge:edge + 1, :])
        os_.append(_dot_nt((q * jnp.exp2(b)).astype(BF16), st.astype(BF16)))

    a_tiles = [[None] * n_tiles for _ in chains]
    m = c // 2
    while m >= (FAST_BLOCK if fast else SUBLANES):
        g = m // SUBLANES
        for ci, (reverse, q, v, k, b, st, b_ref) in enumerate(chains):
            q_ids, qe_tiles, ke_tiles = [], [], [zero_tile] * n_tiles
            for p in range(n_tiles // (2 * g)):
                left = range(2 * g * p, 2 * g * p + g)
                right = range(2 * g * p + g, 2 * g * (p + 1))
                anchor = (2 * g * p + g) * SUBLANES - (0 if reverse else 1)
                r = jnp.broadcast_to(b_ref[anchor:anchor + 1, :], (SUBLANES, HG_DK))
                q_side, k_side = (left, right) if reverse else (right, left)
                for i in q_side:
                    q_ids.append(i)
                    qe_tiles.append(tile(q, i) * jnp.exp2(tile(b, i) - r))
                for i in k_side:
                    ke_tiles[i] = tile(k, i) * jnp.exp2(r - tile(b, i))
            lvl = _dot_nt(jnp.concatenate(qe_tiles, axis=0).astype(BF16),
                          jnp.concatenate(ke_tiles, axis=0).astype(BF16))
            for j, i in enumerate(q_ids):
                old = a_tiles[ci][i]
                a_tiles[ci][i] = tile(lvl, j) if old is None else jnp.where(tile(xr, i) < 2 * m, tile(lvl, j), old)
        m //= 2
    a_s = [jnp.concatenate([t if t is not None else zero_tile for t in tiles], axis=0) for tiles in a_tiles]

    if fast:
        for ci, (reverse, q, v, k, b, st, b_ref) in enumerate(chains):
            mid = FAST_BLOCK // 2
            r = jnp.concatenate(
                [jnp.broadcast_to(b_ref[j * FAST_BLOCK + mid:j * FAST_BLOCK + mid + 1, :], (FAST_BLOCK, HG_DK))
                 for j in range(c // FAST_BLOCK)], axis=0)
            qe = (q * jnp.exp2(b - r)).astype(BF16)
            ke = (k * jnp.exp2(r - b)).astype(BF16)
            a_s[ci] = jnp.where(xr < FAST_BLOCK, _dot_nt(qe, ke), a_s[ci])
    else:
        for level, m in enumerate(SMALL_LEVELS):
            for ci, (reverse, q, v, k, b, st, b_ref) in enumerate(chains):
                d = b - _ref_rows(b, b_ref, m, reverse)
                mask_q, mask_k = (ml_ref, mr_ref) if reverse else (mr_ref, ml_ref)
                qe = (q * jnp.exp2(d + mask_q[level])).astype(BF16)
                ke = (k * jnp.exp2(mask_k[level] - d)).astype(BF16)
                a_s[ci] = jnp.where(xr < 2 * m, _dot_nt(qe, ke), a_s[ci])

    out = []
    for (reverse, q, v, k, b, st, b_ref), a, o, b_tot in zip(chains, a_s, os_, b_tots):
        if fast:
            a = jnp.where((col >= row) if reverse else (col <= row), a, 0.0)
        else:
            diag = jnp.where(xr == 0, _dot_nt(q.astype(BF16), k.astype(BF16)), 0.0)
            a = jnp.where((col > row) if reverse else (col < row), a, diag)
        o = o + _dot(a.astype(BF16), v.astype(BF16))
        k_end = (k * jnp.exp2(b_tot - b)).astype(BF16)
        st_new = st * jnp.exp2(b_tot) + _dot(v.astype(F32).T.astype(BF16), k_end)
        out.append((o, st_new))
    return out


def _hgrn_kernel(layer, n_chunks, n_heads, split_views, has_s0, *refs):
    refs = list(refs)
    if split_views:
        qf_ref, qb_ref, zf_ref, zb_ref, vf_ref, vb_ref = refs[:6]
        del refs[:6]
    else:
        qf_ref, zf_ref, zb_ref, vf_ref = refs[:4]
        qb_ref, vb_ref = qf_ref, vf_ref
        del refs[:4]
    lg_ref, ml_ref, mr_ref, tri_ref = refs[:4]
    del refs[:4]
    s0_ref = refs.pop(0) if has_s0 else None
    o_out = refs[:2]
    sfin_ref = refs[2]
    del refs[:3]
    n_chains = 2 * n_heads
    st_refs, b_refs, k_refs = (refs[i * n_chains:(i + 1) * n_chains] for i in range(3))
    fast_ref = refs[3 * n_chains]
    chains = [(hh, direction) for hh in range(n_heads) for direction in range(2)]
    q_refs, z_refs, v_refs = (qf_ref, qb_ref), (zf_ref, zb_ref), (vf_ref, vb_ref)
    lanes = lambda hh: slice(hh * HG_DK, (hh + 1) * HG_DK)

    @pl.when(pl.program_id(2) == 0)
    def _():
        for ci, (hh, direction) in enumerate(chains):
            if has_s0:
                st_refs[ci][...] = s0_ref[0, direction, hh].T
            else:
                st_refs[ci][...] = jnp.zeros((HG_DV, HG_DK), F32)

    def lower_bound(hh, direction):
        lg = lg_ref[:, direction, hh, 0, :]
        ex = jnp.exp(lg - lg.max(axis=0, keepdims=True))
        return ex[1:layer + 1].sum(axis=0, keepdims=True) / ex.sum(axis=0, keepdims=True)

    lbs = [lower_bound(hh, direction) for hh, direction in chains]

    def chunk_rows(c):
        return pl.ds(pl.multiple_of(c * SCAN_CHUNK, SCAN_CHUNK), SCAN_CHUNK)

    def gates(c_fwd, c_bwd):
        rows = (chunk_rows(c_fwd), chunk_rows(c_bwd))
        return [_scan_gates(z_refs[direction][rows[direction], lanes(hh)], lb, tri_ref[direction])
                for (hh, direction), lb in zip(chains, lbs)]

    def store_gates(gs):
        lo = None
        for (k, b, lo_c), k_ref, b_ref in zip(gs, k_refs, b_refs):
            k_ref[...] = k
            b_ref[...] = b
            lo = lo_c if lo is None else jnp.minimum(lo, lo_c)
        fast_ref[0] = (lo >= -FAST_LOG2_LIMIT).astype(jnp.int32)

    store_gates(gates(0, n_chunks - 1))

    def trip(c, fast, last):
        rows = (chunk_rows(c), chunk_rows(n_chunks - 1 - c))
        nxt = None if last else gates(c + 1, n_chunks - 2 - c)
        outs = _scan_chunks(
            [(direction == 1, q_refs[direction][rows[direction], lanes(hh)],
              v_refs[direction][rows[direction], lanes(hh)],
              k_refs[ci][...], b_refs[ci][...], st_refs[ci][...], b_refs[ci])
             for ci, (hh, direction) in enumerate(chains)],
            ml_ref, mr_ref, fast)
        for ci, ((hh, direction), (o, st)) in enumerate(zip(chains, outs)):
            st_refs[ci][...] = st
            o_out[direction][rows[direction], lanes(hh)] = o
        if not last:
            store_gates(nxt)

    def guarded_trip(c, last):
        use_fast = fast_ref[0] == 1
        pl.when(use_fast)(lambda: trip(c, True, last))
        pl.when(jnp.logical_not(use_fast))(lambda: trip(c, False, last))

    def scan_body(c, carry):
        guarded_trip(c, False)
        return carry

    lax.fori_loop(0, n_chunks - 1, scan_body, 0)
    guarded_trip(n_chunks - 1, True)

    @pl.when(pl.program_id(2) == pl.num_programs(2) - 1)
    def _():
        for ci, (hh, direction) in enumerate(chains):
            sfin_ref[0, direction, hh] = st_refs[ci][...].T


def _hgrn_bidir(layer, seqs, q, zf, zb, v, logits, s0):
    n = q.shape[0]
    t = n // seqs
    tb = min(t, SCAN_TIME_BLOCK)
    n_tb = t // tb
    split = n_tb > 1
    n_chunks = tb // SCAN_CHUNK
    in_bytes = (2 if split else 1) * (4 + 2) + 2 * 4
    bytes_per_elem = in_bytes + 2 * 4
    fits = lambda heads: 2 * tb * heads * HG_DK * bytes_per_elem <= SCAN_WINDOW_BYTES
    n_heads = next((heads for heads in (8, 4, 2) if heads <= SCAN_HEADS_MAX and fits(heads)), 1)
    n_chains = 2 * n_heads
    ml, mr = _level_masks()
    tri = _tri_tables()
    fwd = pl.BlockSpec((tb, n_heads * HG_DK), lambda s, h, j: (s * n_tb + j, h))
    rev = pl.BlockSpec((tb, n_heads * HG_DK), lambda s, h, j: (s * n_tb + n_tb - 1 - j, h))
    state = lambda: pl.BlockSpec((1, 2, n_heads, HG_DK, HG_DV), lambda s, h, j: (s, 0, h, 0, 0))
    tok_specs, tok_args = ([fwd, rev, fwd, rev, fwd, rev], [q, q, zf, zb, v, v]) if split else \
                          ([fwd, fwd, fwd, fwd], [q, zf, zb, v])
    in_specs = tok_specs + [pl.BlockSpec((DEPTH, 2, n_heads, 1, HG_DK), lambda s, h, j: (0, 0, h, 0, 0)),
                            _resident(ml.shape), _resident(mr.shape), _resident(tri.shape)]
    args = tok_args + [logits.reshape(DEPTH, 2, HG_HEADS, 1, HG_DK),
                       jnp.asarray(ml), jnp.asarray(mr), jnp.asarray(tri).astype(BF16)]
    if s0 is not None:
        in_specs.append(state())
        args.append(s0)
    return pl.pallas_call(
        functools.partial(_hgrn_kernel, layer, n_chunks, n_heads, split, s0 is not None),
        grid=(seqs, HG_HEADS // n_heads, n_tb),
        in_specs=in_specs,
        out_specs=[fwd, rev, state()],
        out_shape=[jax.ShapeDtypeStruct((n, HG_KW), F32), jax.ShapeDtypeStruct((n, HG_KW), F32),
                   jax.ShapeDtypeStruct((seqs, 2, HG_HEADS, HG_DK, HG_DV), F32)],
        scratch_shapes=([pltpu.VMEM((HG_DV, HG_DK), F32)] * n_chains
                        + [pltpu.VMEM((SCAN_CHUNK, HG_DK), F32)] * n_chains
                        + [pltpu.VMEM((SCAN_CHUNK, HG_DK), F32)] * n_chains
                        + [pltpu.SMEM((1,), jnp.int32)]),
        compiler_params=_params(3),
        name="hgrn2_bidir",
    )(*args)


def kernel(x_prompt, x_sample, cache_k, cache_v, state_hgrn, c, c_ctx, ada_w, ada_b, norm_g, ffn_w1, ffn_w2,
           mix0_w_in, mix0_w_out, na_rpb, mix1_w_in, mix1_w_out, hg_lb_logits, hg_norm_g, norm_f):
    bp, sp, d = x_prompt.shape
    bs, ts, _ = x_sample.shape
    assert d == D_MODEL and bs <= CTX_ROW < COND_ROWS, "conditioning rows: latent requests, then the context row"
    assert ada_w.shape[0] == DEPTH and ts % GRID_W == 0 and ts // GRID_W >= NA_WIN_R
    xp = x_prompt.reshape(bp * sp, d)
    xs = x_sample.reshape(bs * ts, d)

    cond = jnp.zeros((COND_ROWS, d), F32).at[:bs].set(c).at[CTX_ROW].set(c_ctx)
    mod = _ada_mod(cond, ada_w, ada_b)

    w1 = ffn_w1.astype(BF16)
    w2 = ffn_w2.astype(BF16)

    new_k, new_v, new_s = [], [], []
    for l in range(DEPTH):
        last = l == DEPTH - 1

        def ffn(x, seq_len, j, half, final, mixed=None, scan=None):
            return _half_ffn(x, mod, l, seq_len, j, half, norm_g[l, j], w1, w2, norm_f, final, mixed, scan)

        xp = ffn(xp, None, 0, 0, False)
        xs = ffn(xs, ts, 0, 0, False)

        if l % 2 == 0:
            e = l // 2
            w_in = mix0_w_in[e].astype(BF16)
            w_out = mix0_w_out[e].astype(BF16)
            qk_scale = float(NA_HD) ** -0.5
            qp, kp, vp, fp = _in_proj(xp, mod, l, None, norm_g[l, 1], w_in,
                                      [(NA_W, BF16, qk_scale), (NA_W, F32, 1.0), (NA_W, F32, 1.0), (FN_W, BF16, 1.0)])
            qs, ks, vs, fs = _in_proj(xs, mod, l, ts, norm_g[l, 1], w_in,
                                      [(NA_W, BF16, qk_scale), (NA_W, BF16, 1.0), (NA_W, BF16, 1.0), (FN_W, BF16, 1.0)])
            new_k.append(kp.reshape(bp, sp, NA_HEADS, NA_HD))
            new_v.append(vp.reshape(bp, sp, NA_HEADS, NA_HD))

            r3 = lambda a, b_, t_: a.reshape(b_, t_, a.shape[-1])
            ap = _context_attention(r3(qp, bp, sp), r3(kp, bp, sp), r3(vp, bp, sp))
            ck = cache_k[:, e].reshape(bs, -1, NA_W).astype(BF16)
            cv = cache_v[:, e].reshape(bs, -1, NA_W).astype(BF16)
            a_s = _neighbourhood_attention(r3(qs, bs, ts), r3(ks, bs, ts), r3(vs, bs, ts), ck, cv,
                                           _na_bias_table(na_rpb[e]))
            fn = []
            for f3 in (r3(fp, bp, sp), r3(fs, bs, ts)):
                dft_np, cgsg_np = _dft_tables(f3.shape[1])
                fn.append(_fourier_mix(f3, jnp.asarray(dft_np).astype(BF16), jnp.asarray(cgsg_np).astype(BF16)))
            mixed_p = ([ap.reshape(bp * sp, NA_W), fn[0].reshape(bp * sp, FN_W)], w_out)
            mixed_s = ([a_s.reshape(bs * ts, NA_W), fn[1].reshape(bs * ts, FN_W)], w_out)
            scan_p = scan_s = None
        else:
            o = l // 2
            w_in = mix1_w_in[o].astype(BF16)
            w_out = mix1_w_out[o].astype(BF16)
            outs = [(HG_KW, F32, 1.0)] * 3 + [(HG_KW, BF16, 1.0), (HG_KW, F32, 1.0)]
            qp, zfp, zbp, vp, gate_p = _in_proj(xp, mod, l, None, norm_g[l, 1], w_in, outs)
            qs, zfs, zbs, vs, gate_s = _in_proj(xs, mod, l, ts, norm_g[l, 1], w_in, outs)
            ofp, obp, s_fin = _hgrn_bidir(l, bp, qp, zfp, zbp, vp, hg_lb_logits, None)
            ofs, obs, _ = _hgrn_bidir(l, bs, qs, zfs, zbs, vs, hg_lb_logits, state_hgrn[:, o])
            new_s.append(s_fin)
            mixed_p = mixed_s = None
            scan_p = (ofp, obp, gate_p, hg_norm_g[o], w_out)
            scan_s = (ofs, obs, gate_s, hg_norm_g[o], w_out)

        xp = ffn(xp, None, 2, 1, last, mixed_p, scan_p)
        xs = ffn(xs, ts, 2, 1, last, mixed_s, scan_s)

    stack = lambda parts: jnp.expand_dims(parts[0], 1) if len(parts) == 1 else jnp.stack(parts, axis=1)
    return (xp.reshape(bp, sp, d), xs.reshape(bs, ts, d), stack(new_k), stack(new_v), stack(new_s))
```

```python
import functools

import numpy as np
import jax
import jax.numpy as jnp
from jax import lax
from jax.experimental import pallas as pl
from jax.experimental.pallas import tpu as pltpu

F32 = jnp.float32
BF16 = jnp.bfloat16

D_MODEL = 1024
DEPTH = 2
GRID_W = 64
NA_HEADS = 8
NA_HD = 64
NA_W = NA_HEADS * NA_HD
NA_WIN_R = 8
NA_WIN_C = 16
CTX_SEQS_PER_STEP = 2
NA_ROWS_PER_STEP = 4
FN_GROUPS = 4
FN_GW = 128
FN_W = FN_GROUPS * FN_GW
FN_ROWS_PER_STEP = 1024
HG_HEADS = 8
HG_DK = 128
HG_DV = 128
HG_KW = HG_HEADS * HG_DK
D_FF = 2816
N_MOD = 9
EPS = 1e-6

COND_ROWS = 8
CTX_ROW = 4
ADA_TILE_N = 1152
TOKEN_TILE = 512
PROJ_TILE = 512
SCAN_CHUNK = 128
SUBLANES = 8
SMALL_LEVELS = (4, 2, 1)
SCAN_HEADS_MAX = 8
SCAN_TIME_BLOCK = 1024
SCAN_WINDOW_BYTES = 40 * 1024 * 1024
FAST_BLOCK = 32
FAST_LOG2_LIMIT = 125.0 / (FAST_BLOCK // 2)
MASK_VALUE = -1e30
NEG_BIG = -1e30
VMEM_LIMIT = 56 * 1024 * 1024


def _params(n_axes, vmem=VMEM_LIMIT):
    return pltpu.CompilerParams(dimension_semantics=("arbitrary",) * n_axes, vmem_limit_bytes=vmem)


def _resident(shape):
    nd = len(shape)
    return pl.BlockSpec(shape, lambda *_: (0,) * nd, pipeline_mode=pl.Buffered(1))


def _sigmoid(a):
    return 1.0 / (1.0 + jnp.exp(-a))


def _rms_mod(x, g, shift, scale):
    y = x * lax.rsqrt(jnp.mean(x * x, axis=-1, keepdims=True) + EPS)
    return (y * g) * (1.0 + scale) + shift


def _split2(x):
    hi = x.astype(BF16)
    lo = (x - hi.astype(F32)).astype(BF16)
    return hi, lo


def _dot(a, b):
    return jnp.dot(a, b, preferred_element_type=F32)


def _dot_nt(a, b):
    return lax.dot_general(a, b, (((1,), (1,)), ((), ())), preferred_element_type=F32)


def _ada_kernel(cond_ref, w_ref, b_ref, o_ref):
    c = cond_ref[...]
    s = c * _sigmoid(c)
    s_hi, s_lo = _split2(s)
    w_hi, w_lo = _split2(w_ref[0])
    o_ref[0] = _dot(s_hi, w_hi) + _dot(s_hi, w_lo) + _dot(s_lo, w_hi) + b_ref[0]


def _ada_mod(cond, ada_w, ada_b):
    depth, d, n = ada_w.shape
    tn = ADA_TILE_N
    out = pl.pallas_call(
        _ada_kernel,
        grid=(depth, n // tn),
        in_specs=[
            pl.BlockSpec((COND_ROWS, d), lambda l, j: (0, 0)),
            pl.BlockSpec((1, d, tn), lambda l, j: (l, 0, j)),
            pl.BlockSpec((1, 1, tn), lambda l, j: (l, 0, j)),
        ],
        out_specs=pl.BlockSpec((1, COND_ROWS, tn), lambda l, j: (l, 0, j)),
        out_shape=jax.ShapeDtypeStruct((depth, COND_ROWS, n), F32),
        compiler_params=_params(2),
        name="ada_mod",
    )(cond, ada_w, ada_b.reshape(depth, 1, n))
    return out.reshape(depth, COND_ROWS, N_MOD, d)


def _mod_spec(layer, seq_len, tm):
    if seq_len is None:
        return pl.BlockSpec((1, 1, N_MOD, D_MODEL), lambda i: (layer, CTX_ROW, 0, 0))
    tiles_per_seq = seq_len // tm
    return pl.BlockSpec((1, 1, N_MOD, D_MODEL), lambda i: (layer, i // tiles_per_seq, 0, 0))


def _ffn_kernel(j, final_norm, n_acts, scan_mix, x_ref, mod_ref, g_ref, w1a_ref, w1b_ref, w2_ref, gf_ref, *refs):
    x = x_ref[...]
    m = mod_ref[0, 0]
    if scan_mix:
        of_ref, ob_ref, gate_ref, ng_ref, w_ref, o_ref = refs
        o = of_ref[...] + ob_ref[...]
        ng = ng_ref[...]
        heads = []
        for hh in range(HG_HEADS):
            oh = o[:, hh * HG_DV:(hh + 1) * HG_DV]
            heads.append(oh * lax.rsqrt(jnp.mean(oh * oh, axis=-1, keepdims=True) + EPS) * ng)
        gate = gate_ref[...]
        y = (jnp.concatenate(heads, axis=1) * (gate * _sigmoid(gate))).astype(BF16)
        x = x + m[5:6] * _dot(y, w_ref[...])
    else:
        a_refs, w_refs, o_ref = refs[:n_acts], refs[n_acts:2 * n_acts], refs[2 * n_acts]
        if n_acts:
            y = None
            for a_ref, w_ref in zip(a_refs, w_refs):
                t = _dot(a_ref[...], w_ref[...])
                y = t if y is None else y + t
            x = x + m[5:6] * y
    h = _rms_mod(x, g_ref[...], m[3 * j:3 * j + 1], m[3 * j + 1:3 * j + 2]).astype(BF16)
    a = _dot(h, w1a_ref[...])
    b = _dot(h, w1b_ref[...])
    act = (a * _sigmoid(a) * b).astype(BF16)
    y = x + (0.5 * m[3 * j + 2:3 * j + 3]) * _dot(act, w2_ref[...])
    if final_norm:
        y = y * lax.rsqrt(jnp.mean(y * y, axis=-1, keepdims=True) + EPS) * gf_ref[...]
    o_ref[...] = y


def _half_ffn(x, mod, layer, seq_len, j, half, g, w1, w2, gf, final_norm, mixed=None, scan=None):
    n, d = x.shape
    tm = TOKEN_TILE
    once = pl.Buffered(1)
    acts, w_out = mixed if mixed is not None else ((), None)
    width = acts[0].shape[1] if acts else 0
    mixed_specs = ([pl.BlockSpec((tm, width), lambda i: (i, 0)) for _ in acts]
                   + [pl.BlockSpec((width, d), functools.partial(lambda i, blk: (blk, 0), blk=blk),
                                   pipeline_mode=once) for blk in range(len(acts))])
    mixed_args = [*acts, *([w_out] * len(acts))]
    if scan is not None:
        o_fwd, o_bwd, gate, head_g, w_scan = scan
        mixed_specs = ([pl.BlockSpec((tm, d), lambda i: (i, 0)) for _ in range(3)]
                       + [_resident((1, HG_DV)), _resident(w_scan.shape)])
        mixed_args = [o_fwd, o_bwd, gate, head_g.reshape(1, HG_DV), w_scan]
    return pl.pallas_call(
        functools.partial(_ffn_kernel, j, final_norm, len(acts), scan is not None),
        grid=(n // tm,),
        in_specs=[
            pl.BlockSpec((tm, d), lambda i: (i, 0)),
            _mod_spec(layer, seq_len, tm),
            _resident((1, d)),
            pl.BlockSpec((None, None, d, D_FF), lambda i: (layer, half, 0, 0), pipeline_mode=once),
            pl.BlockSpec((None, None, d, D_FF), lambda i: (layer, half, 0, 1), pipeline_mode=once),
            pl.BlockSpec((None, None, D_FF, d), lambda i: (layer, half, 0, 0), pipeline_mode=once),
            _resident((1, d)),
        ] + mixed_specs,
        out_specs=pl.BlockSpec((tm, d), lambda i: (i, 0)),
        out_shape=jax.ShapeDtypeStruct((n, d), F32),
        compiler_params=_params(1),
        name="half_ffn",
    )(x, mod, g.reshape(1, d), w1, w1, w2, gf.reshape(1, d), *mixed_args)


def _proj_kernel(j, splits, head_splits, x_ref, mod_ref, g_ref, w_ref, *o_refs):
    m = mod_ref[0, 0]
    h = _rms_mod(x_ref[...], g_ref[...], m[3 * j:3 * j + 1], m[3 * j + 1:3 * j + 2]).astype(BF16)
    u = _dot(h, w_ref[...])
    for (lo, hi, scale), o_ref in zip(splits, o_refs):
        piece = u[:, lo:hi]
        if scale != 1.0:
            piece = piece * scale
        o_ref[...] = piece.astype(o_ref.dtype)
    for (lo, hi), o_ref in zip(head_splits, o_refs[len(splits):]):
        o_ref[...] = pltpu.einshape("t(hd)->thd", u[:, lo:hi], h=o_ref.shape[1]).astype(o_ref.dtype)


def _in_proj(x, mod, layer, seq_len, g, w, outs, head_outs=()):
    n, d = x.shape
    tm = PROJ_TILE
    splits, lo = [], 0
    for width, _, scale in outs:
        splits.append((lo, lo + width, scale))
        lo += width
    head_splits = tuple(splits[i][:2] for i, _ in head_outs)
    head_shapes = [(heads, outs[i][0] // heads) for i, heads in head_outs]
    return pl.pallas_call(
        functools.partial(_proj_kernel, 1, tuple(splits), head_splits),
        grid=(n // tm,),
        in_specs=[
            pl.BlockSpec((tm, d), lambda i: (i, 0)),
            _mod_spec(layer, seq_len, tm),
            _resident((1, d)),
            _resident(w.shape),
        ],
        out_specs=([pl.BlockSpec((tm, width), lambda i: (i, 0)) for width, _, _ in outs]
                   + [pl.BlockSpec((tm, hh, hd), lambda i: (i, 0, 0)) for hh, hd in head_shapes]),
        out_shape=([jax.ShapeDtypeStruct((n, width), dt) for width, dt, _ in outs]
                   + [jax.ShapeDtypeStruct((n, hh, hd), F32) for hh, hd in head_shapes]),
        compiler_params=_params(1),
        name="mixer_in_proj",
    )(x, mod, g.reshape(1, d), w)


def _forget_gates(z, lb):
    g = (1.0 - lb) * _sigmoid(z)
    return (1.0 - lb) - g, jnp.log2(lb + g)


def _head_pair_attention(pairs):
    lane = lax.broadcasted_iota(jnp.int32, (1, 2 * NA_HD), 1)
    first = lane < NA_HD
    scores = []
    for q, key_blocks, _, bias_blocks in pairs:
        zero = jnp.zeros_like(q)
        qm = jnp.concatenate([jnp.where(first, q, zero), jnp.where(first, zero, q)], axis=0)
        ss = []
        for kb, bias in zip(key_blocks, bias_blocks):
            s = _dot_nt(qm, kb)
            ss.append(s if bias is None else s + bias)
        scores.append(ss)
    maxes = []
    for ss in scores:
        mx = ss[0].max(axis=-1, keepdims=True)
        for s in ss[1:]:
            mx = jnp.maximum(mx, s.max(axis=-1, keepdims=True))
        maxes.append(mx)
    outs = []
    for (q, _, value_blocks, _), ss, mx in zip(pairs, scores, maxes):
        den, acc = None, None
        for s, vb in zip(ss, value_blocks):
            p = jnp.exp(s - mx)
            dsum = p.sum(axis=-1, keepdims=True)
            pv = _dot(p.astype(BF16), vb)
            den = dsum if den is None else den + dsum
            acc = pv if acc is None else acc + pv
        o = acc / den
        tq = q.shape[0]
        outs.append(jnp.where(first, o[:tq], o[tq:]))
    return outs


def _ctx_attn_kernel(q_ref, k_ref, v_ref, o_ref):
    dests = [(i, slice(pr * 2 * NA_HD, (pr + 1) * 2 * NA_HD))
             for i in range(CTX_SEQS_PER_STEP) for pr in range(NA_HEADS // 2)]
    outs = _head_pair_attention([(q_ref[i, :, sl], [k_ref[i, :, sl].astype(BF16)],
                                  [v_ref[i, :, sl].astype(BF16)], [None]) for i, sl in dests])
    for (i, sl), o in zip(dests, outs):
        o_ref[i, :, sl] = o.astype(o_ref.dtype)


def _context_attention(q, k, v):
    b, s, w = q.shape
    blk = lambda: pl.BlockSpec((CTX_SEQS_PER_STEP, s, w), lambda i: (i, 0, 0))
    return pl.pallas_call(
        _ctx_attn_kernel,
        grid=(b // CTX_SEQS_PER_STEP,),
        in_specs=[blk(), blk(), blk()],
        out_specs=blk(),
        out_shape=jax.ShapeDtypeStruct((b, s, w), BF16),
        compiler_params=_params(1),
        name="context_attention",
    )(q, k, v)


def _na_kernel(rows, q_ref, k_ref, v_ref, ck_ref, cv_ref, bias_ref, o_ref):
    slices = [slice(pr * 2 * NA_HD, (pr + 1) * 2 * NA_HD) for pr in range(NA_HEADS // 2)]
    pairs, dests = [], []
    for rr in range(NA_ROWS_PER_STEP):
        r = pl.program_id(1) * NA_ROWS_PER_STEP + rr
        r0 = jnp.clip(r - NA_WIN_R // 2, 0, rows - NA_WIN_R)
        start = pl.multiple_of(r0 * GRID_W, GRID_W)
        win = pl.ds(start, NA_WIN_R * GRID_W)
        first = (NA_WIN_R - 1) - (r - r0)
        q_rows = slice(rr * GRID_W, (rr + 1) * GRID_W)
        for pr, sl in enumerate(slices):
            bias = jnp.concatenate(
                [jnp.concatenate([bias_ref[2 * pr + hh, first + 2 * jj] for jj in range(NA_WIN_R // 2)], axis=1)
                 for hh in range(2)], axis=0)
            pairs.append((q_ref[0, q_rows, sl], [k_ref[0, win, sl], ck_ref[0, :, sl]],
                          [v_ref[0, win, sl], cv_ref[0, :, sl]], [bias, None]))
            dests.append((q_rows, sl))
    for (q_rows, sl), o in zip(dests, _head_pair_attention(pairs)):
        o_ref[0, q_rows, sl] = o.astype(o_ref.dtype)


def _na_bias_table(rpb):
    cols = np.arange(GRID_W)
    col_start = np.clip(cols - NA_WIN_C // 2, 0, GRID_W - NA_WIN_C)
    inside = (cols[None, :] >= col_start[:, None]) & (cols[None, :] < col_start[:, None] + NA_WIN_C)
    pad = GRID_W - NA_WIN_C
    period = 2 * GRID_W
    v = jnp.pad(rpb.astype(F32), ((0, 0), (0, 0), (pad, pad + 1)))
    t = jnp.tile(v, (1, 1, GRID_W))[:, :, :GRID_W * (period - 1)]
    t = t.reshape(*rpb.shape[:2], GRID_W, period - 1)[:, :, :, GRID_W - 1:]
    t = jnp.where(inside[None, None], t, MASK_VALUE)
    return jnp.concatenate([t[:, :-1], t[:, 1:]], axis=-1)


def _neighbourhood_attention(q, k, v, ck, cv, bias):
    b, t, w = q.shape
    rows = t // GRID_W
    p = ck.shape[1]
    seq = lambda n: pl.BlockSpec((1, n, w), lambda i, r: (i, 0, 0))
    row = lambda: pl.BlockSpec((1, NA_ROWS_PER_STEP * GRID_W, w), lambda i, r: (i, r, 0))
    return pl.pallas_call(
        functools.partial(_na_kernel, rows),
        grid=(b, rows // NA_ROWS_PER_STEP),
        in_specs=[row(), seq(t), seq(t), seq(p), seq(p), _resident(bias.shape)],
        out_specs=row(),
        out_shape=jax.ShapeDtypeStruct((b, t, w), BF16),
        compiler_params=_params(2),
        name="neighbourhood_attention",
    )(q, k, v, ck, cv, bias)


def _dft_tables(t):
    def cs(n):
        idx = (np.arange(n)[:, None] * np.arange(n)[None, :]) % n
        ang = 2.0 * np.pi * idx.astype(np.float64) / n
        return np.cos(ang), np.sin(ang)
    ct, st = cs(t)
    cg, sg = cs(FN_GW)
    return (np.concatenate([ct, -st], axis=1).astype(np.float32),
            np.concatenate([cg, sg], axis=1).astype(np.float32))


def _fnet_kernel(t, seqs, scale, f_ref, cgsg_ref, dft_ref, o_ref, xcs_ref):
    first = pl.program_id(1) * seqs

    @pl.when(pl.program_id(0) == 0)
    def _():
        for s in range(seqs):
            for g in range(FN_GROUPS):
                sl = slice(g * FN_GW, (g + 1) * FN_GW)
                xcs = _dot(f_ref[s, :, sl], cgsg_ref[...])
                xcs_ref[first + s, 0:t, sl] = xcs[:, :FN_GW].astype(BF16)
                xcs_ref[first + s, t:2 * t, sl] = xcs[:, FN_GW:].astype(BF16)

    for s in range(seqs):
        o_ref[s] = (_dot(dft_ref[...], xcs_ref[first + s]) * scale).astype(o_ref.dtype)


def _fourier_mix(f, dft, cgsg):
    b, t, w = f.shape
    tq = min(t, TOKEN_TILE)
    seqs = max(1, min(b, FN_ROWS_PER_STEP // t))
    groups = b // seqs
    scale = float(1.0 / np.sqrt(float(t * FN_GW)))
    return pl.pallas_call(
        functools.partial(_fnet_kernel, t, seqs, scale),
        grid=(t // tq, groups),
        in_specs=[
            pl.BlockSpec((seqs, t, w), lambda r, i: (jnp.where(r == 0, i, groups - 1), 0, 0)),
            pl.BlockSpec(cgsg.shape, lambda r, i: (0, 0)),
            pl.BlockSpec((tq, 2 * t), lambda r, i: (r, 0)),
        ],
        out_specs=pl.BlockSpec((seqs, tq, w), lambda r, i: (i, r, 0)),
        out_shape=jax.ShapeDtypeStruct((b, t, w), BF16),
        scratch_shapes=[pltpu.VMEM((b, 2 * t, w), BF16)],
        compiler_params=_params(2),
        name="fourier_mix",
    )(f, cgsg, dft)


def _ref_rows(b, b_ref, m, reverse):
    c = SCAN_CHUNK
    blk = 2 * m
    anchor = m if reverse else m - 1
    if blk >= 8:
        pieces = [jnp.broadcast_to(b_ref[p * blk + anchor:p * blk + anchor + 1, :], (blk, HG_DK))
                  for p in range(c // blk)]
        return pieces[0] if len(pieces) == 1 else jnp.concatenate(pieces, axis=0)
    pos = lax.broadcasted_iota(jnp.int32, (c, HG_DK), 0) % blk
    r = b
    for p in range(blk):
        delta = anchor - p
        if delta != 0:
            r = jnp.where(pos == p, pltpu.roll(b, (-delta) % c, 0), r)
    return r


def _level_masks():
    c = SCAN_CHUNK
    t = np.arange(c)
    left, right = [], []
    for m in SMALL_LEVELS:
        is_right = (t // m) % 2 == 1
        left.append(np.where(is_right, NEG_BIG, 0.0))
        right.append(np.where(is_right, 0.0, NEG_BIG))
    full = lambda rows: np.ascontiguousarray(
        np.broadcast_to(np.stack(rows)[:, :, None], (len(rows), c, HG_DK))).astype(np.float32)
    return full(left), full(right)


def _tri_tables():
    t = np.arange(SCAN_CHUNK)
    return np.stack([t[None, :] <= t[:, None], t[None, :] >= t[:, None]]).astype(np.float32)


def _scan_gates(z, lb, tri):
    k, log2_f = _forget_gates(z, lb)
    b = None
    for part in _split2(log2_f):
        t = _dot(tri, part)
        b = t if b is None else b + t
    return k, b, jnp.min(log2_f)


def _scan_chunks(chains, ml_ref, mr_ref, fast):
    c = SCAN_CHUNK
    n_tiles = c // SUBLANES
    tile = lambda x, i: x[i * SUBLANES:(i + 1) * SUBLANES]
    row = lax.broadcasted_iota(jnp.int32, (c, c), 0)
    col = lax.broadcasted_iota(jnp.int32, (c, c), 1)
    xr = row ^ col
    zero_tile = jnp.zeros((SUBLANES, HG_DK), F32)

    os_, b_tots = [], []
    for reverse, q, v, k, b, st, b_ref in chains:
        edge = 0 if reverse else c - 1
        b_tots.append(b_ref[edge:edge + 1, :])
        os_.append(_dot_nt((q * jnp.exp2(b)).astype(BF16), st.astype(BF16)))

    a_tiles = [[None] * n_tiles for _ in chains]
    m = c // 2
    while m >= (FAST_BLOCK if fast else SUBLANES):
        g = m // SUBLANES
        for ci, (reverse, q, v, k, b, st, b_ref) in enumerate(chains):
            q_ids, qe_tiles, ke_tiles = [], [], [zero_tile] * n_tiles
            for p in range(n_tiles // (2 * g)):
                left = range(2 * g * p, 2 * g * p + g)
                right = range(2 * g * p + g, 2 * g * (p + 1))
                anchor = (2 * g * p + g) * SUBLANES - (0 if reverse else 1)
                r = jnp.broadcast_to(b_ref[anchor:anchor + 1, :], (SUBLANES, HG_DK))
                q_side, k_side = (left, right) if reverse else (right, left)
                for i in q_side:
                    q_ids.append(i)
                    qe_tiles.append(tile(q, i) * jnp.exp2(tile(b, i) - r))
                for i in k_side:
                    ke_tiles[i] = tile(k, i) * jnp.exp2(r - tile(b, i))
            lvl = _dot_nt(jnp.concatenate(qe_tiles, axis=0).astype(BF16),
                          jnp.concatenate(ke_tiles, axis=0).astype(BF16))
            for j, i in enumerate(q_ids):
                old = a_tiles[ci][i]
                a_tiles[ci][i] = tile(lvl, j) if old is None else jnp.where(tile(xr, i) < 2 * m, tile(lvl, j), old)
        m //= 2
    a_s = [jnp.concatenate([t if t is not None else zero_tile for t in tiles], axis=0) for tiles in a_tiles]

    if fast:
        for ci, (reverse, q, v, k, b, st, b_ref) in enumerate(chains):
            mid = FAST_BLOCK // 2
            r = jnp.concatenate(
                [jnp.broadcast_to(b_ref[j * FAST_BLOCK + mid:j * FAST_BLOCK + mid + 1, :], (FAST_BLOCK, HG_DK))
                 for j in range(c // FAST_BLOCK)], axis=0)
            qe = (q * jnp.exp2(b - r)).astype(BF16)
            ke = (k * jnp.exp2(r - b)).astype(BF16)
            a_s[ci] = jnp.where(xr < FAST_BLOCK, _dot_nt(qe, ke), a_s[ci])
    else:
        for level, m in enumerate(SMALL_LEVELS):
            for ci, (reverse, q, v, k, b, st, b_ref) in enumerate(chains):
                d = b - _ref_rows(b, b_ref, m, reverse)
                mask_q, mask_k = (ml_ref, mr_ref) if reverse else (mr_ref, ml_ref)
                qe = (q * jnp.exp2(d + mask_q[level])).astype(BF16)
                ke = (k * jnp.exp2(mask_k[level] - d)).astype(BF16)
                a_s[ci] = jnp.where(xr < 2 * m, _dot_nt(qe, ke), a_s[ci])

    out = []
    for (reverse, q, v, k, b, st, b_ref), a, o, b_tot in zip(chains, a_s, os_, b_tots):
        if fast:
            a = jnp.where((col >= row) if reverse else (col <= row), a, 0.0)
        else:
            diag = jnp.where(xr == 0, _dot_nt(q.astype(BF16), k.astype(BF16)), 0.0)
            a = jnp.where((col > row) if reverse else (col < row), a, diag)
        o = o + _dot(a.astype(BF16), v.astype(BF16))
        k_end = (k * jnp.exp2(b_tot - b)).astype(BF16)
        st_new = st * jnp.exp2(b_tot) + _dot(v.astype(F32).T.astype(BF16), k_end)
        out.append((o, st_new))
    return out


def _hgrn_kernel(layer, n_chunks, n_heads, split_views, has_s0, *refs):
    refs = list(refs)
    if split_views:
        qf_ref, qb_ref, zf_ref, zb_ref, vf_ref, vb_ref = refs[:6]
        del refs[:6]
    else:
        qf_ref, zf_ref, zb_ref, vf_ref = refs[:4]
        qb_ref, vb_ref = qf_ref, vf_ref
        del refs[:4]
    lg_ref, ml_ref, mr_ref, tri_ref = refs[:4]
    del refs[:4]
    s0_ref = refs.pop(0) if has_s0 else None
    o_out = refs[:2]
    sfin_ref = refs[2]
    del refs[:3]
    n_chains = 2 * n_heads
    st_refs, b_refs, k_refs = (refs[i * n_chains:(i + 1) * n_chains] for i in range(3))
    fast_ref = refs[3 * n_chains]
    chains = [(hh, direction) for hh in range(n_heads) for direction in range(2)]
    q_refs, z_refs, v_refs = (qf_ref, qb_ref), (zf_ref, zb_ref), (vf_ref, vb_ref)
    lanes = lambda hh: slice(hh * HG_DK, (hh + 1) * HG_DK)

    @pl.when(pl.program_id(2) == 0)
    def _():
        for ci, (hh, direction) in enumerate(chains):
            if has_s0:
                st_refs[ci][...] = s0_ref[0, direction, hh].T
            else:
                st_refs[ci][...] = jnp.zeros((HG_DV, HG_DK), F32)

    def lower_bound(hh, direction):
        lg = lg_ref[:, direction, hh, 0, :]
        ex = jnp.exp(lg - lg.max(axis=0, keepdims=True))
        return ex[1:layer + 1].sum(axis=0, keepdims=True) / ex.sum(axis=0, keepdims=True)

    lbs = [lower_bound(hh, direction) for hh, direction in chains]

    def chunk_rows(c):
        return pl.ds(pl.multiple_of(c * SCAN_CHUNK, SCAN_CHUNK), SCAN_CHUNK)

    def gates(c_fwd, c_bwd):
        rows = (chunk_rows(c_fwd), chunk_rows(c_bwd))
        return [_scan_gates(z_refs[direction][rows[direction], lanes(hh)], lb, tri_ref[direction])
                for (hh, direction), lb in zip(chains, lbs)]

    def store_gates(gs):
        lo = None
        for (k, b, lo_c), k_ref, b_ref in zip(gs, k_refs, b_refs):
            k_ref[...] = k
            b_ref[...] = b
            lo = lo_c if lo is None else jnp.minimum(lo, lo_c)
        fast_ref[0] = (lo >= -FAST_LOG2_LIMIT).astype(jnp.int32)

    store_gates(gates(0, n_chunks - 1))

    def trip(c, fast, last):
        rows = (chunk_rows(c), chunk_rows(n_chunks - 1 - c))
        nxt = None if last else gates(c + 1, n_chunks - 2 - c)
        outs = _scan_chunks(
            [(direction == 1, q_refs[direction][rows[direction], lanes(hh)],
              v_refs[direction][rows[direction], lanes(hh)],
              k_refs[ci][...], b_refs[ci][...], st_refs[ci][...], b_refs[ci])
             for ci, (hh, direction) in enumerate(chains)],
            ml_ref, mr_ref, fast)
        for ci, ((hh, direction), (o, st)) in enumerate(zip(chains, outs)):
            st_refs[ci][...] = st
            o_out[direction][rows[direction], lanes(hh)] = o
        if not last:
            store_gates(nxt)

    def guarded_trip(c, last):
        use_fast = fast_ref[0] == 1
        pl.when(use_fast)(lambda: trip(c, True, last))
        pl.when(jnp.logical_not(use_fast))(lambda: trip(c, False, last))

    def scan_body(c, carry):
        guarded_trip(c, False)
        return carry

    lax.fori_loop(0, n_chunks - 1, scan_body, 0)
    guarded_trip(n_chunks - 1, True)

    @pl.when(pl.program_id(2) == pl.num_programs(2) - 1)
    def _():
        for ci, (hh, direction) in enumerate(chains):
            sfin_ref[0, direction, hh] = st_refs[ci][...].T


def _hgrn_bidir(layer, seqs, q, zf, zb, v, logits, s0):
    n = q.shape[0]
    t = n // seqs
    tb = min(t, SCAN_TIME_BLOCK)
    n_tb = t // tb
    split = n_tb > 1
    n_chunks = tb // SCAN_CHUNK
    in_bytes = (2 if split else 1) * (4 + 2) + 2 * 4
    bytes_per_elem = in_bytes + 2 * 4
    fits = lambda heads: 2 * tb * heads * HG_DK * bytes_per_elem <= SCAN_WINDOW_BYTES
    n_heads = next((heads for heads in (8, 4, 2) if heads <= SCAN_HEADS_MAX and fits(heads)), 1)
    n_chains = 2 * n_heads
    ml, mr = _level_masks()
    tri = _tri_tables()
    fwd = pl.BlockSpec((tb, n_heads * HG_DK), lambda s, h, j: (s * n_tb + j, h))
    rev = pl.BlockSpec((tb, n_heads * HG_DK), lambda s, h, j: (s * n_tb + n_tb - 1 - j, h))
    state = lambda: pl.BlockSpec((1, 2, n_heads, HG_DK, HG_DV), lambda s, h, j: (s, 0, h, 0, 0))
    tok_specs, tok_args = ([fwd, rev, fwd, rev, fwd, rev], [q, q, zf, zb, v, v]) if split else \
                          ([fwd, fwd, fwd, fwd], [q, zf, zb, v])
    in_specs = tok_specs + [pl.BlockSpec((DEPTH, 2, n_heads, 1, HG_DK), lambda s, h, j: (0, 0, h, 0, 0)),
                            _resident(ml.shape), _resident(mr.shape), _resident(tri.shape)]
    args = tok_args + [logits.reshape(DEPTH, 2, HG_HEADS, 1, HG_DK),
                       jnp.asarray(ml), jnp.asarray(mr), jnp.asarray(tri).astype(BF16)]
    if s0 is not None:
        in_specs.append(state())
        args.append(s0)
    return pl.pallas_call(
        functools.partial(_hgrn_kernel, layer, n_chunks, n_heads, split, s0 is not None),
        grid=(seqs, HG_HEADS // n_heads, n_tb),
        in_specs=in_specs,
        out_specs=[fwd, rev, state()],
        out_shape=[jax.ShapeDtypeStruct((n, HG_KW), F32), jax.ShapeDtypeStruct((n, HG_KW), F32),
                   jax.ShapeDtypeStruct((seqs, 2, HG_HEADS, HG_DK, HG_DV), F32)],
        scratch_shapes=([pltpu.VMEM((HG_DV, HG_DK), F32)] * n_chains
                        + [pltpu.VMEM((SCAN_CHUNK, HG_DK), F32)] * n_chains
                        + [pltpu.VMEM((SCAN_CHUNK, HG_DK), F32)] * n_chains
                        + [pltpu.SMEM((1,), jnp.int32)]),
        compiler_params=_params(3),
        name="hgrn2_bidir",
    )(*args)


def kernel(x_prompt, x_sample, cache_k, cache_v, state_hgrn, c, c_ctx, ada_w, ada_b, norm_g, ffn_w1, ffn_w2,
           mix0_w_in, mix0_w_out, na_rpb, mix1_w_in, mix1_w_out, hg_lb_logits, hg_norm_g, norm_f):
    bp, sp, d = x_prompt.shape
    bs, ts, _ = x_sample.shape
    assert d == D_MODEL and bs <= CTX_ROW < COND_ROWS, "conditioning rows: latent requests, then the context row"
    assert ada_w.shape[0] == DEPTH and ts % GRID_W == 0 and ts // GRID_W >= NA_WIN_R
    xp = x_prompt.reshape(bp * sp, d)
    xs = x_sample.reshape(bs * ts, d)

    cond = jnp.zeros((COND_ROWS, d), F32).at[:bs].set(c).at[CTX_ROW].set(c_ctx)
    mod = _ada_mod(cond, ada_w, ada_b)

    w1 = ffn_w1.astype(BF16)
    w2 = ffn_w2.astype(BF16)

    new_k, new_v, new_s = [], [], []
    for l in range(DEPTH):
        last = l == DEPTH - 1

        def ffn(x, seq_len, j, half, final, mixed=None, scan=None):
            return _half_ffn(x, mod, l, seq_len, j, half, norm_g[l, j], w1, w2, norm_f, final, mixed, scan)

        xp = ffn(xp, None, 0, 0, False)
        xs = ffn(xs, ts, 0, 0, False)

        if l % 2 == 0:
            e = l // 2
            w_in = mix0_w_in[e].astype(BF16)
            w_out = mix0_w_out[e].astype(BF16)
            qk_scale = float(NA_HD) ** -0.5
            outs = [(NA_W, BF16, qk_scale), (NA_W, BF16, 1.0), (NA_W, BF16, 1.0), (FN_W, BF16, 1.0)]
            qp, kp, vp, fp, k_heads, v_heads = _in_proj(xp, mod, l, None, norm_g[l, 1], w_in, outs,
                                                        head_outs=((1, NA_HEADS), (2, NA_HEADS)))
            qs, ks, vs, fs = _in_proj(xs, mod, l, ts, norm_g[l, 1], w_in, outs)
            new_k.append(k_heads.reshape(bp, sp, NA_HEADS, NA_HD))
            new_v.append(v_heads.reshape(bp, sp, NA_HEADS, NA_HD))

            r3 = lambda a, b_, t_: a.reshape(b_, t_, a.shape[-1])
            ap = _context_attention(r3(qp, bp, sp), r3(kp, bp, sp), r3(vp, bp, sp))
            ck = cache_k[:, e].reshape(bs, -1, NA_W).astype(BF16)
            cv = cache_v[:, e].reshape(bs, -1, NA_W).astype(BF16)
            a_s = _neighbourhood_attention(r3(qs, bs, ts), r3(ks, bs, ts), r3(vs, bs, ts), ck, cv,
                                           _na_bias_table(na_rpb[e]))
            fn = []
            for f3 in (r3(fp, bp, sp), r3(fs, bs, ts)):
                dft_np, cgsg_np = _dft_tables(f3.shape[1])
                fn.append(_fourier_mix(f3, jnp.asarray(dft_np).astype(BF16), jnp.asarray(cgsg_np).astype(BF16)))
            mixed_p = ([ap.reshape(bp * sp, NA_W), fn[0].reshape(bp * sp, FN_W)], w_out)
            mixed_s = ([a_s.reshape(bs * ts, NA_W), fn[1].reshape(bs * ts, FN_W)], w_out)
            scan_p = scan_s = None
        else:
            o = l // 2
            w_in = mix1_w_in[o].astype(BF16)
            w_out = mix1_w_out[o].astype(BF16)
            outs = [(HG_KW, F32, 1.0)] * 3 + [(HG_KW, BF16, 1.0), (HG_KW, F32, 1.0)]
            qp, zfp, zbp, vp, gate_p = _in_proj(xp, mod, l, None, norm_g[l, 1], w_in, outs)
            qs, zfs, zbs, vs, gate_s = _in_proj(xs, mod, l, ts, norm_g[l, 1], w_in, outs)
            ofp, obp, s_fin = _hgrn_bidir(l, bp, qp, zfp, zbp, vp, hg_lb_logits, None)
            ofs, obs, _ = _hgrn_bidir(l, bs, qs, zfs, zbs, vs, hg_lb_logits, state_hgrn[:, o])
            new_s.append(s_fin)
            mixed_p = mixed_s = None
            scan_p = (ofp, obp, gate_p, hg_norm_g[o], w_out)
            scan_s = (ofs, obs, gate_s, hg_norm_g[o], w_out)

        xp = ffn(xp, None, 2, 1, last, mixed_p, scan_p)
        xs = ffn(xs, ts, 2, 1, last, mixed_s, scan_s)

    stack = lambda parts: jnp.expand_dims(parts[0], 1) if len(parts) == 1 else jnp.stack(parts, axis=1)
    return (xp.reshape(bp, sp, d), xs.reshape(bs, ts, d), stack(new_k), stack(new_v), stack(new_s))
```
